```python
import jax, jax.numpy as jnp
from jax import lax
import numpy as np

D_MODEL = 1024
BATCH = 8
SEQ = 2048
DEPTH = 4

CTX_LEN = 256
GRID_W = 64
N_MIXERS = 3
WIDTH = D_MODEL
POOL_WINDOWS = (2, 4, 8, 16)
N_POOL_GROUPS = 4
POOL_GROUP = WIDTH // N_POOL_GROUPS
HEAD_DIM = 64
N_HEADS = WIDTH // HEAD_DIM
WIN_ROWS_MAX = 8
WIN_COLS = 16
CONV_WIDTH = 3
EPS = 1e-6
N_POOL_LAYERS = (DEPTH + 2) // 3
N_NA_LAYERS = (DEPTH + 1) // 3
N_CONV_LAYERS = DEPTH // 3

kernel_name = "hybrid_pool_natten_shortconv_dit"


def _rmsnorm(x, g):
    xf = x.astype(jnp.float32)
    y = xf * lax.rsqrt(jnp.mean(xf * xf, axis=-1, keepdims=True) + EPS)
    return (y * g.astype(jnp.float32)).astype(x.dtype)


def _modulation(cond, w, b):
    m = jax.nn.silu(cond) @ w + b
    return jnp.split(m, 3, axis=-1)


def _centred_mean(u, w):
    b_, l_, c_ = u.shape
    cs = jnp.concatenate([jnp.zeros((b_, 1, c_), jnp.float32),
                          jnp.cumsum(u.astype(jnp.float32), axis=1)], axis=1)
    t = jnp.arange(l_)
    lo = jnp.clip(t - w // 2, 0, l_)
    hi = jnp.clip(t + w // 2, 0, l_)
    cnt = (hi - lo).astype(jnp.float32)
    return ((cs[:, hi] - cs[:, lo]) / cnt[None, :, None]).astype(u.dtype)


def _pool_mixer(h, w_in, w_grp, scale, w_out):
    u, g = jnp.split(h @ w_in, 2, axis=-1)
    b_, l_, _ = u.shape
    ug = u.reshape(b_, l_, N_POOL_GROUPS, POOL_GROUP)
    pooled = jnp.stack([_centred_mean(ug[:, :, i], w) for i, w in enumerate(POOL_WINDOWS)], axis=2)
    mixed = jnp.einsum('blgc,gcd->blgd', pooled - ug, w_grp).reshape(b_, l_, WIDTH)
    return (mixed * scale * jax.nn.silu(g)) @ w_out


def _dwconv3(y, w, b):
    out = lax.conv_general_dilated(y, w[:, None, :], window_strides=(1,), padding=((1, 1),),
                                   dimension_numbers=('NWC', 'WIO', 'NWC'),
                                   feature_group_count=WIDTH)
    return out + b


def _conv_mixer(h, w_in, conv_w, conv_b, w_out):
    bg, cg, v, g = jnp.split(h @ w_in, 4, axis=-1)
    y = bg * _dwconv3(cg * v, conv_w, conv_b)
    return (y * jax.nn.silu(g)) @ w_out


def _na_mixer(h, hc, w_in, rpb, w_out, need_ctx_out):
    b_, l_, _ = h.shape
    rows = l_ // GRID_W
    wr = min(WIN_ROWS_MAX, rows)
    q, k, v, g = jnp.split(h @ w_in, 4, axis=-1)
    q = q.reshape(b_, rows, GRID_W, N_HEADS, HEAD_DIM) * HEAD_DIM ** -0.5
    k = k.reshape(b_, rows, GRID_W, N_HEADS, HEAD_DIM)
    v = v.reshape(b_, rows, GRID_W, N_HEADS, HEAD_DIM)
    if need_ctx_out:
        qc, kc, vc, gc = jnp.split(hc @ w_in, 4, axis=-1)
    else:
        kc, vc = jnp.split(hc @ w_in[:, WIDTH:3 * WIDTH], 2, axis=-1)
    n_ctx = hc.shape[1]
    kc = kc.reshape(b_, n_ctx, N_HEADS, HEAD_DIM)
    vc = vc.reshape(b_, n_ctx, N_HEADS, HEAD_DIM)

    r = jnp.arange(rows)
    row_idx = jnp.clip(r - wr // 2, 0, rows - wr)[:, None] + jnp.arange(wr)[None, :]
    col = jnp.arange(GRID_W)
    col_idx = jnp.clip(col - WIN_COLS // 2, 0, GRID_W - WIN_COLS)[:, None] + jnp.arange(WIN_COLS)[None, :]
    sel = jax.nn.one_hot(col_idx, GRID_W, dtype=h.dtype)

    kb = k[:, row_idx]
    vb = v[:, row_idx]
    s_blk = jnp.einsum('brqhd,brikhd->bhrqik', q, kb)
    s_loc = jnp.einsum('bhrqik,qjk->bhrqij', s_blk, sel).astype(jnp.float32)
    dr_idx = row_idx - r[:, None] + WIN_ROWS_MAX - 1
    dc_idx = col_idx - col[:, None] + WIN_COLS - 1
    bias = rpb[:, dr_idx[:, None, :, None], dc_idx[None, :, None, :]].astype(jnp.float32)
    s_loc = s_loc + bias[None]
    s_ctx = jnp.einsum('brqhd,bchd->bhrqc', q, kc).astype(jnp.float32)
    n_loc = wr * WIN_COLS
    logits = jnp.concatenate([s_loc.reshape(b_, N_HEADS, rows, GRID_W, n_loc), s_ctx], axis=-1)
    p = jax.nn.softmax(logits, axis=-1).astype(v.dtype)
    p_loc = p[..., :n_loc].reshape(b_, N_HEADS, rows, GRID_W, wr, WIN_COLS)
    p_ctx = p[..., n_loc:]
    p_blk = jnp.einsum('bhrqij,qjk->bhrqik', p_loc, sel)
    o = jnp.einsum('bhrqik,brikhd->brqhd', p_blk, vb) + jnp.einsum('bhrqc,bchd->brqhd', p_ctx, vc)
    y = (o.reshape(b_, l_, WIDTH) * jax.nn.silu(g)) @ w_out

    yc = None
    if need_ctx_out:
        qc = qc.reshape(b_, n_ctx, N_HEADS, HEAD_DIM) * HEAD_DIM ** -0.5
        sc = jnp.einsum('bqhd,bkhd->bhqk', qc, kc).astype(jnp.float32)
        pc = jax.nn.softmax(sc, axis=-1).astype(vc.dtype)
        oc = jnp.einsum('bhqk,bkhd->bqhd', pc, vc).reshape(b_, n_ctx, WIDTH)
        yc = (oc * jax.nn.silu(gc)) @ w_out
    return y, yc


def setup_inputs(seed: int = 0) -> dict:
    key = jax.random.key(seed)
    ks = jax.random.split(key, 20)
    nrm = jax.random.normal
    d, w = D_MODEL, WIDTH
    return {
        "x": nrm(ks[0], (BATCH, SEQ, d), jnp.float32),
        "c": nrm(ks[1], (BATCH, d), jnp.float32),
        "ctx": nrm(ks[2], (BATCH, CTX_LEN, d), jnp.float32),
        "c_ctx": nrm(ks[3], (d,), jnp.float32),
        "norm_g": 1.0 + 0.02 * nrm(ks[4], (DEPTH, d), jnp.float32),
        "ada_w": 0.5 * d ** -0.5 * nrm(ks[5], (DEPTH, d, 3 * d), jnp.float32),
        "ada_b": 0.01 * nrm(ks[6], (DEPTH, 3 * d), jnp.float32),
        "pool_w_in": d ** -0.5 * nrm(ks[7], (N_POOL_LAYERS, d, 2 * w), jnp.float32),
        "pool_w_grp": POOL_GROUP ** -0.5 * nrm(ks[8], (N_POOL_LAYERS, N_POOL_GROUPS, POOL_GROUP, POOL_GROUP), jnp.float32),
        "pool_scale": 1.0 + 0.1 * nrm(ks[9], (N_POOL_LAYERS, w), jnp.float32),
        "pool_w_out": w ** -0.5 * nrm(ks[10], (N_POOL_LAYERS, w, d), jnp.float32),
        "na_w_in": d ** -0.5 * nrm(ks[11], (N_NA_LAYERS, d, 4 * w), jnp.float32),
        "na_rpb": 0.1 * nrm(ks[12], (N_NA_LAYERS, N_HEADS, 2 * WIN_ROWS_MAX - 1, 2 * WIN_COLS - 1), jnp.float32),
        "na_w_out": w ** -0.5 * nrm(ks[13], (N_NA_LAYERS, w, d), jnp.float32),
        "conv_w_in": d ** -0.5 * nrm(ks[14], (N_CONV_LAYERS, d, 4 * w), jnp.float32),
        "conv_dw": CONV_WIDTH ** -0.5 * nrm(ks[15], (N_CONV_LAYERS, CONV_WIDTH, w), jnp.float32),
        "conv_db": 0.01 * nrm(ks[16], (N_CONV_LAYERS, w), jnp.float32),
        "conv_w_out": w ** -0.5 * nrm(ks[17], (N_CONV_LAYERS, w, d), jnp.float32),
        "final_g": 1.0 + 0.02 * nrm(ks[18], (d,), jnp.float32),
    }


def reference(x, c, ctx, c_ctx, norm_g, ada_w, ada_b, pool_w_in, pool_w_grp, pool_scale, pool_w_out,
              na_w_in, na_rpb, na_w_out, conv_w_in, conv_dw, conv_db, conv_w_out, final_g):
    last_ctx_reader = max([i for i in range(DEPTH) if i % N_MIXERS == 1], default=-1)
    for i in range(DEPTH):
        kind, j = i % N_MIXERS, i // N_MIXERS
        update_ctx = i < last_ctx_reader
        shift, scale, gate = _modulation(c, ada_w[i], ada_b[i])
        hx = _rmsnorm(x, norm_g[i]) * (1.0 + scale[:, None]) + shift[:, None]
        if kind == 1 or update_ctx:
            cshift, cscale, cgate = _modulation(c_ctx, ada_w[i], ada_b[i])
            hc = _rmsnorm(ctx, norm_g[i]) * (1.0 + cscale) + cshift
        if kind == 0:
            yx = _pool_mixer(hx, pool_w_in[j], pool_w_grp[j], pool_scale[j], pool_w_out[j])
            if update_ctx:
                yc = _pool_mixer(hc, pool_w_in[j], pool_w_grp[j], pool_scale[j], pool_w_out[j])
        elif kind == 1:
            yx, yc = _na_mixer(hx, hc, na_w_in[j], na_rpb[j], na_w_out[j], update_ctx)
        else:
            yx = _conv_mixer(hx, conv_w_in[j], conv_dw[j], conv_db[j], conv_w_out[j])
            if update_ctx:
                yc = _conv_mixer(hc, conv_w_in[j], conv_dw[j], conv_db[j], conv_w_out[j])
        x = x + gate[:, None] * yx
        if update_ctx:
            ctx = ctx + cgate * yc
    return _rmsnorm(x, final_g)
```

```python
import functools

import jax
import jax.numpy as jnp
from jax import lax
from jax.experimental import pallas as pl
from jax.experimental.pallas import tpu as pltpu

D_MODEL = 1024
WIDTH = D_MODEL
GRID_W = 64
N_MIXERS = 3
POOL_WINDOWS = (2, 4, 8, 16)
POOL_GROUP = WIDTH // len(POOL_WINDOWS)
HEAD_DIM = 64
N_HEADS = WIDTH // HEAD_DIM
WIN_ROWS = 8
WIN_COLS = 16
EPS = 1e-6

HALO = 8
LANES = 128
N_HEAD_PAIRS = WIDTH // LANES
MASK_VALUE = -1e30
VMEM_LIMIT = 56 * 1024 * 1024

F32 = jnp.float32
BF16 = jnp.bfloat16


def _silu(x):
    return x / (1.0 + jnp.exp(-x))


def _mod_norm(x, norm_g, shift, scale):
    ms = jnp.mean(x * x, axis=-1, keepdims=True)
    return (x * lax.rsqrt(ms + EPS)) * (norm_g * (1.0 + scale)) + shift


def _params(n_axes):
    return pltpu.CompilerParams(
        dimension_semantics=("arbitrary",) * n_axes, vmem_limit_bytes=VMEM_LIMIT)


def _resident(shape):
    return pl.BlockSpec(shape, lambda *_: (0,) * len(shape), pipeline_mode=pl.Buffered(1))


def _modulation_kernel(cond_ref, w_ref, b_ref, o_ref):
    s = _silu(cond_ref[...]).astype(BF16)
    o_ref[0] = jnp.dot(s, w_ref[0].astype(BF16), preferred_element_type=F32) + b_ref[0]


def _modulation(cond, ada_w, ada_b):
    depth, d, n = ada_w.shape
    rows = cond.shape[0]
    tn = 1024
    return pl.pallas_call(
        _modulation_kernel,
        grid=(depth, n // tn),
        in_specs=[
            pl.BlockSpec((rows, d), lambda i, j: (0, 0)),
            pl.BlockSpec((1, d, tn), lambda i, j: (i, 0, j)),
            pl.BlockSpec((1, 1, tn), lambda i, j: (i, 0, j)),
        ],
        out_specs=pl.BlockSpec((1, rows, tn), lambda i, j: (i, 0, j)),
        out_shape=jax.ShapeDtypeStruct((depth, rows, n), F32),
        compiler_params=_params(2),
        name="modulation",
    )(cond, ada_w, ada_b.reshape(depth, 1, n))


def _extended_rows(x_ref, xp_ref, xn_ref, tile, seq_len):
    xe = jnp.concatenate([xp_ref[0], x_ref[0], xn_ref[0]], axis=0)
    row = lax.broadcasted_iota(jnp.int32, (tile + 2 * HALO, 1), 0)
    t = row + (pl.program_id(1) * tile - HALO)
    return xe, (t >= 0) & (t < seq_len)


def _window_sum(e, width, tile):
    n = e.shape[0]
    a = e
    k = 1
    while k < width:
        a = a + pltpu.roll(a, k, axis=0)
        k *= 2
    ahead = width // 2 - 1
    if ahead:
        a = pltpu.roll(a, n - ahead, axis=0)
    return a[HALO:HALO + tile]


def _finish(x, y, gate, o_ref, fg_ref):
    out = x + gate * y
    if fg_ref is not None:
        ms = jnp.mean(out * out, axis=-1, keepdims=True)
        out = (out * lax.rsqrt(ms + EPS)) * fg_ref[...]
    o_ref[0] = out


def _pool_kernel(x_ref, xp_ref, xn_ref, mod_ref, ng_ref, win_ref, wgrp_ref, ps_ref, wout_ref,
                 *rest, tile, seq_len):
    fg_ref, o_ref = rest if len(rest) == 2 else (None, rest[0])
    shift, scale, gate = mod_ref[0, 0:1, :], mod_ref[0, 1:2, :], mod_ref[0, 2:3, :]
    xe, valid = _extended_rows(x_ref, xp_ref, xn_ref, tile, seq_len)
    he = _mod_norm(xe, ng_ref[...], shift, scale).astype(BF16)
    ug = jnp.dot(he, win_ref[...], preferred_element_type=F32)
    u = jnp.where(valid, ug[:, :WIDTH], 0.0)
    g = ug[HALO:HALO + tile, WIDTH:]

    t = lax.broadcasted_iota(jnp.int32, (tile, 1), 0) + pl.program_id(1) * tile
    mixed = []
    for gi, width in enumerate(POOL_WINDOWS):
        ue = u[:, gi * POOL_GROUP:(gi + 1) * POOL_GROUP]
        half = width // 2
        cnt = jnp.minimum(t + half, seq_len) - jnp.maximum(t - half, 0)
        pooled = _window_sum(ue, width, tile) / cnt.astype(F32)
        diff = (pooled - ue[HALO:HALO + tile]).astype(BF16)
        mixed.append(jnp.dot(diff, wgrp_ref[gi], preferred_element_type=F32))
    mixed = jnp.concatenate(mixed, axis=1)
    z = (mixed * ps_ref[...] * _silu(g)).astype(BF16)
    y = jnp.dot(z, wout_ref[...], preferred_element_type=F32)
    _finish(x_ref[0], y, gate, o_ref, fg_ref)


def _conv_kernel(x_ref, xp_ref, xn_ref, mod_ref, ng_ref, win_ref, dw_ref, db_ref, wout_ref,
                 *rest, tile, seq_len):
    fg_ref, o_ref = rest if len(rest) == 2 else (None, rest[0])
    shift, scale, gate = mod_ref[0, 0:1, :], mod_ref[0, 1:2, :], mod_ref[0, 2:3, :]
    xe, valid = _extended_rows(x_ref, xp_ref, xn_ref, tile, seq_len)
    he = _mod_norm(xe, ng_ref[...], shift, scale).astype(BF16)
    pr = jnp.dot(he, win_ref[...], preferred_element_type=F32)
    n = tile + 2 * HALO
    rows = slice(HALO, HALO + tile)
    z = jnp.where(valid, pr[:, WIDTH:2 * WIDTH] * pr[:, 2 * WIDTH:3 * WIDTH], 0.0)
    conv = (dw_ref[0:1, :] * pltpu.roll(z, 1, axis=0)[rows]
            + dw_ref[1:2, :] * z[rows]
            + dw_ref[2:3, :] * pltpu.roll(z, n - 1, axis=0)[rows]
            + db_ref[...])
    y = pr[rows, :WIDTH] * conv * _silu(pr[rows, 3 * WIDTH:])
    y = jnp.dot(y.astype(BF16), wout_ref[...], preferred_element_type=F32)
    _finish(x_ref[0], y, gate, o_ref, fg_ref)


def _mixer_layer(kernel_fn, x, mod, norm_g, weights, final_g, tile, name):
    b, seq_len, d = x.shape
    tile = min(tile, seq_len)
    halo_blocks = seq_len // HALO
    per_tile = tile // HALO
    mod_map = (lambda bi, i: (bi, 0, 0)) if mod.shape[0] > 1 else (lambda bi, i: (0, 0, 0))
    in_specs = [
        pl.BlockSpec((1, tile, d), lambda bi, i: (bi, i, 0)),
        pl.BlockSpec((1, HALO, d), lambda bi, i: (bi, jnp.maximum(i * per_tile - 1, 0), 0)),
        pl.BlockSpec((1, HALO, d),
                     lambda bi, i: (bi, jnp.minimum((i + 1) * per_tile, halo_blocks - 1), 0)),
        pl.BlockSpec((1, 3, d), mod_map),
        _resident((1, d)),
    ] + [_resident(w.shape) for w in weights]
    args = [x, x, x, mod, norm_g.reshape(1, d)] + list(weights)
    if final_g is not None:
        in_specs.append(_resident((1, d)))
        args.append(final_g.reshape(1, d))
    return pl.pallas_call(
        functools.partial(kernel_fn, tile=tile, seq_len=seq_len),
        grid=(b, seq_len // tile),
        in_specs=in_specs,
        out_specs=pl.BlockSpec((1, tile, d), lambda bi, i: (bi, i, 0)),
        out_shape=jax.ShapeDtypeStruct(x.shape, F32),
        compiler_params=_params(2),
        name=name,
    )(*args)


def _project_kernel(x_ref, mod_ref, ng_ref, w_ref, *o_refs, scales):
    shift, scale = mod_ref[0, 0:1, :], mod_ref[0, 1:2, :]
    h = _mod_norm(x_ref[0], ng_ref[...], shift, scale).astype(BF16)
    for j, (o_ref, s) in enumerate(zip(o_refs, scales)):
        p = jnp.dot(h, w_ref[:, j * WIDTH:(j + 1) * WIDTH], preferred_element_type=F32)
        if s != 1.0:
            p = p * s
        o_ref[0] = p.astype(o_ref.dtype)


def _project(x, mod, norm_g, w, out_dtypes, scales, tile, name):
    b, seq_len, d = x.shape
    tile = min(tile, seq_len)
    mod_map = (lambda bi, i: (bi, 0, 0)) if mod.shape[0] > 1 else (lambda bi, i: (0, 0, 0))
    out_spec = pl.BlockSpec((1, tile, WIDTH), lambda bi, i: (bi, i, 0))
    return pl.pallas_call(
        functools.partial(_project_kernel, scales=scales),
        grid=(b, seq_len // tile),
        in_specs=[
            pl.BlockSpec((1, tile, d), lambda bi, i: (bi, i, 0)),
            pl.BlockSpec((1, 3, d), mod_map),
            _resident((1, d)),
            _resident(w.shape),
        ],
        out_specs=[out_spec] * len(out_dtypes),
        out_shape=[jax.ShapeDtypeStruct((b, seq_len, WIDTH), dt) for dt in out_dtypes],
        compiler_params=_params(2),
        name=name,
    )(x, mod, norm_g.reshape(1, d), w)


def _attention_kernel(q_ref, k_ref, v_ref, kc_ref, vc_ref, g_ref, x_ref, mod_ref, bias_ref,
                      wout_ref, o_ref, o_scr, *, rows_per_step, n_rows):
    lane = lax.broadcasted_iota(jnp.int32, (GRID_W, LANES), 1)
    first_head = lane < HEAD_DIM
    nt = (((1,), (1,)), ((), ()))
    win_keys = WIN_ROWS * GRID_W

    def row_body(rl, carry):
        r = pl.program_id(1) * rows_per_step + rl
        start = jnp.clip(r - WIN_ROWS // 2, 0, n_rows - WIN_ROWS)
        d0 = start - r + (WIN_ROWS - 1)
        q_off = pl.multiple_of(rl * GRID_W, GRID_W)
        k_off = pl.multiple_of(start * GRID_W, GRID_W)
        for hp in range(N_HEAD_PAIRS):
            cols = slice(hp * LANES, (hp + 1) * LANES)
            qp = q_ref[0, pl.ds(q_off, GRID_W), cols]
            zero = jnp.zeros_like(qp)
            q2 = jnp.concatenate([jnp.where(first_head, qp, zero),
                                  jnp.where(first_head, zero, qp)], axis=0)
            kw = k_ref[0, pl.ds(k_off, win_keys), cols]
            vw = v_ref[0, pl.ds(k_off, win_keys), cols]
            s_loc = lax.dot_general(q2, kw, nt, preferred_element_type=F32)
            s_ctx = lax.dot_general(q2, kc_ref[0, :, cols], nt, preferred_element_type=F32)
            bias = jnp.concatenate(
                [bias_ref[hp, d0 + 2 * j] for j in range(WIN_ROWS // 2)], axis=1)
            s_loc = s_loc + bias
            m = jnp.maximum(jnp.max(s_loc, axis=1, keepdims=True),
                            jnp.max(s_ctx, axis=1, keepdims=True))
            p_loc = jnp.exp(s_loc - m)
            p_ctx = jnp.exp(s_ctx - m)
            denom = jnp.sum(p_loc, axis=1, keepdims=True) + jnp.sum(p_ctx, axis=1, keepdims=True)
            pv = (jnp.dot(p_loc.astype(BF16), vw, preferred_element_type=F32)
                  + jnp.dot(p_ctx.astype(BF16), vc_ref[0, :, cols], preferred_element_type=F32))
            pv = pv / denom
            o_scr[pl.ds(q_off, GRID_W), cols] = jnp.where(first_head, pv[:GRID_W], pv[GRID_W:])
        return carry

    lax.fori_loop(0, rows_per_step, row_body, 0)
    gate = mod_ref[0, 2:3, :]
    y = (o_scr[...] * _silu(g_ref[0])).astype(BF16)
    y = jnp.dot(y, wout_ref[...], preferred_element_type=F32)
    o_ref[0] = x_ref[0] + gate * y


def _attention_bias(rpb):
    col = jnp.arange(GRID_W)
    c_start = jnp.clip(col - WIN_COLS // 2, 0, GRID_W - WIN_COLS)
    rel = col[None, :] - col[:, None] + WIN_COLS - 1
    inside = (col[None, :] >= c_start[:, None]) & (col[None, :] < c_start[:, None] + WIN_COLS)
    t = rpb[:, :, jnp.clip(rel, 0, 2 * WIN_COLS - 2)]
    t = jnp.where(inside[None, None], t, MASK_VALUE)
    t2 = jnp.concatenate([t[:, :-1], t[:, 1:]], axis=-1)
    t2 = t2.reshape(N_HEAD_PAIRS, 2, 2 * WIN_ROWS - 2, GRID_W, LANES)
    return t2.transpose(0, 2, 1, 3, 4).reshape(N_HEAD_PAIRS, 2 * WIN_ROWS - 2, 2 * GRID_W, LANES)


def _attention_layer(x, q, k, v, kc, vc, g, mod, rpb, w_out, rows_per_step):
    b, seq_len, d = x.shape
    n_ctx = kc.shape[1]
    n_rows = seq_len // GRID_W
    tile = rows_per_step * GRID_W
    bias = _attention_bias(rpb)
    tile_spec = pl.BlockSpec((1, tile, d), lambda bi, i: (bi, i, 0))
    seq_spec = pl.BlockSpec((1, seq_len, d), lambda bi, i: (bi, 0, 0))
    ctx_spec = pl.BlockSpec((1, n_ctx, d), lambda bi, i: (bi, 0, 0))
    return pl.pallas_call(
        functools.partial(_attention_kernel, rows_per_step=rows_per_step, n_rows=n_rows),
        grid=(b, n_rows // rows_per_step),
        in_specs=[tile_spec, seq_spec, seq_spec, ctx_spec, ctx_spec, tile_spec, tile_spec,
                  pl.BlockSpec((1, 3, d), lambda bi, i: (bi, 0, 0)),
                  _resident(bias.shape), _resident(w_out.shape)],
        out_specs=tile_spec,
        out_shape=jax.ShapeDtypeStruct(x.shape, F32),
        scratch_shapes=[pltpu.VMEM((tile, d), F32)],
        compiler_params=_params(2),
        name="na_attention",
    )(q, k, v, kc, vc, g, x, mod, bias, w_out)


def kernel(x, c, ctx, c_ctx, norm_g, ada_w, ada_b, pool_w_in, pool_w_grp, pool_scale, pool_w_out,
           na_w_in, na_rpb, na_w_out, conv_w_in, conv_dw, conv_db, conv_w_out, final_g):
    depth = norm_g.shape[0]
    batch, _, d = x.shape
    assert ctx.shape[1] % GRID_W == 0 and WIN_ROWS // 2 <= HALO

    cond_rows = 16
    cond = jnp.zeros((cond_rows, d), F32).at[:batch].set(c).at[batch].set(c_ctx)
    mods = _modulation(cond, ada_w, ada_b)

    last_ctx_reader = max([i for i in range(depth) if i % N_MIXERS == 1], default=-1)
    for i in range(depth):
        kind, j = i % N_MIXERS, i // N_MIXERS
        update_ctx = i < last_ctx_reader
        mod_x = mods[i, :batch].reshape(batch, 3, d)
        mod_c = mods[i, batch:batch + 1].reshape(1, 3, d)
        fg = final_g if i == depth - 1 else None
        if kind == 0:
            weights = (pool_w_in[j].astype(BF16), pool_w_grp[j].astype(BF16),
                       pool_scale[j].reshape(1, WIDTH), pool_w_out[j].astype(BF16))
            run = functools.partial(_mixer_layer, _pool_kernel, norm_g=norm_g[i], weights=weights)
        elif kind == 2:
            weights = (conv_w_in[j].astype(BF16), conv_dw[j], conv_db[j].reshape(1, WIDTH),
                       conv_w_out[j].astype(BF16))
            run = functools.partial(_mixer_layer, _conv_kernel, norm_g=norm_g[i], weights=weights)
        if kind != 1:
            if update_ctx:
                ctx = run(ctx, mod_c, final_g=None, tile=256, name=f"ctx_layer{i}")
            x = run(x, mod_x, final_g=fg, tile=512, name=f"layer{i}")
            continue

        w_in = na_w_in[j].astype(BF16)
        q, k, v, g = _project(x, mod_x, norm_g[i], w_in, (BF16, BF16, BF16, F32),
                              (HEAD_DIM ** -0.5, 1.0, 1.0, 1.0), 512, f"na_project{i}")
        if update_ctx:
            raise NotImplementedError("context output of a neighbourhood-attention layer")
        kc, vc = _project(ctx, mod_c, norm_g[i], w_in[:, WIDTH:3 * WIDTH], (BF16, BF16),
                          (1.0, 1.0), 256, f"na_ctx_project{i}")
        x = _attention_layer(x, q, k, v, kc, vc, g, mod_x, na_rpb[j], na_w_out[j].astype(BF16), 8)
        if fg is not None:
            raise NotImplementedError("final norm after a neighbourhood-attention layer")
    return x
```

```python
import functools
import math

import numpy as np

import jax
import jax.numpy as jnp
from jax import lax
from jax.experimental import pallas as pl
from jax.experimental.pallas import tpu as pltpu

D_MODEL = 1024
WIDTH = D_MODEL
GRID_W = 64
N_MIXERS = 3
POOL_WINDOWS = (2, 4, 8, 16)
POOL_GROUP = WIDTH // len(POOL_WINDOWS)
HEAD_DIM = 64
N_HEADS = WIDTH // HEAD_DIM
WIN_ROWS = 8
WIN_COLS = 16
EPS = 1e-6

HALO = 8
LANES = 128
N_HEAD_PAIRS = WIDTH // LANES
MASK_VALUE = -1e30
LOG2E = math.log2(math.e)
VMEM_LIMIT = 56 * 1024 * 1024

X_TILE = 512
CTX_TILE = 256
ATT_ROWS = 8
SLAB_ROWS = WIN_ROWS + 2
PIPE_UNROLL = 6
COND_ROWS = 16

F32 = jnp.float32
BF16 = jnp.bfloat16


def _silu(x):
    return x / (1.0 + jnp.exp(-x))


def _mod_norm(x, norm_g, shift, scale):
    ms = jnp.mean(x * x, axis=-1, keepdims=True)
    return (x * lax.rsqrt(ms + EPS)) * (norm_g * (1.0 + scale)) + shift


def _params(n_axes):
    return pltpu.CompilerParams(
        dimension_semantics=("arbitrary",) * n_axes, vmem_limit_bytes=VMEM_LIMIT)


def _layer_block(arr, j, col=None):
    shape = (1,) + arr.shape[1:]
    index = (j,) + (0,) * (arr.ndim - 1)
    if col is not None:
        shape = shape[:-1] + (WIDTH,)
        index = index[:-1] + (col,)
    return pl.BlockSpec(shape, lambda *_: index, pipeline_mode=pl.Buffered(1))


def _mod_block(mods, layer, row):
    d = mods.shape[-1]
    if row is None:
        return pl.BlockSpec((1, 1, 3, d), lambda bi, i: (layer, bi, 0, 0))
    return pl.BlockSpec((1, 1, 3, d), lambda bi, i: (layer, row, 0, 0))


def _modulation_kernel(cond_ref, w_ref, b_ref, o_ref):
    s = _silu(cond_ref[...]).astype(BF16)
    o_ref[0] = jnp.dot(s, w_ref[0].astype(BF16), preferred_element_type=F32) + b_ref[0]


def _modulation(cond, ada_w, ada_b):
    depth, d, n = ada_w.shape
    rows = cond.shape[0]
    tn = 1024
    return pl.pallas_call(
        _modulation_kernel,
        grid=(depth, n // tn),
        in_specs=[
            pl.BlockSpec((rows, d), lambda i, j: (0, 0)),
            pl.BlockSpec((1, d, tn), lambda i, j: (i, 0, j)),
            pl.BlockSpec((1, 1, tn), lambda i, j: (i, 0, j)),
        ],
        out_specs=pl.BlockSpec((1, rows, tn), lambda i, j: (i, 0, j)),
        out_shape=jax.ShapeDtypeStruct((depth, rows, n), F32),
        compiler_params=_params(2),
        name="modulation",
    )(cond, ada_w, ada_b.reshape(depth, 1, n))


def _extended_rows(x_ref, xp_ref, xn_ref, tile, seq_len):
    xe = jnp.concatenate([xp_ref[0], x_ref[0], xn_ref[0]], axis=0)
    row = lax.broadcasted_iota(jnp.int32, (tile + 2 * HALO, 1), 0)
    t = row + (pl.program_id(1) * tile - HALO)
    return xe, (t >= 0) & (t < seq_len)


def _window_sum(e, width, tile):
    n = e.shape[0]
    a = e
    k = 1
    while k < width:
        a = a + pltpu.roll(a, k, axis=0)
        k *= 2
    ahead = width // 2 - 1
    if ahead:
        a = pltpu.roll(a, n - ahead, axis=0)
    return a[HALO:HALO + tile]


def _finish(x, y, gate, o_ref, fg_ref):
    out = x + gate * y
    if fg_ref is not None:
        ms = jnp.mean(out * out, axis=-1, keepdims=True)
        out = (out * lax.rsqrt(ms + EPS)) * fg_ref[...]
    o_ref[0] = out


def _pool_kernel(x_ref, xp_ref, xn_ref, mod_ref, ng_ref, win_ref, wgrp_ref, ps_ref, wout_ref,
                 *rest, tile, seq_len):
    fg_ref, o_ref = rest if len(rest) == 2 else (None, rest[0])
    shift, scale, gate = mod_ref[0, 0, 0:1, :], mod_ref[0, 0, 1:2, :], mod_ref[0, 0, 2:3, :]
    xe, valid = _extended_rows(x_ref, xp_ref, xn_ref, tile, seq_len)
    he = _mod_norm(xe, ng_ref[0], shift, scale).astype(BF16)
    ug = jnp.dot(he, win_ref[0], preferred_element_type=F32)
    u = jnp.where(valid, ug[:, :WIDTH], 0.0)
    g = ug[HALO:HALO + tile, WIDTH:]

    t = lax.broadcasted_iota(jnp.int32, (tile, 1), 0) + pl.program_id(1) * tile
    mixed = []
    for gi, width in enumerate(POOL_WINDOWS):
        ue = u[:, gi * POOL_GROUP:(gi + 1) * POOL_GROUP]
        half = width // 2
        cnt = jnp.minimum(t + half, seq_len) - jnp.maximum(t - half, 0)
        pooled = _window_sum(ue, width, tile) / cnt.astype(F32)
        diff = (pooled - ue[HALO:HALO + tile]).astype(BF16)
        mixed.append(jnp.dot(diff, wgrp_ref[0, gi], preferred_element_type=F32))
    mixed = jnp.concatenate(mixed, axis=1)
    z = (mixed * ps_ref[0] * _silu(g)).astype(BF16)
    y = jnp.dot(z, wout_ref[0], preferred_element_type=F32)
    _finish(x_ref[0], y, gate, o_ref, fg_ref)


def _conv_kernel(x_ref, xp_ref, xn_ref, mod_ref, ng_ref, win_ref, dw_ref, db_ref, wout_ref,
                 *rest, tile, seq_len):
    fg_ref, o_ref = rest if len(rest) == 2 else (None, rest[0])
    shift, scale, gate = mod_ref[0, 0, 0:1, :], mod_ref[0, 0, 1:2, :], mod_ref[0, 0, 2:3, :]
    xe, valid = _extended_rows(x_ref, xp_ref, xn_ref, tile, seq_len)
    he = _mod_norm(xe, ng_ref[0], shift, scale).astype(BF16)
    pr = jnp.dot(he, win_ref[0], preferred_element_type=F32)
    n = tile + 2 * HALO
    rows = slice(HALO, HALO + tile)
    z = jnp.where(valid, pr[:, WIDTH:2 * WIDTH] * pr[:, 2 * WIDTH:3 * WIDTH], 0.0)
    conv = (dw_ref[0, 0:1, :] * pltpu.roll(z, 1, axis=0)[rows]
            + dw_ref[0, 1:2, :] * z[rows]
            + dw_ref[0, 2:3, :] * pltpu.roll(z, n - 1, axis=0)[rows]
            + db_ref[0])
    y = pr[rows, :WIDTH] * conv * _silu(pr[rows, 3 * WIDTH:])
    y = jnp.dot(y.astype(BF16), wout_ref[0], preferred_element_type=F32)
    _finish(x_ref[0], y, gate, o_ref, fg_ref)


def _mixer_layer(kernel_fn, x, mods, layer, mod_row, norm_g, weights, j, final_g, tile, name):
    b, seq_len, d = x.shape
    tile = min(tile, seq_len)
    halo_blocks = seq_len // HALO
    per_tile = tile // HALO
    in_specs = [
        pl.BlockSpec((1, tile, d), lambda bi, i: (bi, i, 0)),
        pl.BlockSpec((1, HALO, d), lambda bi, i: (bi, jnp.maximum(i * per_tile - 1, 0), 0)),
        pl.BlockSpec((1, HALO, d),
                     lambda bi, i: (bi, jnp.minimum((i + 1) * per_tile, halo_blocks - 1), 0)),
        _mod_block(mods, layer, mod_row),
        _layer_block(norm_g, layer),
    ] + [_layer_block(w, j) for w in weights]
    args = [x, x, x, mods, norm_g] + list(weights)
    if final_g is not None:
        in_specs.append(pl.BlockSpec((1, d), lambda bi, i: (0, 0)))
        args.append(final_g.reshape(1, d))
    return pl.pallas_call(
        functools.partial(kernel_fn, tile=tile, seq_len=seq_len),
        grid=(b, seq_len // tile),
        in_specs=in_specs,
        out_specs=pl.BlockSpec((1, tile, d), lambda bi, i: (bi, i, 0)),
        out_shape=jax.ShapeDtypeStruct(x.shape, F32),
        compiler_params=_params(2),
        name=name,
    )(*args)


def _project_kernel(x_ref, mod_ref, ng_ref, *refs, scales):
    n = len(scales)
    w_refs, o_refs = refs[:n], refs[n:]
    shift, scale = mod_ref[0, 0, 0:1, :], mod_ref[0, 0, 1:2, :]
    h = _mod_norm(x_ref[0], ng_ref[0], shift, scale).astype(BF16)
    for w_ref, o_ref, s in zip(w_refs, o_refs, scales):
        p = jnp.dot(h, w_ref[0], preferred_element_type=F32)
        if s != 1.0:
            p = p * s
        if len(o_ref.shape) == 3:
            o_ref[0] = p.astype(o_ref.dtype)
        else:
            for hp in range(N_HEAD_PAIRS):
                o_ref[0, hp] = p[:, hp * LANES:(hp + 1) * LANES].astype(o_ref.dtype)


def _project(x, mods, layer, mod_row, norm_g, w, j, cols, out_dtypes, scales, tile, name):
    b, seq_len, d = x.shape
    tile = min(tile, seq_len)
    flat_spec = pl.BlockSpec((1, tile, WIDTH), lambda bi, i: (bi, i, 0))
    pair_spec = pl.BlockSpec((1, N_HEAD_PAIRS, tile, LANES), lambda bi, i: (bi, 0, i, 0))
    out_specs = [pair_spec if dt == BF16 else flat_spec for dt in out_dtypes]
    out_shape = [jax.ShapeDtypeStruct((b, N_HEAD_PAIRS, seq_len, LANES) if dt == BF16
                                      else (b, seq_len, WIDTH), dt) for dt in out_dtypes]
    return pl.pallas_call(
        functools.partial(_project_kernel, scales=scales),
        grid=(b, seq_len // tile),
        in_specs=[
            pl.BlockSpec((1, tile, d), lambda bi, i: (bi, i, 0)),
            _mod_block(mods, layer, mod_row),
            _layer_block(norm_g, layer),
        ] + [_layer_block(w, j, col) for col in cols],
        out_specs=out_specs,
        out_shape=out_shape,
        compiler_params=_params(2),
        name=name,
    )(x, mods, norm_g, *([w] * len(cols)))


N_PAIR_TILES = 2 * WIN_ROWS - 2
TILE_MASK_THEN_FIRST = N_PAIR_TILES
TILE_LAST_THEN_MASK = N_PAIR_TILES + 1
TILE_MASKED = N_PAIR_TILES + 2
N_BIAS_TILES = N_PAIR_TILES + 3
SLAB_TILES = SLAB_ROWS // 2


def _slab_base(pair_row0, n_rows):
    return jnp.clip(pair_row0 - WIN_ROWS // 2, 0, n_rows - SLAB_ROWS)


def _bias_tile_ids(n_rows):
    ids = np.zeros((n_rows, SLAB_TILES), np.int32)
    interior = WIN_ROWS // 2 - 1
    for r in range(n_rows):
        base = int(np.clip(r - r % 2 - WIN_ROWS // 2, 0, n_rows - SLAB_ROWS))
        start = int(np.clip(r - WIN_ROWS // 2, 0, n_rows - WIN_ROWS))
        lead, d0 = start - base, start - r + WIN_ROWS - 1
        assert 0 <= lead <= 2 and start + WIN_ROWS <= base + SLAB_ROWS
        pairs = [d0 + 2 * j for j in range(WIN_ROWS // 2)]
        if lead == 0:
            row = pairs + [TILE_MASKED]
        elif lead == 2:
            row = [TILE_MASKED] + pairs
        else:
            assert d0 == interior
            row = [TILE_MASK_THEN_FIRST] + [d0 + 1 + 2 * j for j in range(WIN_ROWS // 2 - 1)] \
                + [TILE_LAST_THEN_MASK]
        ids[r] = row
    return ids.reshape(-1)


def _attention_bias(rpb):
    col = np.arange(GRID_W)
    c_start = np.clip(col - WIN_COLS // 2, 0, GRID_W - WIN_COLS)
    inside = (col[None, :] >= c_start[:, None]) & (col[None, :] < c_start[:, None] + WIN_COLS)
    h, n_dr, n_dc = rpb.shape
    period = 2 * GRID_W
    w = jnp.concatenate([rpb[..., WIN_COLS - 1:], jnp.zeros((h, n_dr, period - n_dc), F32),
                         rpb[..., :WIN_COLS - 1]], axis=-1)
    t = jnp.tile(w, (1, 1, GRID_W))[..., :GRID_W * (period - 1)]
    t = t.reshape(h, n_dr, GRID_W, period - 1)[..., :GRID_W]
    t = jnp.where(inside[None, None], t * LOG2E, MASK_VALUE)
    masked = jnp.full((h, 1, GRID_W, GRID_W), MASK_VALUE, F32)
    first, last = WIN_ROWS // 2 - 1, WIN_ROWS // 2 - 1 + WIN_ROWS - 1
    left = jnp.concatenate([t[:, :-1], masked, t[:, last:last + 1], masked], axis=1)
    right = jnp.concatenate([t[:, 1:], t[:, first:first + 1], masked, masked], axis=1)
    return jnp.concatenate([left, right], axis=-1)


def _attention_kernel(ids_ref, q_ref, k_ref, v_ref, kc_ref, vc_ref, g_ref, x_ref, mod_ref,
                      bias_ref, wout_ref, o_ref, s_scr, p_scr, r_scr, o_scr, *, n_rows):
    pair_tokens = 2 * GRID_W
    n_ctx = kc_ref.shape[2]
    slab_keys = SLAB_ROWS * GRID_W
    n_items = (ATT_ROWS // 2) * N_HEAD_PAIRS
    lane = lax.broadcasted_iota(jnp.int32, (pair_tokens, LANES), 1)
    first_head = lane < HEAD_DIM
    nt = (((1,), (1,)), ((), ()))

    def locate(item):
        item = jnp.asarray(item, jnp.int32)
        pb, hp = item // N_HEAD_PAIRS, item % N_HEAD_PAIRS
        r0 = pl.program_id(1) * ATT_ROWS + 2 * pb
        q_off = pl.multiple_of(pb * pair_tokens, pair_tokens)
        k_off = pl.multiple_of(_slab_base(r0, n_rows) * GRID_W, pair_tokens)
        return hp, r0, q_off, k_off

    def scores(item, slot):
        hp, _, q_off, k_off = locate(item)
        qp = q_ref[0, hp, pl.ds(q_off, pair_tokens), :]
        zero = jnp.zeros_like(qp)
        qa = jnp.where(first_head, qp, zero)
        qb = jnp.where(first_head, zero, qp)
        q4 = jnp.concatenate([qa[:GRID_W], qb[:GRID_W], qa[GRID_W:], qb[GRID_W:]], axis=0)
        s_scr[slot, :, :n_ctx] = lax.dot_general(q4, kc_ref[0, hp], nt, preferred_element_type=F32)
        s_scr[slot, :, n_ctx:] = lax.dot_general(
            q4, k_ref[0, hp, pl.ds(k_off, slab_keys), :], nt, preferred_element_type=F32)

    def softmax(item, slot):
        hp, r0, _, _ = locate(item)
        bias = jnp.concatenate(
            [jnp.concatenate([bias_ref[2 * hp + a, ids_ref[(r0 + i) * SLAB_TILES + j]]
                              for i in range(2) for a in range(2)], axis=0)
             for j in range(SLAB_TILES)], axis=1)
        s_ctx = s_scr[slot, :, :n_ctx]
        s_loc = s_scr[slot, :, n_ctx:] + bias
        m = jnp.maximum(jnp.max(s_loc, axis=1, keepdims=True),
                        jnp.max(s_ctx, axis=1, keepdims=True))
        p_ctx = jnp.exp2(s_ctx - m)
        p_loc = jnp.exp2(s_loc - m)
        denom = jnp.sum(p_loc, axis=1, keepdims=True) + jnp.sum(p_ctx, axis=1, keepdims=True)
        p_scr[slot, :, :n_ctx] = p_ctx.astype(BF16)
        p_scr[slot, :, n_ctx:] = p_loc.astype(BF16)
        r_scr[slot] = jnp.broadcast_to(1.0 / denom, r_scr.shape[1:])

    def values(item, slot):
        hp, _, q_off, k_off = locate(item)
        pv = (jnp.dot(p_scr[slot, :, :n_ctx], vc_ref[0, hp], preferred_element_type=F32)
              + jnp.dot(p_scr[slot, :, n_ctx:], v_ref[0, hp, pl.ds(k_off, slab_keys), :],
                        preferred_element_type=F32))
        pv = pv * r_scr[slot]
        out_a = jnp.concatenate([pv[:GRID_W], pv[2 * GRID_W:3 * GRID_W]], axis=0)
        out_b = jnp.concatenate([pv[GRID_W:2 * GRID_W], pv[3 * GRID_W:]], axis=0)
        o_scr[hp, pl.ds(q_off, pair_tokens), :] = jnp.where(first_head, out_a, out_b)

    def step(item, parity):
        scores(item + 2, parity)
        softmax(item + 1, 1 - parity)
        values(item, parity)

    scores(0, 0)
    scores(1, 1)
    softmax(0, 0)

    def trip(t, carry):
        for i in range(PIPE_UNROLL):
            step(PIPE_UNROLL * t + i, i % 2)
        return carry

    assert (n_items - 2) % PIPE_UNROLL == 0 and PIPE_UNROLL % 2 == 0
    lax.fori_loop(0, (n_items - 2) // PIPE_UNROLL, trip, 0)
    softmax(n_items - 1, 1)
    values(n_items - 2, 0)
    values(n_items - 1, 1)

    gate = mod_ref[0, 0, 2:3, :]
    o = jnp.concatenate([o_scr[hp] for hp in range(N_HEAD_PAIRS)], axis=1)
    y = (o * _silu(g_ref[0])).astype(BF16)
    y = jnp.dot(y, wout_ref[0], preferred_element_type=F32)
    o_ref[0] = x_ref[0] + gate * y


def _attention_layer(x, q, k, v, kc, vc, g, mods, layer, rpb, w_out, j):
    b, seq_len, d = x.shape
    n_ctx = kc.shape[2]
    n_rows = seq_len // GRID_W
    assert n_rows % ATT_ROWS == 0 and n_rows >= SLAB_ROWS and ATT_ROWS % 2 == 0
    tile = ATT_ROWS * GRID_W
    n_keys = n_ctx + SLAB_ROWS * GRID_W
    bias = _attention_bias(rpb)
    ids = jnp.asarray(_bias_tile_ids(n_rows))
    tile_spec = pl.BlockSpec((1, tile, d), lambda bi, i: (bi, i, 0))
    q_spec = pl.BlockSpec((1, N_HEAD_PAIRS, tile, LANES), lambda bi, i: (bi, 0, i, 0))
    seq_spec = pl.BlockSpec((1, N_HEAD_PAIRS, seq_len, LANES), lambda bi, i: (bi, 0, 0, 0))
    ctx_spec = pl.BlockSpec((1, N_HEAD_PAIRS, n_ctx, LANES), lambda bi, i: (bi, 0, 0, 0))
    return pl.pallas_call(
        functools.partial(_attention_kernel, n_rows=n_rows),
        grid=(b, n_rows // ATT_ROWS),
        in_specs=[pl.BlockSpec(memory_space=pltpu.SMEM),
                  q_spec, seq_spec, seq_spec, ctx_spec, ctx_spec, tile_spec, tile_spec,
                  _mod_block(mods, layer, None),
                  pl.BlockSpec(bias.shape, lambda bi, i: (0, 0, 0, 0),
                               pipeline_mode=pl.Buffered(1)),
                  _layer_block(w_out, j)],
        out_specs=tile_spec,
        out_shape=jax.ShapeDtypeStruct(x.shape, F32),
        scratch_shapes=[pltpu.VMEM((2, 4 * GRID_W, n_keys), F32),
                        pltpu.VMEM((2, 4 * GRID_W, n_keys), BF16),
                        pltpu.VMEM((2, 4 * GRID_W, LANES), F32),
                        pltpu.VMEM((N_HEAD_PAIRS, tile, LANES), F32)],
        compiler_params=_params(2),
        name="na_attention",
    )(ids, q, k, v, kc, vc, g, x, mods, bias, w_out)


def kernel(x, c, ctx, c_ctx, norm_g, ada_w, ada_b, pool_w_in, pool_w_grp, pool_scale, pool_w_out,
           na_w_in, na_rpb, na_w_out, conv_w_in, conv_dw, conv_db, conv_w_out, final_g):
    depth = norm_g.shape[0]
    batch, _, d = x.shape
    assert batch < COND_ROWS and WIN_ROWS // 2 <= HALO

    cond = jnp.zeros((COND_ROWS, d), F32).at[:batch].set(c).at[batch].set(c_ctx)
    mods = _modulation(cond, ada_w, ada_b).reshape(depth, COND_ROWS, 3, d)
    norm_g = norm_g.reshape(depth, 1, d)

    pool_weights = (pool_w_in.astype(BF16), pool_w_grp.astype(BF16),
                    pool_scale.reshape(-1, 1, WIDTH), pool_w_out.astype(BF16))
    conv_weights = (conv_w_in.astype(BF16), conv_dw, conv_db.reshape(-1, 1, WIDTH),
                    conv_w_out.astype(BF16))
    na_w_in, na_w_out = na_w_in.astype(BF16), na_w_out.astype(BF16)

    last_ctx_reader = max([i for i in range(depth) if i % N_MIXERS == 1], default=-1)
    for i in range(depth):
        kind, j = i % N_MIXERS, i // N_MIXERS
        update_ctx = i < last_ctx_reader
        fg = final_g if i == depth - 1 else None
        if kind != 1:
            kernel_fn, weights = ((_pool_kernel, pool_weights) if kind == 0
                                  else (_conv_kernel, conv_weights))
            run = functools.partial(_mixer_layer, kernel_fn, mods=mods, layer=i, norm_g=norm_g,
                                    weights=weights, j=j)
            if update_ctx:
                ctx = run(x=ctx, mod_row=batch, final_g=None, tile=CTX_TILE, name=f"ctx_layer{i}")
            x = run(x=x, mod_row=None, final_g=fg, tile=X_TILE, name=f"layer{i}")
            continue

        if update_ctx:
            raise NotImplementedError("context output of a neighbourhood-attention layer")
        if fg is not None:
            raise NotImplementedError("final norm after a neighbourhood-attention layer")
        q, k, v, g = _project(x, mods, i, None, norm_g, na_w_in, j, (0, 1, 2, 3),
                              (BF16, BF16, BF16, F32),
                              (HEAD_DIM ** -0.5 * LOG2E, 1.0, 1.0, 1.0), X_TILE, f"na_project{i}")
        kc, vc = _project(ctx, mods, i, batch, norm_g, na_w_in, j, (1, 2), (BF16, BF16),
                          (1.0, 1.0), CTX_TILE, f"na_ctx_project{i}")
        x = _attention_layer(x, q, k, v, kc, vc, g, mods, i, na_rpb[j], na_w_out, j)
    return x
```

```python
import functools
import math

import numpy as np

import jax
import jax.numpy as jnp
from jax import lax
from jax.experimental import pallas as pl
from jax.experimental.pallas import tpu as pltpu

D_MODEL = 1024
WIDTH = D_MODEL
GRID_W = 64
N_MIXERS = 3
POOL_WINDOWS = (2, 4, 8, 16)
POOL_GROUP = WIDTH // len(POOL_WINDOWS)
HEAD_DIM = 64
N_HEADS = WIDTH // HEAD_DIM
WIN_ROWS = 8
WIN_COLS = 16
EPS = 1e-6

HALO = 8
LANES = 128
N_HEAD_PAIRS = WIDTH // LANES
MASK_VALUE = -1e30
LOG2E = math.log2(math.e)
VMEM_LIMIT = 56 * 1024 * 1024

X_TILE = 512
CTX_TILE = 256
ATT_ROWS = 8
SLAB_ROWS = WIN_ROWS + 2
PIPE_UNROLL = 6
CONV_BLOCK = 256
COND_ROWS = 16

F32 = jnp.float32
BF16 = jnp.bfloat16


def _silu(x):
    return x / (1.0 + jnp.exp(-x))


def _mod_norm(x, norm_g, shift, scale):
    ms = jnp.mean(x * x, axis=-1, keepdims=True)
    return (x * lax.rsqrt(ms + EPS)) * (norm_g * (1.0 + scale)) + shift


def _params(n_axes):
    return pltpu.CompilerParams(
        dimension_semantics=("arbitrary",) * n_axes, vmem_limit_bytes=VMEM_LIMIT)


def _layer_block(arr, j, col=None):
    shape = (1,) + arr.shape[1:]
    index = (j,) + (0,) * (arr.ndim - 1)
    if col is not None:
        shape = shape[:-1] + (WIDTH,)
        index = index[:-1] + (col,)
    return pl.BlockSpec(shape, lambda *_: index, pipeline_mode=pl.Buffered(1))


def _mod_block(mods, layer, row):
    d = mods.shape[-1]
    if row is None:
        return pl.BlockSpec((1, 1, 3, d), lambda bi, i: (layer, bi, 0, 0))
    return pl.BlockSpec((1, 1, 3, d), lambda bi, i: (layer, row, 0, 0))


def _modulation_kernel(cond_ref, w_ref, b_ref, o_ref):
    s = _silu(cond_ref[...]).astype(BF16)
    o_ref[0] = jnp.dot(s, w_ref[0].astype(BF16), preferred_element_type=F32) + b_ref[0]


def _modulation(cond, ada_w, ada_b):
    depth, d, n = ada_w.shape
    rows = cond.shape[0]
    tn = 1024
    return pl.pallas_call(
        _modulation_kernel,
        grid=(depth, n // tn),
        in_specs=[
            pl.BlockSpec((rows, d), lambda i, j: (0, 0)),
            pl.BlockSpec((1, d, tn), lambda i, j: (i, 0, j)),
            pl.BlockSpec((1, 1, tn), lambda i, j: (i, 0, j)),
        ],
        out_specs=pl.BlockSpec((1, rows, tn), lambda i, j: (i, 0, j)),
        out_shape=jax.ShapeDtypeStruct((depth, rows, n), F32),
        compiler_params=_params(2),
        name="modulation",
    )(cond, ada_w, ada_b.reshape(depth, 1, n))


def _normed_rows(x_ref, xp_ref, xn_ref, mod_ref, ng_ref):
    shift, scale = mod_ref[0, 0, 0:1, :], mod_ref[0, 0, 1:2, :]
    xe = jnp.concatenate([xp_ref[0], x_ref[0], xn_ref[0]], axis=0)
    he = _mod_norm(xe, ng_ref[0], shift, scale)
    return he.astype(BF16), he[HALO:he.shape[0] - HALO].astype(BF16)


def _zero_outside_sequence(e, tile):
    i = pl.program_id(1)
    keep_prev = (i > 0).astype(F32)
    keep_next = (i < pl.num_programs(1) - 1).astype(F32)
    return jnp.concatenate(
        [e[:HALO] * keep_prev, e[HALO:HALO + tile], e[HALO + tile:] * keep_next], axis=0)


def _window_sum(e, width, tile):
    n = e.shape[0]
    half = width // 2
    f = e
    k = 1
    while k < half:
        f = f + pltpu.roll(f, n - k, axis=0)
        k *= 2
    if half == HALO:
        return f[:tile] + f[HALO:HALO + tile]
    return (pltpu.roll(f, half, axis=0) + f)[HALO:HALO + tile]


def _window_mean(ws, width, tile, seq_len):
    half = width // 2
    row = lax.broadcasted_iota(jnp.int32, (HALO, 1), 0)

    def inv_count(first_row):
        t = row + (pl.program_id(1) * tile + first_row)
        cnt = jnp.minimum(t + half, seq_len) - jnp.maximum(t - half, 0)
        return 1.0 / cnt.astype(F32)

    return jnp.concatenate([ws[:HALO] * inv_count(0),
                            ws[HALO:tile - HALO] * (1.0 / width),
                            ws[tile - HALO:] * inv_count(tile - HALO)], axis=0)


def _finish(x, y, gate, o_ref, fg_ref):
    out = x + gate * y
    if fg_ref is not None:
        ms = jnp.mean(out * out, axis=-1, keepdims=True)
        out = (out * lax.rsqrt(ms + EPS)) * fg_ref[...]
    o_ref[0] = out


def _pool_kernel(x_ref, xp_ref, xn_ref, mod_ref, ng_ref, win_ref, wgrp_ref, ps_ref, wout_ref,
                 *rest, tile, seq_len):
    fg_ref, o_ref = rest if len(rest) == 2 else (None, rest[0])
    he, hm = _normed_rows(x_ref, xp_ref, xn_ref, mod_ref, ng_ref)
    u = jnp.dot(he, win_ref[0, :, :WIDTH], preferred_element_type=F32)
    u = _zero_outside_sequence(u, tile)
    g = jnp.dot(hm, win_ref[0, :, WIDTH:], preferred_element_type=F32)
    mixed = []
    for gi, width in enumerate(POOL_WINDOWS):
        ue = u[:, gi * POOL_GROUP:(gi + 1) * POOL_GROUP]
        pooled = _window_mean(_window_sum(ue, width, tile), width, tile, seq_len)
        diff = (pooled - ue[HALO:HALO + tile]).astype(BF16)
        mixed.append(jnp.dot(diff, wgrp_ref[0, gi], preferred_element_type=F32))
    mixed = jnp.concatenate(mixed, axis=1)
    z = (mixed * ps_ref[0] * _silu(g)).astype(BF16)
    y = jnp.dot(z, wout_ref[0], preferred_element_type=F32)
    _finish(x_ref[0], y, mod_ref[0, 0, 2:3, :], o_ref, fg_ref)


def _conv_kernel(x_ref, xp_ref, xn_ref, mod_ref, ng_ref, win_ref, dw_ref, db_ref, wout_ref,
                 *rest, tile, seq_len):
    fg_ref, o_ref = rest if len(rest) == 2 else (None, rest[0])
    he, hm = _normed_rows(x_ref, xp_ref, xn_ref, mod_ref, ng_ref)
    n = tile + 2 * HALO
    rows = slice(HALO, HALO + tile)
    y = None
    for ci in range(WIDTH // CONV_BLOCK):
        cols = slice(ci * CONV_BLOCK, (ci + 1) * CONV_BLOCK)

        def proj(h, part):
            w = win_ref[0, :, part * WIDTH + ci * CONV_BLOCK:part * WIDTH + (ci + 1) * CONV_BLOCK]
            return jnp.dot(h, w, preferred_element_type=F32)

        z = _zero_outside_sequence(proj(he, 1) * proj(he, 2), tile)
        conv = (dw_ref[0, 0:1, cols] * pltpu.roll(z, 1, axis=0)[rows]
                + dw_ref[0, 1:2, cols] * z[rows]
                + dw_ref[0, 2:3, cols] * pltpu.roll(z, n - 1, axis=0)[rows]
                + db_ref[0, :, cols])
        yc = (proj(hm, 0) * conv * _silu(proj(hm, 3))).astype(BF16)
        part = jnp.dot(yc, wout_ref[0, cols, :], preferred_element_type=F32)
        y = part if y is None else y + part
    _finish(x_ref[0], y, mod_ref[0, 0, 2:3, :], o_ref, fg_ref)


def _mixer_layer(kernel_fn, x, mods, layer, mod_row, norm_g, weights, j, final_g, tile, name):
    b, seq_len, d = x.shape
    tile = min(tile, seq_len)
    halo_blocks = seq_len // HALO
    per_tile = tile // HALO
    in_specs = [
        pl.BlockSpec((1, tile, d), lambda bi, i: (bi, i, 0)),
        pl.BlockSpec((1, HALO, d), lambda bi, i: (bi, jnp.maximum(i * per_tile - 1, 0), 0)),
        pl.BlockSpec((1, HALO, d),
                     lambda bi, i: (bi, jnp.minimum((i + 1) * per_tile, halo_blocks - 1), 0)),
        _mod_block(mods, layer, mod_row),
        _layer_block(norm_g, layer),
    ] + [_layer_block(w, j) for w in weights]
    args = [x, x, x, mods, norm_g] + list(weights)
    if final_g is not None:
        in_specs.append(pl.BlockSpec((1, d), lambda bi, i: (0, 0)))
        args.append(final_g.reshape(1, d))
    return pl.pallas_call(
        functools.partial(kernel_fn, tile=tile, seq_len=seq_len),
        grid=(b, seq_len // tile),
        in_specs=in_specs,
        out_specs=pl.BlockSpec((1, tile, d), lambda bi, i: (bi, i, 0)),
        out_shape=jax.ShapeDtypeStruct(x.shape, F32),
        compiler_params=_params(2),
        name=name,
    )(*args)


def _project_kernel(x_ref, mod_ref, ng_ref, *refs, scales):
    n = len(scales)
    w_refs, o_refs = refs[:n], refs[n:]
    shift, scale = mod_ref[0, 0, 0:1, :], mod_ref[0, 0, 1:2, :]
    h = _mod_norm(x_ref[0], ng_ref[0], shift, scale).astype(BF16)
    for w_ref, o_ref, s in zip(w_refs, o_refs, scales):
        p = jnp.dot(h, w_ref[0], preferred_element_type=F32)
        if s != 1.0:
            p = p * s
        if len(o_ref.shape) == 3:
            o_ref[0] = p.astype(o_ref.dtype)
        else:
            for hp in range(N_HEAD_PAIRS):
                o_ref[0, hp] = p[:, hp * LANES:(hp + 1) * LANES].astype(o_ref.dtype)


def _project(x, mods, layer, mod_row, norm_g, w, j, cols, out_dtypes, scales, tile, name):
    b, seq_len, d = x.shape
    tile = min(tile, seq_len)
    flat_spec = pl.BlockSpec((1, tile, WIDTH), lambda bi, i: (bi, i, 0))
    pair_spec = pl.BlockSpec((1, N_HEAD_PAIRS, tile, LANES), lambda bi, i: (bi, 0, i, 0))
    out_specs = [pair_spec if dt == BF16 else flat_spec for dt in out_dtypes]
    out_shape = [jax.ShapeDtypeStruct((b, N_HEAD_PAIRS, seq_len, LANES) if dt == BF16
                                      else (b, seq_len, WIDTH), dt) for dt in out_dtypes]
    return pl.pallas_call(
        functools.partial(_project_kernel, scales=scales),
        grid=(b, seq_len // tile),
        in_specs=[
            pl.BlockSpec((1, tile, d), lambda bi, i: (bi, i, 0)),
            _mod_block(mods, layer, mod_row),
            _layer_block(norm_g, layer),
        ] + [_layer_block(w, j, col) for col in cols],
        out_specs=out_specs,
        out_shape=out_shape,
        compiler_params=_params(2),
        name=name,
    )(x, mods, norm_g, *([w] * len(cols)))


N_PAIR_TILES = 2 * WIN_ROWS - 2
TILE_MASK_THEN_FIRST = N_PAIR_TILES
TILE_LAST_THEN_MASK = N_PAIR_TILES + 1
TILE_MASKED = N_PAIR_TILES + 2
N_BIAS_TILES = N_PAIR_TILES + 3
SLAB_TILES = SLAB_ROWS // 2


def _slab_base(pair_row0, n_rows):
    return jnp.clip(pair_row0 - WIN_ROWS // 2, 0, n_rows - SLAB_ROWS)


def _bias_tile_ids(n_rows):
    ids = np.zeros((n_rows, SLAB_TILES), np.int32)
    interior = WIN_ROWS // 2 - 1
    for r in range(n_rows):
        base = int(np.clip(r - r % 2 - WIN_ROWS // 2, 0, n_rows - SLAB_ROWS))
        start = int(np.clip(r - WIN_ROWS // 2, 0, n_rows - WIN_ROWS))
        lead, d0 = start - base, start - r + WIN_ROWS - 1
        assert 0 <= lead <= 2 and start + WIN_ROWS <= base + SLAB_ROWS
        pairs = [d0 + 2 * j for j in range(WIN_ROWS // 2)]
        if lead == 0:
            row = pairs + [TILE_MASKED]
        elif lead == 2:
            row = [TILE_MASKED] + pairs
        else:
            assert d0 == interior
            row = [TILE_MASK_THEN_FIRST] + [d0 + 1 + 2 * j for j in range(WIN_ROWS // 2 - 1)] \
                + [TILE_LAST_THEN_MASK]
        ids[r] = row
    return ids.reshape(-1)


def _bias_tile_offsets():
    first, last = WIN_ROWS // 2 - 1, WIN_ROWS // 2 - 1 + WIN_ROWS - 1
    return ([(d, d + 1) for d in range(N_PAIR_TILES)]
            + [(None, first), (last, None), (None, None)])


def _bias_tiles_kernel(w_ref, o_ref):
    q_col = lax.broadcasted_iota(jnp.int32, (GRID_W, LANES), 0)
    lane = lax.broadcasted_iota(jnp.int32, (GRID_W, LANES), 1)
    k_col = lane % GRID_W
    c_start = jnp.clip(q_col - WIN_COLS // 2, 0, GRID_W - WIN_COLS)
    inside = (k_col >= c_start) & (k_col < c_start + WIN_COLS)
    left_half = lane < GRID_W

    def toeplitz(d, shift):
        row = jnp.broadcast_to(w_ref[0, d:d + 1, :], (GRID_W, LANES))
        return pltpu.roll(row, shift, axis=1, stride=1, stride_axis=0)

    masked = jnp.full((GRID_W, LANES), MASK_VALUE, F32)
    for tile_id, (left, right) in enumerate(_bias_tile_offsets()):
        lv = masked if left is None else toeplitz(left, 0) * LOG2E
        rv = masked if right is None else toeplitz(right, GRID_W) * LOG2E
        o_ref[0, tile_id] = jnp.where(inside, jnp.where(left_half, lv, rv), masked)


def _attention_bias(rpb):
    h, n_dr, n_dc = rpb.shape
    assert n_dr == 2 * WIN_ROWS - 1 and n_dc == 2 * WIN_COLS - 1
    w = jnp.concatenate([rpb[..., WIN_COLS - 1:], jnp.zeros((h, n_dr, LANES - n_dc), F32),
                         rpb[..., :WIN_COLS - 1]], axis=-1)
    w = jnp.pad(w, ((0, 0), (0, 2 * WIN_ROWS - n_dr), (0, 0)))
    return pl.pallas_call(
        _bias_tiles_kernel,
        grid=(h,),
        in_specs=[pl.BlockSpec((1, 2 * WIN_ROWS, LANES), lambda i: (i, 0, 0))],
        out_specs=pl.BlockSpec((1, N_BIAS_TILES, GRID_W, LANES), lambda i: (i, 0, 0, 0)),
        out_shape=jax.ShapeDtypeStruct((h, N_BIAS_TILES, GRID_W, LANES), F32),
        compiler_params=_params(1),
        name="bias_tiles",
    )(w)


def _attention_kernel(ids_ref, q_ref, k_ref, v_ref, kc_ref, vc_ref, g_ref, x_ref, mod_ref,
                      bias_ref, wout_ref, o_ref, s_scr, p_scr, r_scr, o_scr, *, n_rows):
    pair_tokens = 2 * GRID_W
    n_ctx = kc_ref.shape[2]
    slab_keys = SLAB_ROWS * GRID_W
    n_items = (ATT_ROWS // 2) * N_HEAD_PAIRS
    lane = lax.broadcasted_iota(jnp.int32, (pair_tokens, LANES), 1)
    first_head = lane < HEAD_DIM
    nt = (((1,), (1,)), ((), ()))

    def locate(item):
        item = jnp.asarray(item, jnp.int32)
        pb, hp = item // N_HEAD_PAIRS, item % N_HEAD_PAIRS
        r0 = pl.program_id(1) * ATT_ROWS + 2 * pb
        q_off = pl.multiple_of(pb * pair_tokens, pair_tokens)
        k_off = pl.multiple_of(_slab_base(r0, n_rows) * GRID_W, pair_tokens)
        return hp, r0, q_off, k_off

    def scores(item, slot):
        hp, _, q_off, k_off = locate(item)
        qp = q_ref[0, hp, pl.ds(q_off, pair_tokens), :]
        zero = jnp.zeros_like(qp)
        qa = jnp.where(first_head, qp, zero)
        qb = jnp.where(first_head, zero, qp)
        q4 = jnp.concatenate([qa[:GRID_W], qb[:GRID_W], qa[GRID_W:], qb[GRID_W:]], axis=0)
        s_scr[slot, :, :n_ctx] = lax.dot_general(q4, kc_ref[0, hp], nt, preferred_element_type=F32)
        s_scr[slot, :, n_ctx:] = lax.dot_general(
            q4, k_ref[0, hp, pl.ds(k_off, slab_keys), :], nt, preferred_element_type=F32)

    def softmax(item, slot):
        hp, r0, _, _ = locate(item)
        bias = jnp.concatenate(
            [jnp.concatenate([bias_ref[2 * hp + a, ids_ref[(r0 + i) * SLAB_TILES + j]]
                              for i in range(2) for a in range(2)], axis=0)
             for j in range(SLAB_TILES)], axis=1)
        s_ctx = s_scr[slot, :, :n_ctx]
        s_loc = s_scr[slot, :, n_ctx:] + bias
        m = jnp.maximum(jnp.max(s_loc, axis=1, keepdims=True),
                        jnp.max(s_ctx, axis=1, keepdims=True))
        p_ctx = jnp.exp2(s_ctx - m)
        p_loc = jnp.exp2(s_loc - m)
        denom = jnp.sum(p_loc, axis=1, keepdims=True) + jnp.sum(p_ctx, axis=1, keepdims=True)
        p_scr[slot, :, :n_ctx] = p_ctx.astype(BF16)
        p_scr[slot, :, n_ctx:] = p_loc.astype(BF16)
        r_scr[slot] = jnp.broadcast_to(1.0 / denom, r_scr.shape[1:])

    def values(item, slot):
        hp, _, q_off, k_off = locate(item)
        pv = (jnp.dot(p_scr[slot, :, :n_ctx], vc_ref[0, hp], preferred_element_type=F32)
              + jnp.dot(p_scr[slot, :, n_ctx:], v_ref[0, hp, pl.ds(k_off, slab_keys), :],
                        preferred_element_type=F32))
        pv = pv * r_scr[slot]
        out_a = jnp.concatenate([pv[:GRID_W], pv[2 * GRID_W:3 * GRID_W]], axis=0)
        out_b = jnp.concatenate([pv[GRID_W:2 * GRID_W], pv[3 * GRID_W:]], axis=0)
        o_scr[hp, pl.ds(q_off, pair_tokens), :] = jnp.where(first_head, out_a, out_b)

    def step(item, parity):
        scores(item + 2, parity)
        softmax(item + 1, 1 - parity)
        values(item, parity)

    scores(0, 0)
    scores(1, 1)
    softmax(0, 0)

    def trip(t, carry):
        for i in range(PIPE_UNROLL):
            step(PIPE_UNROLL * t + i, i % 2)
        return carry

    assert (n_items - 2) % PIPE_UNROLL == 0 and PIPE_UNROLL % 2 == 0
    lax.fori_loop(0, (n_items - 2) // PIPE_UNROLL, trip, 0)
    softmax(n_items - 1, 1)
    values(n_items - 2, 0)
    values(n_items - 1, 1)

    gate = mod_ref[0, 0, 2:3, :]
    o = jnp.concatenate([o_scr[hp] for hp in range(N_HEAD_PAIRS)], axis=1)
    y = (o * _silu(g_ref[0])).astype(BF16)
    y = jnp.dot(y, wout_ref[0], preferred_element_type=F32)
    o_ref[0] = x_ref[0] + gate * y


def _attention_layer(x, q, k, v, kc, vc, g, mods, layer, rpb, w_out, j):
    b, seq_len, d = x.shape
    n_ctx = kc.shape[2]
    n_rows = seq_len // GRID_W
    assert n_rows % ATT_ROWS == 0 and n_rows >= SLAB_ROWS and ATT_ROWS % 2 == 0
    tile = ATT_ROWS * GRID_W
    n_keys = n_ctx + SLAB_ROWS * GRID_W
    bias = _attention_bias(rpb)
    ids = jnp.asarray(_bias_tile_ids(n_rows))
    tile_spec = pl.BlockSpec((1, tile, d), lambda bi, i: (bi, i, 0))
    q_spec = pl.BlockSpec((1, N_HEAD_PAIRS, tile, LANES), lambda bi, i: (bi, 0, i, 0))
    seq_spec = pl.BlockSpec((1, N_HEAD_PAIRS, seq_len, LANES), lambda bi, i: (bi, 0, 0, 0))
    ctx_spec = pl.BlockSpec((1, N_HEAD_PAIRS, n_ctx, LANES), lambda bi, i: (bi, 0, 0, 0))
    return pl.pallas_call(
        functools.partial(_attention_kernel, n_rows=n_rows),
        grid=(b, n_rows // ATT_ROWS),
        in_specs=[pl.BlockSpec(memory_space=pltpu.SMEM),
                  q_spec, seq_spec, seq_spec, ctx_spec, ctx_spec, tile_spec, tile_spec,
                  _mod_block(mods, layer, None),
                  pl.BlockSpec(bias.shape, lambda bi, i: (0, 0, 0, 0),
                               pipeline_mode=pl.Buffered(1)),
                  _layer_block(w_out, j)],
        out_specs=tile_spec,
        out_shape=jax.ShapeDtypeStruct(x.shape, F32),
        scratch_shapes=[pltpu.VMEM((2, 4 * GRID_W, n_keys), F32),
                        pltpu.VMEM((2, 4 * GRID_W, n_keys), BF16),
                        pltpu.VMEM((2, 4 * GRID_W, LANES), F32),
                        pltpu.VMEM((N_HEAD_PAIRS, tile, LANES), F32)],
        compiler_params=_params(2),
        name="na_attention",
    )(ids, q, k, v, kc, vc, g, x, mods, bias, w_out)


def kernel(x, c, ctx, c_ctx, norm_g, ada_w, ada_b, pool_w_in, pool_w_grp, pool_scale, pool_w_out,
           na_w_in, na_rpb, na_w_out, conv_w_in, conv_dw, conv_db, conv_w_out, final_g):
    depth = norm_g.shape[0]
    batch, _, d = x.shape
    assert batch < COND_ROWS and WIN_ROWS // 2 <= HALO

    cond = jnp.zeros((COND_ROWS, d), F32).at[:batch].set(c).at[batch].set(c_ctx)
    mods = _modulation(cond, ada_w, ada_b).reshape(depth, COND_ROWS, 3, d)
    norm_g = norm_g.reshape(depth, 1, d)

    pool_weights = (pool_w_in.astype(BF16), pool_w_grp.astype(BF16),
                    pool_scale.reshape(-1, 1, WIDTH), pool_w_out.astype(BF16))
    conv_weights = (conv_w_in.astype(BF16), conv_dw, conv_db.reshape(-1, 1, WIDTH),
                    conv_w_out.astype(BF16))
    na_w_in, na_w_out = na_w_in.astype(BF16), na_w_out.astype(BF16)

    last_ctx_reader = max([i for i in range(depth) if i % N_MIXERS == 1], default=-1)
    for i in range(depth):
        kind, j = i % N_MIXERS, i // N_MIXERS
        update_ctx = i < last_ctx_reader
        fg = final_g if i == depth - 1 else None
        if kind != 1:
            kernel_fn, weights = ((_pool_kernel, pool_weights) if kind == 0
                                  else (_conv_kernel, conv_weights))
            run = functools.partial(_mixer_layer, kernel_fn, mods=mods, layer=i, norm_g=norm_g,
                                    weights=weights, j=j)
            if update_ctx:
                ctx = run(x=ctx, mod_row=batch, final_g=None, tile=CTX_TILE, name=f"ctx_layer{i}")
            x = run(x=x, mod_row=None, final_g=fg, tile=X_TILE, name=f"layer{i}")
            continue

        if update_ctx:
            raise NotImplementedError("context output of a neighbourhood-attention layer")
        if fg is not None:
            raise NotImplementedError("final norm after a neighbourhood-attention layer")
        q, k, v, g = _project(x, mods, i, None, norm_g, na_w_in, j, (0, 1, 2, 3),
                              (BF16, BF16, BF16, F32),
                              (HEAD_DIM ** -0.5 * LOG2E, 1.0, 1.0, 1.0), X_TILE, f"na_project{i}")
        kc, vc = _project(ctx, mods, i, batch, norm_g, na_w_in, j, (1, 2), (BF16, BF16),
                          (1.0, 1.0), CTX_TILE, f"na_ctx_project{i}")
        x = _attention_layer(x, q, k, v, kc, vc, g, mods, i, na_rpb[j], na_w_out, j)
    return x
```

```python
import functools
import math

import numpy as np

import jax
import jax.numpy as jnp
from jax import lax
from jax.experimental import pallas as pl
from jax.experimental.pallas import tpu as pltpu

D_MODEL = 1024
WIDTH = D_MODEL
GRID_W = 64
N_MIXERS = 3
POOL_WINDOWS = (2, 4, 8, 16)
POOL_GROUP = WIDTH // len(POOL_WINDOWS)
HEAD_DIM = 64
N_HEADS = WIDTH // HEAD_DIM
WIN_ROWS = 8
WIN_COLS = 16
EPS = 1e-6

HALO = 8
LANES = 128
N_HEAD_PAIRS = WIDTH // LANES
MASK_VALUE = -1e30
LOG2E = math.log2(math.e)
VMEM_LIMIT = 56 * 1024 * 1024

X_TILE = 1024
CTX_TILE = 256
ATT_ROWS = 8
SLAB_ROWS = WIN_ROWS + 2
PIPE_SLOTS = 2
PIPE_UNROLL = 6
BIAS_HEADS_PER_STEP = 8
CONV_BLOCK = 256
COND_ROWS = 16

F32 = jnp.float32
BF16 = jnp.bfloat16


def _silu(x):
    return x / (1.0 + jnp.exp(-x))


def _mod_norm(x, norm_g, shift, scale):
    ms = jnp.mean(x * x, axis=-1, keepdims=True)
    return (x * lax.rsqrt(ms + EPS)) * (norm_g * (1.0 + scale)) + shift


def _params(n_axes, flags=None):
    return pltpu.CompilerParams(
        dimension_semantics=("arbitrary",) * n_axes, vmem_limit_bytes=VMEM_LIMIT, flags=flags)


def _layer_block(arr, j, col=None):
    shape = (1,) + arr.shape[1:]
    index = (j,) + (0,) * (arr.ndim - 1)
    if col is not None:
        shape = shape[:-1] + (WIDTH,)
        index = index[:-1] + (col,)
    return pl.BlockSpec(shape, lambda *_: index, pipeline_mode=pl.Buffered(1))


def _mod_block(mods, layer, row):
    d = mods.shape[-1]
    if row is None:
        return pl.BlockSpec((1, 1, 3, d), lambda bi, i: (layer, bi, 0, 0))
    return pl.BlockSpec((1, 1, 3, d), lambda bi, i: (layer, row, 0, 0))


def _modulation_kernel(cond_ref, w_ref, b_ref, o_ref):
    s = _silu(cond_ref[...]).astype(BF16)
    o_ref[0] = jnp.dot(s, w_ref[0].astype(BF16), preferred_element_type=F32) + b_ref[0]


def _modulation(cond, ada_w, ada_b):
    depth, d, n = ada_w.shape
    rows = cond.shape[0]
    tn = 1024
    return pl.pallas_call(
        _modulation_kernel,
        grid=(depth, n // tn),
        in_specs=[
            pl.BlockSpec((rows, d), lambda i, j: (0, 0)),
            pl.BlockSpec((1, d, tn), lambda i, j: (i, 0, j)),
            pl.BlockSpec((1, 1, tn), lambda i, j: (i, 0, j)),
        ],
        out_specs=pl.BlockSpec((1, rows, tn), lambda i, j: (i, 0, j)),
        out_shape=jax.ShapeDtypeStruct((depth, rows, n), F32),
        compiler_params=_params(2),
        name="modulation",
    )(cond, ada_w, ada_b.reshape(depth, 1, n))


def _normed_rows(x_ref, xp_ref, xn_ref, mod_ref, ng_ref):
    shift, scale = mod_ref[0, 0, 0:1, :], mod_ref[0, 0, 1:2, :]
    xe = jnp.concatenate([xp_ref[0], x_ref[0], xn_ref[0]], axis=0)
    he = _mod_norm(xe, ng_ref[0], shift, scale)
    return he.astype(BF16), he[HALO:he.shape[0] - HALO].astype(BF16)


def _zero_outside_sequence(e, tile):
    i = pl.program_id(1)
    keep_prev = (i > 0).astype(F32)
    keep_next = (i < pl.num_programs(1) - 1).astype(F32)
    return jnp.concatenate(
        [e[:HALO] * keep_prev, e[HALO:HALO + tile], e[HALO + tile:] * keep_next], axis=0)


def _window_sum(e, width, tile):
    n = e.shape[0]
    half = width // 2
    f = e
    k = 1
    while k < half:
        f = f + pltpu.roll(f, n - k, axis=0)
        k *= 2
    if half == HALO:
        return f[:tile] + f[HALO:HALO + tile]
    return (pltpu.roll(f, half, axis=0) + f)[HALO:HALO + tile]


def _window_mean(ws, width, tile, seq_len):
    half = width // 2
    row = lax.broadcasted_iota(jnp.int32, (HALO, 1), 0)

    def inv_count(first_row):
        t = row + (pl.program_id(1) * tile + first_row)
        cnt = jnp.minimum(t + half, seq_len) - jnp.maximum(t - half, 0)
        return 1.0 / cnt.astype(F32)

    return jnp.concatenate([ws[:HALO] * inv_count(0),
                            ws[HALO:tile - HALO] * (1.0 / width),
                            ws[tile - HALO:] * inv_count(tile - HALO)], axis=0)


def _finish(x, y, gate, o_ref, fg_ref):
    out = x + gate * y
    if fg_ref is not None:
        ms = jnp.mean(out * out, axis=-1, keepdims=True)
        out = (out * lax.rsqrt(ms + EPS)) * fg_ref[...]
    o_ref[0] = out


def _pool_kernel(x_ref, xp_ref, xn_ref, mod_ref, ng_ref, win_ref, wgrp_ref, ps_ref, wout_ref,
                 *rest, tile, seq_len):
    fg_ref, o_ref = rest if len(rest) == 2 else (None, rest[0])
    he, hm = _normed_rows(x_ref, xp_ref, xn_ref, mod_ref, ng_ref)
    u = jnp.dot(he, win_ref[0, :, :WIDTH], preferred_element_type=F32)
    u = _zero_outside_sequence(u, tile)
    g = jnp.dot(hm, win_ref[0, :, WIDTH:], preferred_element_type=F32)
    mixed = []
    for gi, width in enumerate(POOL_WINDOWS):
        ue = u[:, gi * POOL_GROUP:(gi + 1) * POOL_GROUP]
        pooled = _window_mean(_window_sum(ue, width, tile), width, tile, seq_len)
        diff = (pooled - ue[HALO:HALO + tile]).astype(BF16)
        mixed.append(jnp.dot(diff, wgrp_ref[0, gi], preferred_element_type=F32))
    mixed = jnp.concatenate(mixed, axis=1)
    z = (mixed * ps_ref[0] * _silu(g)).astype(BF16)
    y = jnp.dot(z, wout_ref[0], preferred_element_type=F32)
    _finish(x_ref[0], y, mod_ref[0, 0, 2:3, :], o_ref, fg_ref)


def _conv_kernel(x_ref, xp_ref, xn_ref, mod_ref, ng_ref, win_ref, dw_ref, db_ref, wout_ref,
                 *rest, tile, seq_len):
    fg_ref, o_ref = rest if len(rest) == 2 else (None, rest[0])
    he, hm = _normed_rows(x_ref, xp_ref, xn_ref, mod_ref, ng_ref)
    n = tile + 2 * HALO
    rows = slice(HALO, HALO + tile)
    y = None
    for ci in range(WIDTH // CONV_BLOCK):
        cols = slice(ci * CONV_BLOCK, (ci + 1) * CONV_BLOCK)

        def proj(h, part):
            w = win_ref[0, :, part * WIDTH + ci * CONV_BLOCK:part * WIDTH + (ci + 1) * CONV_BLOCK]
            return jnp.dot(h, w, preferred_element_type=F32)

        z = _zero_outside_sequence(proj(he, 1) * proj(he, 2), tile)
        conv = (dw_ref[0, 0:1, cols] * pltpu.roll(z, 1, axis=0)[rows]
                + dw_ref[0, 1:2, cols] * z[rows]
                + dw_ref[0, 2:3, cols] * pltpu.roll(z, n - 1, axis=0)[rows]
                + db_ref[0, :, cols])
        yc = (proj(hm, 0) * conv * _silu(proj(hm, 3))).astype(BF16)
        part = jnp.dot(yc, wout_ref[0, cols, :], preferred_element_type=F32)
        y = part if y is None else y + part
    _finish(x_ref[0], y, mod_ref[0, 0, 2:3, :], o_ref, fg_ref)


def _mixer_layer(kernel_fn, x, mods, layer, mod_row, norm_g, weights, j, final_g, tile, name):
    b, seq_len, d = x.shape
    tile = min(tile, seq_len)
    halo_blocks = seq_len // HALO
    per_tile = tile // HALO
    in_specs = [
        pl.BlockSpec((1, tile, d), lambda bi, i: (bi, i, 0)),
        pl.BlockSpec((1, HALO, d), lambda bi, i: (bi, jnp.maximum(i * per_tile - 1, 0), 0)),
        pl.BlockSpec((1, HALO, d),
                     lambda bi, i: (bi, jnp.minimum((i + 1) * per_tile, halo_blocks - 1), 0)),
        _mod_block(mods, layer, mod_row),
        _layer_block(norm_g, layer),
    ] + [_layer_block(w, j) for w in weights]
    args = [x, x, x, mods, norm_g] + list(weights)
    if final_g is not None:
        in_specs.append(pl.BlockSpec((1, d), lambda bi, i: (0, 0)))
        args.append(final_g.reshape(1, d))
    return pl.pallas_call(
        functools.partial(kernel_fn, tile=tile, seq_len=seq_len),
        grid=(b, seq_len // tile),
        in_specs=in_specs,
        out_specs=pl.BlockSpec((1, tile, d), lambda bi, i: (bi, i, 0)),
        out_shape=jax.ShapeDtypeStruct(x.shape, F32),
        compiler_params=_params(2),
        name=name,
    )(*args)


def _project_kernel(x_ref, mod_ref, ng_ref, *refs, scales):
    n = len(scales)
    w_refs, o_refs = refs[:n], refs[n:]
    shift, scale = mod_ref[0, 0, 0:1, :], mod_ref[0, 0, 1:2, :]
    h = _mod_norm(x_ref[0], ng_ref[0], shift, scale).astype(BF16)
    for w_ref, o_ref, s in zip(w_refs, o_refs, scales):
        p = jnp.dot(h, w_ref[0], preferred_element_type=F32)
        if s != 1.0:
            p = p * s
        if len(o_ref.shape) == 3:
            o_ref[0] = p.astype(o_ref.dtype)
        else:
            for hp in range(N_HEAD_PAIRS):
                o_ref[0, hp] = p[:, hp * LANES:(hp + 1) * LANES].astype(o_ref.dtype)


def _project(x, mods, layer, mod_row, norm_g, w, j, cols, out_dtypes, scales, tile, name):
    b, seq_len, d = x.shape
    tile = min(tile, seq_len)
    flat_spec = pl.BlockSpec((1, tile, WIDTH), lambda bi, i: (bi, i, 0))
    pair_spec = pl.BlockSpec((1, N_HEAD_PAIRS, tile, LANES), lambda bi, i: (bi, 0, i, 0))
    out_specs = [pair_spec if dt == BF16 else flat_spec for dt in out_dtypes]
    out_shape = [jax.ShapeDtypeStruct((b, N_HEAD_PAIRS, seq_len, LANES) if dt == BF16
                                      else (b, seq_len, WIDTH), dt) for dt in out_dtypes]
    return pl.pallas_call(
        functools.partial(_project_kernel, scales=scales),
        grid=(b, seq_len // tile),
        in_specs=[
            pl.BlockSpec((1, tile, d), lambda bi, i: (bi, i, 0)),
            _mod_block(mods, layer, mod_row),
            _layer_block(norm_g, layer),
        ] + [_layer_block(w, j, col) for col in cols],
        out_specs=out_specs,
        out_shape=out_shape,
        compiler_params=_params(2),
        name=name,
    )(x, mods, norm_g, *([w] * len(cols)))


N_PAIR_TILES = 2 * WIN_ROWS - 2
TILE_MASK_THEN_FIRST = N_PAIR_TILES
TILE_LAST_THEN_MASK = N_PAIR_TILES + 1
TILE_MASKED = N_PAIR_TILES + 2
N_BIAS_TILES = N_PAIR_TILES + 3
SLAB_TILES = SLAB_ROWS // 2


def _slab_base(pair_row0, n_rows):
    return jnp.clip(pair_row0 - WIN_ROWS // 2, 0, n_rows - SLAB_ROWS)


def _bias_tile_ids(n_rows):
    ids = np.zeros((n_rows, SLAB_TILES), np.int32)
    interior = WIN_ROWS // 2 - 1
    for r in range(n_rows):
        base = int(np.clip(r - r % 2 - WIN_ROWS // 2, 0, n_rows - SLAB_ROWS))
        start = int(np.clip(r - WIN_ROWS // 2, 0, n_rows - WIN_ROWS))
        lead, d0 = start - base, start - r + WIN_ROWS - 1
        assert 0 <= lead <= 2 and start + WIN_ROWS <= base + SLAB_ROWS
        pairs = [d0 + 2 * j for j in range(WIN_ROWS // 2)]
        if lead == 0:
            row = pairs + [TILE_MASKED]
        elif lead == 2:
            row = [TILE_MASKED] + pairs
        else:
            assert d0 == interior
            row = [TILE_MASK_THEN_FIRST] + [d0 + 1 + 2 * j for j in range(WIN_ROWS // 2 - 1)] \
                + [TILE_LAST_THEN_MASK]
        ids[r] = row
    return ids.reshape(-1)


def _bias_tile_offsets():
    first, last = WIN_ROWS // 2 - 1, WIN_ROWS // 2 - 1 + WIN_ROWS - 1
    return ([(d, d + 1) for d in range(N_PAIR_TILES)]
            + [(None, first), (last, None), (None, None)])


def _bias_tiles_kernel(w_ref, o_ref):
    q_col = lax.broadcasted_iota(jnp.int32, (GRID_W, LANES), 0)
    lane = lax.broadcasted_iota(jnp.int32, (GRID_W, LANES), 1)
    k_col = lane % GRID_W
    c_start = jnp.clip(q_col - WIN_COLS // 2, 0, GRID_W - WIN_COLS)
    inside = (k_col >= c_start) & (k_col < c_start + WIN_COLS)
    left_half = lane < GRID_W

    def toeplitz(head, d, shift):
        row = jnp.broadcast_to(w_ref[head, d:d + 1, :], (GRID_W, LANES))
        return pltpu.roll(row, shift, axis=1, stride=1, stride_axis=0)

    masked = jnp.full((GRID_W, LANES), MASK_VALUE, F32)
    for head in range(w_ref.shape[0]):
        for tile_id, (left, right) in enumerate(_bias_tile_offsets()):
            lv = masked if left is None else toeplitz(head, left, 0) * LOG2E
            rv = masked if right is None else toeplitz(head, right, GRID_W) * LOG2E
            o_ref[head, tile_id] = jnp.where(inside, jnp.where(left_half, lv, rv), masked)


def _attention_bias(rpb):
    h, n_dr, n_dc = rpb.shape
    assert n_dr == 2 * WIN_ROWS - 1 and n_dc == 2 * WIN_COLS - 1
    w = jnp.concatenate([rpb[..., WIN_COLS - 1:], jnp.zeros((h, n_dr, LANES - n_dc), F32),
                         rpb[..., :WIN_COLS - 1]], axis=-1)
    w = jnp.pad(w, ((0, 0), (0, 2 * WIN_ROWS - n_dr), (0, 0)))
    return pl.pallas_call(
        _bias_tiles_kernel,
        grid=(h // BIAS_HEADS_PER_STEP,),
        in_specs=[pl.BlockSpec((BIAS_HEADS_PER_STEP, 2 * WIN_ROWS, LANES), lambda i: (i, 0, 0))],
        out_specs=pl.BlockSpec((BIAS_HEADS_PER_STEP, N_BIAS_TILES, GRID_W, LANES),
                               lambda i: (i, 0, 0, 0)),
        out_shape=jax.ShapeDtypeStruct((h, N_BIAS_TILES, GRID_W, LANES), F32),
        compiler_params=_params(1),
        name="bias_tiles",
    )(w)


def _attention_kernel(ids_ref, q_ref, k_ref, v_ref, kc_ref, vc_ref, g_ref, x_ref, mod_ref,
                      bias_ref, wout_ref, o_ref, s_scr, p_scr, r_scr, o_scr, *, n_rows):
    pair_tokens = 2 * GRID_W
    n_ctx = kc_ref.shape[2]
    slab_keys = SLAB_ROWS * GRID_W
    n_items = (ATT_ROWS // 2) * N_HEAD_PAIRS
    lane = lax.broadcasted_iota(jnp.int32, (pair_tokens, LANES), 1)
    first_head = lane < HEAD_DIM
    nt = (((1,), (1,)), ((), ()))

    def locate(item):
        item = jnp.asarray(item, jnp.int32)
        pb, hp = item // N_HEAD_PAIRS, item % N_HEAD_PAIRS
        r0 = pl.program_id(1) * ATT_ROWS + 2 * pb
        q_off = pl.multiple_of(pb * pair_tokens, pair_tokens)
        k_off = pl.multiple_of(_slab_base(r0, n_rows) * GRID_W, pair_tokens)
        return hp, r0, q_off, k_off

    def scores(item, slot):
        hp, r0, q_off, k_off = locate(item)
        qp = q_ref[0, hp, pl.ds(q_off, pair_tokens), :]
        zero = jnp.zeros_like(qp)
        qa = jnp.where(first_head, qp, zero)
        qb = jnp.where(first_head, zero, qp)
        q4 = jnp.concatenate([qa[:GRID_W], qb[:GRID_W], qa[GRID_W:], qb[GRID_W:]], axis=0)
        s_scr[slot, :, :n_ctx] = lax.dot_general(q4, kc_ref[0, hp], nt, preferred_element_type=F32)
        s_scr[slot, :, n_ctx:] = lax.dot_general(
            q4, k_ref[0, hp, pl.ds(k_off, slab_keys), :], nt, preferred_element_type=F32)

    def softmax(item, slot):
        hp, r0, _, _ = locate(item)
        bias = jnp.concatenate(
            [jnp.concatenate([bias_ref[2 * hp + a, ids_ref[(r0 + i) * SLAB_TILES + j]]
                              for i in range(2) for a in range(2)], axis=0)
             for j in range(SLAB_TILES)], axis=1)
        s_ctx = s_scr[slot, :, :n_ctx]
        s_loc = s_scr[slot, :, n_ctx:] + bias
        m = jnp.maximum(jnp.max(s_loc, axis=1, keepdims=True),
                        jnp.max(s_ctx, axis=1, keepdims=True))
        p_ctx = jnp.exp2(s_ctx - m)
        p_loc = jnp.exp2(s_loc - m)
        denom = jnp.sum(p_loc, axis=1, keepdims=True) + jnp.sum(p_ctx, axis=1, keepdims=True)
        p_scr[slot, :, :n_ctx] = p_ctx.astype(BF16)
        p_scr[slot, :, n_ctx:] = p_loc.astype(BF16)
        r_scr[slot] = jnp.broadcast_to(1.0 / denom, r_scr.shape[1:])

    def values(item, slot):
        hp, _, q_off, k_off = locate(item)
        pv = (jnp.dot(p_scr[slot, :, :n_ctx], vc_ref[0, hp], preferred_element_type=F32)
              + jnp.dot(p_scr[slot, :, n_ctx:], v_ref[0, hp, pl.ds(k_off, slab_keys), :],
                        preferred_element_type=F32))
        pv = pv * r_scr[slot]
        out_a = jnp.concatenate([pv[:GRID_W], pv[2 * GRID_W:3 * GRID_W]], axis=0)
        out_b = jnp.concatenate([pv[GRID_W:2 * GRID_W], pv[3 * GRID_W:]], axis=0)
        o_scr[hp, pl.ds(q_off, pair_tokens), :] = jnp.where(first_head, out_a, out_b)

    scores(0, 0)
    scores(1, 1)
    softmax(0, 0)

    def trip(t, carry):
        for i in range(PIPE_UNROLL):
            item = PIPE_UNROLL * t + i
            scores(item + 2, (i + 2) % PIPE_SLOTS)
            softmax(item + 1, (i + 1) % PIPE_SLOTS)
            values(item, i % PIPE_SLOTS)
        return carry

    assert (n_items - 2) % PIPE_UNROLL == 0 and PIPE_UNROLL % PIPE_SLOTS == 0
    lax.fori_loop(0, (n_items - 2) // PIPE_UNROLL, trip, 0)
    softmax(n_items - 1, (n_items - 1) % PIPE_SLOTS)
    values(n_items - 2, (n_items - 2) % PIPE_SLOTS)
    values(n_items - 1, (n_items - 1) % PIPE_SLOTS)

    gate = mod_ref[0, 0, 2:3, :]
    o = jnp.concatenate([o_scr[hp] for hp in range(N_HEAD_PAIRS)], axis=1)
    y = (o * _silu(g_ref[0])).astype(BF16)
    y = jnp.dot(y, wout_ref[0], preferred_element_type=F32)
    o_ref[0] = x_ref[0] + gate * y


def _attention_layer(x, q, k, v, kc, vc, g, mods, layer, rpb, w_out, j):
    b, seq_len, d = x.shape
    n_ctx = kc.shape[2]
    n_rows = seq_len // GRID_W
    assert n_rows % ATT_ROWS == 0 and n_rows >= SLAB_ROWS and ATT_ROWS % 2 == 0
    tile = ATT_ROWS * GRID_W
    n_keys = n_ctx + SLAB_ROWS * GRID_W
    bias = _attention_bias(rpb)
    ids = jnp.asarray(_bias_tile_ids(n_rows))
    tile_spec = pl.BlockSpec((1, tile, d), lambda bi, i: (bi, i, 0))
    q_spec = pl.BlockSpec((1, N_HEAD_PAIRS, tile, LANES), lambda bi, i: (bi, 0, i, 0))
    seq_spec = pl.BlockSpec((1, N_HEAD_PAIRS, seq_len, LANES), lambda bi, i: (bi, 0, 0, 0))
    ctx_spec = pl.BlockSpec((1, N_HEAD_PAIRS, n_ctx, LANES), lambda bi, i: (bi, 0, 0, 0))
    return pl.pallas_call(
        functools.partial(_attention_kernel, n_rows=n_rows),
        grid=(b, n_rows // ATT_ROWS),
        in_specs=[pl.BlockSpec(memory_space=pltpu.SMEM),
                  q_spec, seq_spec, seq_spec, ctx_spec, ctx_spec, tile_spec, tile_spec,
                  _mod_block(mods, layer, None),
                  pl.BlockSpec(bias.shape, lambda bi, i: (0, 0, 0, 0),
                               pipeline_mode=pl.Buffered(1)),
                  _layer_block(w_out, j)],
        out_specs=tile_spec,
        out_shape=jax.ShapeDtypeStruct(x.shape, F32),
        scratch_shapes=[pltpu.VMEM((PIPE_SLOTS, 4 * GRID_W, n_keys), F32),
                        pltpu.VMEM((PIPE_SLOTS, 4 * GRID_W, n_keys), BF16),
                        pltpu.VMEM((PIPE_SLOTS, 4 * GRID_W, LANES), F32),
                        pltpu.VMEM((N_HEAD_PAIRS, tile, LANES), F32)],
        compiler_params=_params(2),
        name="na_attention",
    )(ids, q, k, v, kc, vc, g, x, mods, bias, w_out)


def kernel(x, c, ctx, c_ctx, norm_g, ada_w, ada_b, pool_w_in, pool_w_grp, pool_scale, pool_w_out,
           na_w_in, na_rpb, na_w_out, conv_w_in, conv_dw, conv_db, conv_w_out, final_g):
    depth = norm_g.shape[0]
    batch, _, d = x.shape
    assert batch < COND_ROWS and WIN_ROWS // 2 <= HALO

    cond = jnp.zeros((COND_ROWS, d), F32).at[:batch].set(c).at[batch].set(c_ctx)
    mods = _modulation(cond, ada_w, ada_b).reshape(depth, COND_ROWS, 3, d)
    norm_g = norm_g.reshape(depth, 1, d)

    pool_weights = (pool_w_in.astype(BF16), pool_w_grp.astype(BF16),
                    pool_scale.reshape(-1, 1, WIDTH), pool_w_out.astype(BF16))
    conv_weights = (conv_w_in.astype(BF16), conv_dw, conv_db.reshape(-1, 1, WIDTH),
                    conv_w_out.astype(BF16))
    na_w_in, na_w_out = na_w_in.astype(BF16), na_w_out.astype(BF16)

    last_ctx_reader = max([i for i in range(depth) if i % N_MIXERS == 1], default=-1)
    for i in range(depth):
        kind, j = i % N_MIXERS, i // N_MIXERS
        update_ctx = i < last_ctx_reader
        fg = final_g if i == depth - 1 else None
        if kind != 1:
            kernel_fn, weights = ((_pool_kernel, pool_weights) if kind == 0
                                  else (_conv_kernel, conv_weights))
            run = functools.partial(_mixer_layer, kernel_fn, mods=mods, layer=i, norm_g=norm_g,
                                    weights=weights, j=j)
            if update_ctx:
                ctx = run(x=ctx, mod_row=batch, final_g=None, tile=CTX_TILE, name=f"ctx_layer{i}")
            x = run(x=x, mod_row=None, final_g=fg, tile=X_TILE, name=f"layer{i}")
            continue

        if update_ctx:
            raise NotImplementedError("context output of a neighbourhood-attention layer")
        if fg is not None:
            raise NotImplementedError("final norm after a neighbourhood-attention layer")
        q, k, v, g = _project(x, mods, i, None, norm_g, na_w_in, j, (0, 1, 2, 3),
                              (BF16, BF16, BF16, F32),
                              (HEAD_DIM ** -0.5 * LOG2E, 1.0, 1.0, 1.0), X_TILE, f"na_project{i}")
        kc, vc = _project(ctx, mods, i, batch, norm_g, na_w_in, j, (1, 2), (BF16, BF16),
                          (1.0, 1.0), CTX_TILE, f"na_ctx_project{i}")
        x = _attention_layer(x, q, k, v, kc, vc, g, mods, i, na_rpb[j], na_w_out, j)
    return x
```

```python
import functools
import math

import numpy as np

import jax
import jax.numpy as jnp
from jax import lax
from jax.experimental import pallas as pl
from jax.experimental.pallas import tpu as pltpu

D_MODEL = 1024
WIDTH = D_MODEL
GRID_W = 64
N_MIXERS = 3
POOL_WINDOWS = (2, 4, 8, 16)
POOL_GROUP = WIDTH // len(POOL_WINDOWS)
HEAD_DIM = 64
N_HEADS = WIDTH // HEAD_DIM
WIN_ROWS = 8
WIN_COLS = 16
EPS = 1e-6

HALO = 8
LANES = 128
N_HEAD_PAIRS = WIDTH // LANES
MASK_VALUE = -1e30
LOG2E = math.log2(math.e)
VMEM_LIMIT = 56 * 1024 * 1024

X_TILE = 1024
CTX_TILE = 256
ATT_ROWS = 8
SLAB_ROWS = WIN_ROWS + 2
PIPE_SLOTS = 2
PIPE_UNROLL = 10
BIAS_HEADS_PER_STEP = 8
CONV_BLOCK = 256
COND_ROWS = 16

F32 = jnp.float32
BF16 = jnp.bfloat16


def _silu(x):
    return x / (1.0 + jnp.exp(-x))


def _mod_norm(x, norm_g, shift, scale):
    ms = jnp.mean(x * x, axis=-1, keepdims=True)
    return (x * lax.rsqrt(ms + EPS)) * (norm_g * (1.0 + scale)) + shift


def _params(n_axes):
    return pltpu.CompilerParams(
        dimension_semantics=("arbitrary",) * n_axes, vmem_limit_bytes=VMEM_LIMIT)


def _layer_block(arr, j, col=None):
    shape = (1,) + arr.shape[1:]
    index = (j,) + (0,) * (arr.ndim - 1)
    if col is not None:
        shape = shape[:-1] + (WIDTH,)
        index = index[:-1] + (col,)
    return pl.BlockSpec(shape, lambda *_: index, pipeline_mode=pl.Buffered(1))


def _mod_block(mods, layer, row):
    d = mods.shape[-1]
    if row is None:
        return pl.BlockSpec((1, 1, 3, d), lambda bi, i: (layer, bi, 0, 0))
    return pl.BlockSpec((1, 1, 3, d), lambda bi, i: (layer, row, 0, 0))


def _modulation_kernel(cond_ref, w_ref, b_ref, o_ref):
    s = _silu(cond_ref[...]).astype(BF16)
    o_ref[0] = jnp.dot(s, w_ref[0].astype(BF16), preferred_element_type=F32) + b_ref[0]


def _modulation(cond, ada_w, ada_b):
    depth, d, n = ada_w.shape
    rows = cond.shape[0]
    tn = 1024
    return pl.pallas_call(
        _modulation_kernel,
        grid=(depth, n // tn),
        in_specs=[
            pl.BlockSpec((rows, d), lambda i, j: (0, 0)),
            pl.BlockSpec((1, d, tn), lambda i, j: (i, 0, j)),
            pl.BlockSpec((1, 1, tn), lambda i, j: (i, 0, j)),
        ],
        out_specs=pl.BlockSpec((1, rows, tn), lambda i, j: (i, 0, j)),
        out_shape=jax.ShapeDtypeStruct((depth, rows, n), F32),
        compiler_params=_params(2),
        name="modulation",
    )(cond, ada_w, ada_b.reshape(depth, 1, n))


def _normed_rows(x_ref, xp_ref, xn_ref, mod_ref, ng_ref):
    shift, scale = mod_ref[0, 0, 0:1, :], mod_ref[0, 0, 1:2, :]
    xe = jnp.concatenate([xp_ref[0], x_ref[0], xn_ref[0]], axis=0)
    he = _mod_norm(xe, ng_ref[0], shift, scale)
    return he.astype(BF16), he[HALO:he.shape[0] - HALO].astype(BF16)


def _zero_outside_sequence(e, tile):
    i = pl.program_id(1)
    keep_prev = (i > 0).astype(F32)
    keep_next = (i < pl.num_programs(1) - 1).astype(F32)
    return jnp.concatenate(
        [e[:HALO] * keep_prev, e[HALO:HALO + tile], e[HALO + tile:] * keep_next], axis=0)


def _window_sum(e, width, tile):
    n = e.shape[0]
    half = width // 2
    f = e
    k = 1
    while k < half:
        f = f + pltpu.roll(f, n - k, axis=0)
        k *= 2
    if half == HALO:
        return f[:tile] + f[HALO:HALO + tile]
    return (pltpu.roll(f, half, axis=0) + f)[HALO:HALO + tile]


def _window_mean(ws, width, tile, seq_len):
    half = width // 2
    row = lax.broadcasted_iota(jnp.int32, (HALO, 1), 0)

    def inv_count(first_row):
        t = row + (pl.program_id(1) * tile + first_row)
        cnt = jnp.minimum(t + half, seq_len) - jnp.maximum(t - half, 0)
        return 1.0 / cnt.astype(F32)

    return jnp.concatenate([ws[:HALO] * inv_count(0),
                            ws[HALO:tile - HALO] * (1.0 / width),
                            ws[tile - HALO:] * inv_count(tile - HALO)], axis=0)


def _finish(x, y, gate, o_ref, fg_ref):
    out = x + gate * y
    if fg_ref is not None:
        ms = jnp.mean(out * out, axis=-1, keepdims=True)
        out = (out * lax.rsqrt(ms + EPS)) * fg_ref[...]
    o_ref[0] = out


def _pool_kernel(x_ref, xp_ref, xn_ref, mod_ref, ng_ref, win_ref, wgrp_ref, ps_ref, wout_ref,
                 *rest, tile, seq_len):
    fg_ref, o_ref = rest if len(rest) == 2 else (None, rest[0])
    he, hm = _normed_rows(x_ref, xp_ref, xn_ref, mod_ref, ng_ref)
    u = jnp.dot(he, win_ref[0, :, :WIDTH], preferred_element_type=F32)
    u = _zero_outside_sequence(u, tile)
    g = jnp.dot(hm, win_ref[0, :, WIDTH:], preferred_element_type=F32)
    mixed = []
    for gi, width in enumerate(POOL_WINDOWS):
        ue = u[:, gi * POOL_GROUP:(gi + 1) * POOL_GROUP]
        pooled = _window_mean(_window_sum(ue, width, tile), width, tile, seq_len)
        diff = (pooled - ue[HALO:HALO + tile]).astype(BF16)
        mixed.append(jnp.dot(diff, wgrp_ref[0, gi], preferred_element_type=F32))
    mixed = jnp.concatenate(mixed, axis=1)
    z = (mixed * ps_ref[0] * _silu(g)).astype(BF16)
    y = jnp.dot(z, wout_ref[0], preferred_element_type=F32)
    _finish(x_ref[0], y, mod_ref[0, 0, 2:3, :], o_ref, fg_ref)


def _conv_kernel(x_ref, xp_ref, xn_ref, mod_ref, ng_ref, win_ref, dw_ref, db_ref, wout_ref,
                 *rest, tile, seq_len):
    fg_ref, o_ref = rest if len(rest) == 2 else (None, rest[0])
    he, hm = _normed_rows(x_ref, xp_ref, xn_ref, mod_ref, ng_ref)
    n = tile + 2 * HALO
    rows = slice(HALO, HALO + tile)
    y = None
    for ci in range(WIDTH // CONV_BLOCK):
        cols = slice(ci * CONV_BLOCK, (ci + 1) * CONV_BLOCK)

        def proj(h, part):
            w = win_ref[0, :, part * WIDTH + ci * CONV_BLOCK:part * WIDTH + (ci + 1) * CONV_BLOCK]
            return jnp.dot(h, w, preferred_element_type=F32)

        z = _zero_outside_sequence(proj(he, 1) * proj(he, 2), tile)
        conv = (dw_ref[0, 0:1, cols] * pltpu.roll(z, 1, axis=0)[rows]
                + dw_ref[0, 1:2, cols] * z[rows]
                + dw_ref[0, 2:3, cols] * pltpu.roll(z, n - 1, axis=0)[rows]
                + db_ref[0, :, cols])
        yc = (proj(hm, 0) * conv * _silu(proj(hm, 3))).astype(BF16)
        part = jnp.dot(yc, wout_ref[0, cols, :], preferred_element_type=F32)
        y = part if y is None else y + part
    _finish(x_ref[0], y, mod_ref[0, 0, 2:3, :], o_ref, fg_ref)


def _mixer_layer(kernel_fn, x, mods, layer, mod_row, norm_g, weights, j, final_g, tile, name):
    b, seq_len, d = x.shape
    tile = min(tile, seq_len)
    halo_blocks = seq_len // HALO
    per_tile = tile // HALO
    in_specs = [
        pl.BlockSpec((1, tile, d), lambda bi, i: (bi, i, 0)),
        pl.BlockSpec((1, HALO, d), lambda bi, i: (bi, jnp.maximum(i * per_tile - 1, 0), 0)),
        pl.BlockSpec((1, HALO, d),
                     lambda bi, i: (bi, jnp.minimum((i + 1) * per_tile, halo_blocks - 1), 0)),
        _mod_block(mods, layer, mod_row),
        _layer_block(norm_g, layer),
    ] + [_layer_block(w, j) for w in weights]
    args = [x, x, x, mods, norm_g] + list(weights)
    if final_g is not None:
        in_specs.append(pl.BlockSpec((1, d), lambda bi, i: (0, 0)))
        args.append(final_g.reshape(1, d))
    return pl.pallas_call(
        functools.partial(kernel_fn, tile=tile, seq_len=seq_len),
        grid=(b, seq_len // tile),
        in_specs=in_specs,
        out_specs=pl.BlockSpec((1, tile, d), lambda bi, i: (bi, i, 0)),
        out_shape=jax.ShapeDtypeStruct(x.shape, F32),
        compiler_params=_params(2),
        name=name,
    )(*args)


def _project_kernel(x_ref, mod_ref, ng_ref, *refs, scales, layouts):
    n = len(scales)
    w_refs, o_refs = refs[:n], refs[n:]
    shift, scale = mod_ref[0, 0, 0:1, :], mod_ref[0, 0, 1:2, :]
    h = _mod_norm(x_ref[0], ng_ref[0], shift, scale).astype(BF16)
    for w_ref, o_ref, s, layout in zip(w_refs, o_refs, scales, layouts):
        p = jnp.dot(h, w_ref[0], preferred_element_type=F32)
        if s != 1.0:
            p = p * s
        if layout == "flat":
            o_ref[0] = p.astype(o_ref.dtype)
            continue
        for hp in range(N_HEAD_PAIRS):
            block = p[:, hp * LANES:(hp + 1) * LANES]
            o_ref[0, hp] = (block.T if layout == "pairs_t" else block).astype(o_ref.dtype)


def _project(x, mods, layer, mod_row, norm_g, w, j, cols, layouts, scales, tile, name):
    b, seq_len, d = x.shape
    tile = min(tile, seq_len)
    assert tile % LANES == 0
    specs = {
        "flat": (pl.BlockSpec((1, tile, WIDTH), lambda bi, i: (bi, i, 0)),
                 jax.ShapeDtypeStruct((b, seq_len, WIDTH), F32)),
        "pairs": (pl.BlockSpec((1, N_HEAD_PAIRS, tile, LANES), lambda bi, i: (bi, 0, i, 0)),
                  jax.ShapeDtypeStruct((b, N_HEAD_PAIRS, seq_len, LANES), BF16)),
        "pairs_t": (pl.BlockSpec((1, N_HEAD_PAIRS, LANES, tile), lambda bi, i: (bi, 0, 0, i)),
                    jax.ShapeDtypeStruct((b, N_HEAD_PAIRS, LANES, seq_len), BF16)),
    }
    out_specs = [specs[name_][0] for name_ in layouts]
    out_shape = [specs[name_][1] for name_ in layouts]
    return pl.pallas_call(
        functools.partial(_project_kernel, scales=scales, layouts=layouts),
        grid=(b, seq_len // tile),
        in_specs=[
            pl.BlockSpec((1, tile, d), lambda bi, i: (bi, i, 0)),
            _mod_block(mods, layer, mod_row),
            _layer_block(norm_g, layer),
        ] + [_layer_block(w, j, col) for col in cols],
        out_specs=out_specs,
        out_shape=out_shape,
        compiler_params=_params(2),
        name=name,
    )(x, mods, norm_g, *([w] * len(cols)))


SLAB_TILES = SLAB_ROWS // 2


def _slab_base(pair_row0, n_rows):
    return jnp.clip(pair_row0 - WIN_ROWS // 2, 0, n_rows - SLAB_ROWS)


def _bias_tile_plan(n_rows):
    tiles, ids = [], []
    for r0 in range(0, n_rows, 2):
        base = int(np.clip(r0 - WIN_ROWS // 2, 0, n_rows - SLAB_ROWS))
        for j in range(SLAB_TILES):
            quad = []
            for key_row in (base + 2 * j, base + 2 * j + 1):
                for r in (r0, r0 + 1):
                    start = int(np.clip(r - WIN_ROWS // 2, 0, n_rows - WIN_ROWS))
                    assert base <= start and start + WIN_ROWS <= base + SLAB_ROWS
                    inside = start <= key_row < start + WIN_ROWS
                    quad.append(key_row - r + WIN_ROWS - 1 if inside else None)
            quad = tuple(quad)
            if quad not in tiles:
                tiles.append(quad)
            ids.append(tiles.index(quad))
    return tiles, np.asarray(ids, np.int32)


def _bias_tiles_kernel(w_ref, o_ref, *, tiles):
    k_col = lax.broadcasted_iota(jnp.int32, (GRID_W, LANES), 0)
    lane = lax.broadcasted_iota(jnp.int32, (GRID_W, LANES), 1)
    q_col = lane % GRID_W
    c_start = jnp.clip(q_col - WIN_COLS // 2, 0, GRID_W - WIN_COLS)
    inside = (k_col >= c_start) & (k_col < c_start + WIN_COLS)
    first_query_row = lane < GRID_W
    masked = jnp.full((GRID_W, LANES), MASK_VALUE, F32)

    def toeplitz(head, d, shift):
        if d is None:
            return masked
        row = jnp.broadcast_to(w_ref[head, d:d + 1, :], (GRID_W, LANES))
        return pltpu.roll(row, shift, axis=1, stride=1, stride_axis=0) * LOG2E

    for head in range(w_ref.shape[0]):
        for tile_id, quad in enumerate(tiles):
            halves = [jnp.where(inside,
                                jnp.where(first_query_row, toeplitz(head, quad[2 * kr], 0),
                                          toeplitz(head, quad[2 * kr + 1], GRID_W)),
                                masked) for kr in range(2)]
            o_ref[head, tile_id] = jnp.concatenate(halves, axis=0)


def _attention_bias(rpb, tiles):
    h, n_dr, n_dc = rpb.shape
    assert n_dr == 2 * WIN_ROWS - 1 and n_dc == 2 * WIN_COLS - 1
    rev = rpb[..., ::-1]
    w = jnp.concatenate([rev[..., WIN_COLS - 1:], jnp.zeros((h, n_dr, LANES - n_dc), F32),
                         rev[..., :WIN_COLS - 1]], axis=-1)
    w = jnp.pad(w, ((0, 0), (0, 2 * WIN_ROWS - n_dr), (0, 0)))
    return pl.pallas_call(
        functools.partial(_bias_tiles_kernel, tiles=tiles),
        grid=(h // BIAS_HEADS_PER_STEP,),
        in_specs=[pl.BlockSpec((BIAS_HEADS_PER_STEP, 2 * WIN_ROWS, LANES), lambda i: (i, 0, 0))],
        out_specs=pl.BlockSpec((BIAS_HEADS_PER_STEP, len(tiles), LANES, LANES),
                               lambda i: (i, 0, 0, 0)),
        out_shape=jax.ShapeDtypeStruct((h, len(tiles), LANES, LANES), F32),
        compiler_params=_params(1),
        name="bias_tiles",
    )(w)


def _attention_kernel(ids_ref, q_ref, k_ref, v_ref, kc_ref, vc_ref, g_ref, x_ref, mod_ref,
                      bias_ref, wout_ref, o_ref, s_scr, p_scr, r_scr, o_scr, *, n_rows):
    pair_tokens = 2 * GRID_W
    n_ctx = kc_ref.shape[2]
    slab_keys = SLAB_ROWS * GRID_W
    n_items = (ATT_ROWS // 2) * N_HEAD_PAIRS
    first_head = lax.broadcasted_iota(jnp.int32, (LANES, pair_tokens), 0) < HEAD_DIM

    def locate(item):
        item = jnp.asarray(item, jnp.int32)
        pb, hp = item // N_HEAD_PAIRS, item % N_HEAD_PAIRS
        pair = pl.program_id(1) * (ATT_ROWS // 2) + pb
        q_off = pl.multiple_of(pb * pair_tokens, pair_tokens)
        k_off = pl.multiple_of(_slab_base(2 * pair, n_rows) * GRID_W, pair_tokens)
        return hp, pair, q_off, k_off

    def scores(item, slot):
        hp, _, q_off, k_off = locate(item)
        qt = q_ref[0, hp, :, pl.ds(q_off, pair_tokens)]
        zero = jnp.zeros_like(qt)
        q2 = jnp.concatenate([jnp.where(first_head, qt, zero),
                              jnp.where(first_head, zero, qt)], axis=1)
        s_scr[slot, :n_ctx, :] = jnp.dot(kc_ref[0, hp], q2, preferred_element_type=F32)
        s_scr[slot, n_ctx:, :] = jnp.dot(k_ref[0, hp, pl.ds(k_off, slab_keys), :], q2,
                                         preferred_element_type=F32)

    def softmax(item, slot):
        hp, pair, _, _ = locate(item)
        bias = jnp.concatenate(
            [jnp.concatenate([bias_ref[2 * hp + a, ids_ref[pair * SLAB_TILES + j]]
                              for a in range(2)], axis=1)
             for j in range(SLAB_TILES)], axis=0)
        s_ctx = s_scr[slot, :n_ctx, :]
        s_loc = s_scr[slot, n_ctx:, :] + bias
        m = jnp.maximum(jnp.max(s_loc, axis=0, keepdims=True),
                        jnp.max(s_ctx, axis=0, keepdims=True))
        p_ctx = jnp.exp2(s_ctx - m)
        p_loc = jnp.exp2(s_loc - m)
        denom = jnp.sum(p_loc, axis=0, keepdims=True) + jnp.sum(p_ctx, axis=0, keepdims=True)
        p_scr[slot, :n_ctx, :] = p_ctx.astype(BF16)
        p_scr[slot, n_ctx:, :] = p_loc.astype(BF16)
        r_scr[slot] = jnp.broadcast_to(1.0 / denom, r_scr.shape[1:])

    def values(item, slot):
        hp, _, q_off, k_off = locate(item)
        pv = (jnp.dot(vc_ref[0, hp], p_scr[slot, :n_ctx, :], preferred_element_type=F32)
              + jnp.dot(v_ref[0, hp, :, pl.ds(k_off, slab_keys)], p_scr[slot, n_ctx:, :],
                        preferred_element_type=F32))
        pv = pv * r_scr[slot, 0:1, :]
        out = jnp.where(first_head, pv[:, :pair_tokens], pv[:, pair_tokens:])
        o_scr[hp, pl.ds(q_off, pair_tokens), :] = out.T

    scores(0, 0)
    scores(1, 1)
    softmax(0, 0)

    def trip(t, carry):
        for i in range(PIPE_UNROLL):
            item = PIPE_UNROLL * t + i
            scores(item + 2, (i + 2) % PIPE_SLOTS)
            softmax(item + 1, (i + 1) % PIPE_SLOTS)
            values(item, i % PIPE_SLOTS)
        return carry

    assert (n_items - 2) % PIPE_UNROLL == 0 and PIPE_UNROLL % PIPE_SLOTS == 0
    lax.fori_loop(0, (n_items - 2) // PIPE_UNROLL, trip, 0)
    softmax(n_items - 1, (n_items - 1) % PIPE_SLOTS)
    values(n_items - 2, (n_items - 2) % PIPE_SLOTS)
    values(n_items - 1, (n_items - 1) % PIPE_SLOTS)

    gate = mod_ref[0, 0, 2:3, :]
    o = jnp.concatenate([o_scr[hp] for hp in range(N_HEAD_PAIRS)], axis=1)
    y = (o * _silu(g_ref[0])).astype(BF16)
    y = jnp.dot(y, wout_ref[0], preferred_element_type=F32)
    o_ref[0] = x_ref[0] + gate * y


def _attention_layer(x, q, k, v, kc, vc, g, mods, layer, rpb, w_out, j):
    b, seq_len, d = x.shape
    n_ctx = kc.shape[2]
    n_rows = seq_len // GRID_W
    assert n_rows % ATT_ROWS == 0 and n_rows >= SLAB_ROWS and ATT_ROWS % 2 == 0
    tile = ATT_ROWS * GRID_W
    n_keys = n_ctx + SLAB_ROWS * GRID_W
    tiles, ids = _bias_tile_plan(n_rows)
    bias = _attention_bias(rpb, tiles)
    tile_spec = pl.BlockSpec((1, tile, d), lambda bi, i: (bi, i, 0))
    q_spec = pl.BlockSpec((1, N_HEAD_PAIRS, LANES, tile), lambda bi, i: (bi, 0, 0, i))
    seq_spec = pl.BlockSpec((1, N_HEAD_PAIRS, seq_len, LANES), lambda bi, i: (bi, 0, 0, 0))
    ctx_spec = pl.BlockSpec((1, N_HEAD_PAIRS, n_ctx, LANES), lambda bi, i: (bi, 0, 0, 0))
    seq_t_spec = pl.BlockSpec((1, N_HEAD_PAIRS, LANES, seq_len), lambda bi, i: (bi, 0, 0, 0))
    ctx_t_spec = pl.BlockSpec((1, N_HEAD_PAIRS, LANES, n_ctx), lambda bi, i: (bi, 0, 0, 0))
    return pl.pallas_call(
        functools.partial(_attention_kernel, n_rows=n_rows),
        grid=(b, n_rows // ATT_ROWS),
        in_specs=[pl.BlockSpec(memory_space=pltpu.SMEM),
                  q_spec, seq_spec, seq_t_spec, ctx_spec, ctx_t_spec, tile_spec, tile_spec,
                  _mod_block(mods, layer, None),
                  pl.BlockSpec(bias.shape, lambda bi, i: (0, 0, 0, 0),
                               pipeline_mode=pl.Buffered(1)),
                  _layer_block(w_out, j)],
        out_specs=tile_spec,
        out_shape=jax.ShapeDtypeStruct(x.shape, F32),
        scratch_shapes=[pltpu.VMEM((PIPE_SLOTS, n_keys, 4 * GRID_W), F32),
                        pltpu.VMEM((PIPE_SLOTS, n_keys, 4 * GRID_W), BF16),
                        pltpu.VMEM((PIPE_SLOTS, 8, 4 * GRID_W), F32),
                        pltpu.VMEM((N_HEAD_PAIRS, tile, LANES), F32)],
        compiler_params=_params(2),
        name="na_attention",
    )(jnp.asarray(ids), q, k, v, kc, vc, g, x, mods, bias, w_out)


def kernel(x, c, ctx, c_ctx, norm_g, ada_w, ada_b, pool_w_in, pool_w_grp, pool_scale, pool_w_out,
           na_w_in, na_rpb, na_w_out, conv_w_in, conv_dw, conv_db, conv_w_out, final_g):
    depth = norm_g.shape[0]
    batch, _, d = x.shape
    assert batch < COND_ROWS and WIN_ROWS // 2 <= HALO

    cond = jnp.zeros((COND_ROWS, d), F32).at[:batch].set(c).at[batch].set(c_ctx)
    mods = _modulation(cond, ada_w, ada_b).reshape(depth, COND_ROWS, 3, d)
    norm_g = norm_g.reshape(depth, 1, d)

    pool_weights = (pool_w_in.astype(BF16), pool_w_grp.astype(BF16),
                    pool_scale.reshape(-1, 1, WIDTH), pool_w_out.astype(BF16))
    conv_weights = (conv_w_in.astype(BF16), conv_dw, conv_db.reshape(-1, 1, WIDTH),
                    conv_w_out.astype(BF16))
    na_w_in, na_w_out = na_w_in.astype(BF16), na_w_out.astype(BF16)

    last_ctx_reader = max([i for i in range(depth) if i % N_MIXERS == 1], default=-1)
    for i in range(depth):
        kind, j = i % N_MIXERS, i // N_MIXERS
        update_ctx = i < last_ctx_reader
        fg = final_g if i == depth - 1 else None
        if kind != 1:
            kernel_fn, weights = ((_pool_kernel, pool_weights) if kind == 0
                                  else (_conv_kernel, conv_weights))
            run = functools.partial(_mixer_layer, kernel_fn, mods=mods, layer=i, norm_g=norm_g,
                                    weights=weights, j=j)
            if update_ctx:
                ctx = run(x=ctx, mod_row=batch, final_g=None, tile=CTX_TILE, name=f"ctx_layer{i}")
            x = run(x=x, mod_row=None, final_g=fg, tile=X_TILE, name=f"layer{i}")
            continue

        if update_ctx:
            raise NotImplementedError("context output of a neighbourhood-attention layer")
        if fg is not None:
            raise NotImplementedError("final norm after a neighbourhood-attention layer")
        q, k, v, g = _project(x, mods, i, None, norm_g, na_w_in, j, (0, 1, 2, 3),
                              ("pairs_t", "pairs", "pairs_t", "flat"),
                              (HEAD_DIM ** -0.5 * LOG2E, 1.0, 1.0, 1.0), X_TILE, f"na_project{i}")
        kc, vc = _project(ctx, mods, i, batch, norm_g, na_w_in, j, (1, 2), ("pairs", "pairs_t"),
                          (1.0, 1.0), CTX_TILE, f"na_ctx_project{i}")
        x = _attention_layer(x, q, k, v, kc, vc, g, mods, i, na_rpb[j], na_w_out, j)
    return x
```

```python
import functools
import math

import numpy as np

import jax
import jax.numpy as jnp
from jax import lax
from jax.experimental import pallas as pl
from jax.experimental.pallas import tpu as pltpu

D_MODEL = 1024
WIDTH = D_MODEL
GRID_W = 64
N_MIXERS = 3
POOL_WINDOWS = (2, 4, 8, 16)
POOL_GROUP = WIDTH // len(POOL_WINDOWS)
HEAD_DIM = 64
N_HEADS = WIDTH // HEAD_DIM
WIN_ROWS = 8
WIN_COLS = 16
EPS = 1e-6

HALO = 8
LANES = 128
N_HEAD_PAIRS = WIDTH // LANES
MASK_VALUE = -1e30
LOG2E = math.log2(math.e)
VMEM_LIMIT = 56 * 1024 * 1024

X_TILE = 1024
CTX_TILE = 256
ATT_ROWS = 8
SLAB_ROWS = WIN_ROWS + 2
PIPE_SLOTS = 2
PIPE_UNROLL = 6
BIAS_HEADS_PER_STEP = 8
CONV_BLOCK = 256
COND_ROWS = 16

F32 = jnp.float32
BF16 = jnp.bfloat16


def _silu(x):
    return x / (1.0 + jnp.exp(-x))


def _mod_norm(x, norm_g, shift, scale):
    ms = jnp.mean(x * x, axis=-1, keepdims=True)
    return (x * lax.rsqrt(ms + EPS)) * (norm_g * (1.0 + scale)) + shift


def _params(n_axes):
    return pltpu.CompilerParams(
        dimension_semantics=("arbitrary",) * n_axes, vmem_limit_bytes=VMEM_LIMIT)


def _layer_block(arr, j, col=None):
    shape = (1,) + arr.shape[1:]
    index = (j,) + (0,) * (arr.ndim - 1)
    if col is not None:
        shape = shape[:-1] + (WIDTH,)
        index = index[:-1] + (col,)
    return pl.BlockSpec(shape, lambda *_: index, pipeline_mode=pl.Buffered(1))


def _mod_block(mods, layer, row):
    d = mods.shape[-1]
    if row is None:
        return pl.BlockSpec((1, 1, 3, d), lambda bi, i: (layer, bi, 0, 0))
    return pl.BlockSpec((1, 1, 3, d), lambda bi, i: (layer, row, 0, 0))


def _modulation_kernel(cond_ref, w_ref, b_ref, o_ref):
    s = _silu(cond_ref[...]).astype(BF16)
    o_ref[0] = jnp.dot(s, w_ref[0].astype(BF16), preferred_element_type=F32) + b_ref[0]


def _modulation(cond, ada_w, ada_b):
    depth, d, n = ada_w.shape
    rows = cond.shape[0]
    tn = 1024
    return pl.pallas_call(
        _modulation_kernel,
        grid=(depth, n // tn),
        in_specs=[
            pl.BlockSpec((rows, d), lambda i, j: (0, 0)),
            pl.BlockSpec((1, d, tn), lambda i, j: (i, 0, j)),
            pl.BlockSpec((1, 1, tn), lambda i, j: (i, 0, j)),
        ],
        out_specs=pl.BlockSpec((1, rows, tn), lambda i, j: (i, 0, j)),
        out_shape=jax.ShapeDtypeStruct((depth, rows, n), F32),
        compiler_params=_params(2),
        name="modulation",
    )(cond, ada_w, ada_b.reshape(depth, 1, n))


def _normed_rows(x_ref, xp_ref, xn_ref, mod_ref, ng_ref):
    shift, scale = mod_ref[0, 0, 0:1, :], mod_ref[0, 0, 1:2, :]
    xe = jnp.concatenate([xp_ref[0], x_ref[0], xn_ref[0]], axis=0)
    he = _mod_norm(xe, ng_ref[0], shift, scale)
    return he.astype(BF16), he[HALO:he.shape[0] - HALO].astype(BF16)


def _zero_outside_sequence(e, tile):
    i = pl.program_id(1)
    keep_prev = (i > 0).astype(F32)
    keep_next = (i < pl.num_programs(1) - 1).astype(F32)
    return jnp.concatenate(
        [e[:HALO] * keep_prev, e[HALO:HALO + tile], e[HALO + tile:] * keep_next], axis=0)


def _window_sum(e, width, tile):
    n = e.shape[0]
    half = width // 2
    f = e
    k = 1
    while k < half:
        f = f + pltpu.roll(f, n - k, axis=0)
        k *= 2
    if half == HALO:
        return f[:tile] + f[HALO:HALO + tile]
    return (pltpu.roll(f, half, axis=0) + f)[HALO:HALO + tile]


def _window_mean(ws, width, tile, seq_len):
    half = width // 2
    row = lax.broadcasted_iota(jnp.int32, (HALO, 1), 0)

    def inv_count(first_row):
        t = row + (pl.program_id(1) * tile + first_row)
        cnt = jnp.minimum(t + half, seq_len) - jnp.maximum(t - half, 0)
        return 1.0 / cnt.astype(F32)

    return jnp.concatenate([ws[:HALO] * inv_count(0),
                            ws[HALO:tile - HALO] * (1.0 / width),
                            ws[tile - HALO:] * inv_count(tile - HALO)], axis=0)


def _finish(x, y, gate, o_ref, fg_ref):
    out = x + gate * y
    if fg_ref is not None:
        ms = jnp.mean(out * out, axis=-1, keepdims=True)
        out = (out * lax.rsqrt(ms + EPS)) * fg_ref[...]
    o_ref[0] = out


def _pool_kernel(x_ref, xp_ref, xn_ref, mod_ref, ng_ref, win_ref, wgrp_ref, ps_ref, wout_ref,
                 *rest, tile, seq_len):
    fg_ref, o_ref = rest if len(rest) == 2 else (None, rest[0])
    he, hm = _normed_rows(x_ref, xp_ref, xn_ref, mod_ref, ng_ref)
    u = jnp.dot(he, win_ref[0, :, :WIDTH], preferred_element_type=F32)
    u = _zero_outside_sequence(u, tile)
    g = jnp.dot(hm, win_ref[0, :, WIDTH:], preferred_element_type=F32)
    mixed = []
    for gi, width in enumerate(POOL_WINDOWS):
        ue = u[:, gi * POOL_GROUP:(gi + 1) * POOL_GROUP]
        pooled = _window_mean(_window_sum(ue, width, tile), width, tile, seq_len)
        diff = (pooled - ue[HALO:HALO + tile]).astype(BF16)
        mixed.append(jnp.dot(diff, wgrp_ref[0, gi], preferred_element_type=F32))
    mixed = jnp.concatenate(mixed, axis=1)
    z = (mixed * ps_ref[0] * _silu(g)).astype(BF16)
    y = jnp.dot(z, wout_ref[0], preferred_element_type=F32)
    _finish(x_ref[0], y, mod_ref[0, 0, 2:3, :], o_ref, fg_ref)


def _conv_kernel(x_ref, xp_ref, xn_ref, mod_ref, ng_ref, win_ref, dw_ref, db_ref, wout_ref,
                 *rest, tile, seq_len):
    fg_ref, o_ref = rest if len(rest) == 2 else (None, rest[0])
    he, hm = _normed_rows(x_ref, xp_ref, xn_ref, mod_ref, ng_ref)
    n = tile + 2 * HALO
    rows = slice(HALO, HALO + tile)
    y = None
    for ci in range(WIDTH // CONV_BLOCK):
        cols = slice(ci * CONV_BLOCK, (ci + 1) * CONV_BLOCK)

        def proj(h, part):
            w = win_ref[0, :, part * WIDTH + ci * CONV_BLOCK:part * WIDTH + (ci + 1) * CONV_BLOCK]
            return jnp.dot(h, w, preferred_element_type=F32)

        z = _zero_outside_sequence(proj(he, 1) * proj(he, 2), tile)
        conv = (dw_ref[0, 0:1, cols] * pltpu.roll(z, 1, axis=0)[rows]
                + dw_ref[0, 1:2, cols] * z[rows]
                + dw_ref[0, 2:3, cols] * pltpu.roll(z, n - 1, axis=0)[rows]
                + db_ref[0, :, cols])
        yc = (proj(hm, 0) * conv * _silu(proj(hm, 3))).astype(BF16)
        part = jnp.dot(yc, wout_ref[0, cols, :], preferred_element_type=F32)
        y = part if y is None else y + part
    _finish(x_ref[0], y, mod_ref[0, 0, 2:3, :], o_ref, fg_ref)


def _mixer_layer(kernel_fn, x, mods, layer, mod_row, norm_g, weights, j, final_g, tile, name):
    b, seq_len, d = x.shape
    tile = min(tile, seq_len)
    halo_blocks = seq_len // HALO
    per_tile = tile // HALO
    in_specs = [
        pl.BlockSpec((1, tile, d), lambda bi, i: (bi, i, 0)),
        pl.BlockSpec((1, HALO, d), lambda bi, i: (bi, jnp.maximum(i * per_tile - 1, 0), 0)),
        pl.BlockSpec((1, HALO, d),
                     lambda bi, i: (bi, jnp.minimum((i + 1) * per_tile, halo_blocks - 1), 0)),
        _mod_block(mods, layer, mod_row),
        _layer_block(norm_g, layer),
    ] + [_layer_block(w, j) for w in weights]
    args = [x, x, x, mods, norm_g] + list(weights)
    if final_g is not None:
        in_specs.append(pl.BlockSpec((1, d), lambda bi, i: (0, 0)))
        args.append(final_g.reshape(1, d))
    return pl.pallas_call(
        functools.partial(kernel_fn, tile=tile, seq_len=seq_len),
        grid=(b, seq_len // tile),
        in_specs=in_specs,
        out_specs=pl.BlockSpec((1, tile, d), lambda bi, i: (bi, i, 0)),
        out_shape=jax.ShapeDtypeStruct(x.shape, F32),
        compiler_params=_params(2),
        name=name,
    )(*args)


def _project_kernel(x_ref, mod_ref, ng_ref, *refs, scales, layouts):
    n = len(scales)
    w_refs, o_refs = refs[:n], refs[n:]
    shift, scale = mod_ref[0, 0, 0:1, :], mod_ref[0, 0, 1:2, :]
    h = _mod_norm(x_ref[0], ng_ref[0], shift, scale).astype(BF16)
    for w_ref, o_ref, s, layout in zip(w_refs, o_refs, scales, layouts):
        p = jnp.dot(h, w_ref[0], preferred_element_type=F32)
        if s != 1.0:
            p = p * s
        if layout == "flat":
            o_ref[0] = p.astype(o_ref.dtype)
            continue
        for hp in range(N_HEAD_PAIRS):
            block = p[:, hp * LANES:(hp + 1) * LANES]
            o_ref[0, hp] = (block.T if layout == "pairs_t" else block).astype(o_ref.dtype)


def _project(x, mods, layer, mod_row, norm_g, w, j, cols, layouts, scales, tile, name):
    b, seq_len, d = x.shape
    tile = min(tile, seq_len)
    assert tile % LANES == 0
    specs = {
        "flat": (pl.BlockSpec((1, tile, WIDTH), lambda bi, i: (bi, i, 0)),
                 jax.ShapeDtypeStruct((b, seq_len, WIDTH), F32)),
        "pairs": (pl.BlockSpec((1, N_HEAD_PAIRS, tile, LANES), lambda bi, i: (bi, 0, i, 0)),
                  jax.ShapeDtypeStruct((b, N_HEAD_PAIRS, seq_len, LANES), BF16)),
        "pairs_t": (pl.BlockSpec((1, N_HEAD_PAIRS, LANES, tile), lambda bi, i: (bi, 0, 0, i)),
                    jax.ShapeDtypeStruct((b, N_HEAD_PAIRS, LANES, seq_len), BF16)),
    }
    out_specs = [specs[name_][0] for name_ in layouts]
    out_shape = [specs[name_][1] for name_ in layouts]
    return pl.pallas_call(
        functools.partial(_project_kernel, scales=scales, layouts=layouts),
        grid=(b, seq_len // tile),
        in_specs=[
            pl.BlockSpec((1, tile, d), lambda bi, i: (bi, i, 0)),
            _mod_block(mods, layer, mod_row),
            _layer_block(norm_g, layer),
        ] + [_layer_block(w, j, col) for col in cols],
        out_specs=out_specs,
        out_shape=out_shape,
        compiler_params=_params(2),
        name=name,
    )(x, mods, norm_g, *([w] * len(cols)))


SLAB_TILES = SLAB_ROWS // 2


def _slab_base(pair_row0, n_rows):
    return jnp.clip(pair_row0 - WIN_ROWS // 2, 0, n_rows - SLAB_ROWS)


def _bias_tile_plan(n_rows):
    tiles, ids = [], []
    for r0 in range(0, n_rows, 2):
        base = int(np.clip(r0 - WIN_ROWS // 2, 0, n_rows - SLAB_ROWS))
        for j in range(SLAB_TILES):
            quad = []
            for key_row in (base + 2 * j, base + 2 * j + 1):
                for r in (r0, r0 + 1):
                    start = int(np.clip(r - WIN_ROWS // 2, 0, n_rows - WIN_ROWS))
                    assert base <= start and start + WIN_ROWS <= base + SLAB_ROWS
                    inside = start <= key_row < start + WIN_ROWS
                    quad.append(key_row - r + WIN_ROWS - 1 if inside else None)
            quad = tuple(quad)
            if quad not in tiles:
                tiles.append(quad)
            ids.append(tiles.index(quad))
    return tiles, np.asarray(ids, np.int32)


def _bias_tiles_kernel(w_ref, o_ref, *, tiles):
    k_col = lax.broadcasted_iota(jnp.int32, (GRID_W, LANES), 0)
    lane = lax.broadcasted_iota(jnp.int32, (GRID_W, LANES), 1)
    q_col = lane % GRID_W
    c_start = jnp.clip(q_col - WIN_COLS // 2, 0, GRID_W - WIN_COLS)
    inside = (k_col >= c_start) & (k_col < c_start + WIN_COLS)
    first_query_row = lane < GRID_W
    masked = jnp.full((GRID_W, LANES), MASK_VALUE, F32)

    def toeplitz(head, d, shift):
        if d is None:
            return masked
        row = jnp.broadcast_to(w_ref[head, d:d + 1, :], (GRID_W, LANES))
        return pltpu.roll(row, shift, axis=1, stride=1, stride_axis=0) * LOG2E

    for head in range(w_ref.shape[0]):
        for tile_id, quad in enumerate(tiles):
            halves = [jnp.where(inside,
                                jnp.where(first_query_row, toeplitz(head, quad[2 * kr], 0),
                                          toeplitz(head, quad[2 * kr + 1], GRID_W)),
                                masked) for kr in range(2)]
            o_ref[head, tile_id] = jnp.concatenate(halves, axis=0)


def _attention_bias(rpb, tiles):
    h, n_dr, n_dc = rpb.shape
    assert n_dr == 2 * WIN_ROWS - 1 and n_dc == 2 * WIN_COLS - 1
    rev = rpb[..., ::-1]
    w = jnp.concatenate([rev[..., WIN_COLS - 1:], jnp.zeros((h, n_dr, LANES - n_dc), F32),
                         rev[..., :WIN_COLS - 1]], axis=-1)
    w = jnp.pad(w, ((0, 0), (0, 2 * WIN_ROWS - n_dr), (0, 0)))
    return pl.pallas_call(
        functools.partial(_bias_tiles_kernel, tiles=tiles),
        grid=(h // BIAS_HEADS_PER_STEP,),
        in_specs=[pl.BlockSpec((BIAS_HEADS_PER_STEP, 2 * WIN_ROWS, LANES), lambda i: (i, 0, 0))],
        out_specs=pl.BlockSpec((BIAS_HEADS_PER_STEP, len(tiles), LANES, LANES),
                               lambda i: (i, 0, 0, 0)),
        out_shape=jax.ShapeDtypeStruct((h, len(tiles), LANES, LANES), F32),
        compiler_params=_params(1),
        name="bias_tiles",
    )(w)


def _attention_kernel(ids_ref, q_ref, k_ref, v_ref, kc_ref, vc_ref, g_ref, x_ref, mod_ref,
                      bias_ref, wout_ref, o_ref, s_scr, p_scr, r_scr, o_scr, *, n_rows):
    pair_tokens = 2 * GRID_W
    n_ctx = kc_ref.shape[2]
    slab_keys = SLAB_ROWS * GRID_W
    n_items = (ATT_ROWS // 2) * N_HEAD_PAIRS
    first_head = lax.broadcasted_iota(jnp.int32, (LANES, pair_tokens), 0) < HEAD_DIM

    def locate(item):
        item = jnp.asarray(item, jnp.int32)
        pb, hp = item // N_HEAD_PAIRS, item % N_HEAD_PAIRS
        pair = pl.program_id(1) * (ATT_ROWS // 2) + pb
        q_off = pl.multiple_of(pb * pair_tokens, pair_tokens)
        k_off = pl.multiple_of(_slab_base(2 * pair, n_rows) * GRID_W, pair_tokens)
        return hp, pair, q_off, k_off

    def scores(item, slot):
        hp, _, q_off, k_off = locate(item)
        qt = q_ref[0, hp, :, pl.ds(q_off, pair_tokens)]
        zero = jnp.zeros_like(qt)
        q2 = jnp.concatenate([jnp.where(first_head, qt, zero),
                              jnp.where(first_head, zero, qt)], axis=1)
        s_scr[slot, :n_ctx, :] = jnp.dot(kc_ref[0, hp], q2, preferred_element_type=F32)
        s_scr[slot, n_ctx:, :] = jnp.dot(k_ref[0, hp, pl.ds(k_off, slab_keys), :], q2,
                                         preferred_element_type=F32)

    def softmax(item, slot):
        hp, pair, _, _ = locate(item)
        bias = jnp.concatenate(
            [jnp.concatenate([bias_ref[2 * hp + a, ids_ref[pair * SLAB_TILES + j]]
                              for a in range(2)], axis=1)
             for j in range(SLAB_TILES)], axis=0)
        s_ctx = s_scr[slot, :n_ctx, :]
        s_loc = s_scr[slot, n_ctx:, :] + bias
        m = jnp.maximum(jnp.max(s_loc, axis=0, keepdims=True),
                        jnp.max(s_ctx, axis=0, keepdims=True))
        p_ctx = jnp.exp2(s_ctx - m)
        p_loc = jnp.exp2(s_loc - m)
        denom = jnp.sum(p_loc, axis=0, keepdims=True) + jnp.sum(p_ctx, axis=0, keepdims=True)
        p_scr[slot, :n_ctx, :] = p_ctx.astype(BF16)
        p_scr[slot, n_ctx:, :] = p_loc.astype(BF16)
        r_scr[slot] = jnp.broadcast_to(1.0 / denom, r_scr.shape[1:])

    def values(item, slot):
        hp, _, q_off, k_off = locate(item)
        pv = (jnp.dot(vc_ref[0, hp], p_scr[slot, :n_ctx, :], preferred_element_type=F32)
              + jnp.dot(v_ref[0, hp, :, pl.ds(k_off, slab_keys)], p_scr[slot, n_ctx:, :],
                        preferred_element_type=F32))
        pv = pv * r_scr[slot, 0:1, :]
        out = jnp.where(first_head, pv[:, :pair_tokens], pv[:, pair_tokens:])
        o_scr[hp, pl.ds(q_off, pair_tokens), :] = out.T

    scores(0, 0)
    scores(1, 1)
    softmax(0, 0)

    def trip(t, carry):
        for i in range(PIPE_UNROLL):
            item = PIPE_UNROLL * t + i
            scores(item + 2, (i + 2) % PIPE_SLOTS)
            softmax(item + 1, (i + 1) % PIPE_SLOTS)
            values(item, i % PIPE_SLOTS)
        return carry

    assert (n_items - 2) % PIPE_UNROLL == 0 and PIPE_UNROLL % PIPE_SLOTS == 0
    lax.fori_loop(0, (n_items - 2) // PIPE_UNROLL, trip, 0)
    softmax(n_items - 1, (n_items - 1) % PIPE_SLOTS)
    values(n_items - 2, (n_items - 2) % PIPE_SLOTS)
    values(n_items - 1, (n_items - 1) % PIPE_SLOTS)

    gate = mod_ref[0, 0, 2:3, :]
    o = jnp.concatenate([o_scr[hp] for hp in range(N_HEAD_PAIRS)], axis=1)
    y = (o * _silu(g_ref[0])).astype(BF16)
    y = jnp.dot(y, wout_ref[0], preferred_element_type=F32)
    o_ref[0] = x_ref[0] + gate * y


def _attention_layer(x, q, k, v, kc, vc, g, mods, layer, rpb, w_out, j):
    b, seq_len, d = x.shape
    n_ctx = kc.shape[2]
    n_rows = seq_len // GRID_W
    assert n_rows % ATT_ROWS == 0 and n_rows >= SLAB_ROWS and ATT_ROWS % 2 == 0
    tile = ATT_ROWS * GRID_W
    n_keys = n_ctx + SLAB_ROWS * GRID_W
    tiles, ids = _bias_tile_plan(n_rows)
    bias = _attention_bias(rpb, tiles)
    tile_spec = pl.BlockSpec((1, tile, d), lambda bi, i: (bi, i, 0))
    q_spec = pl.BlockSpec((1, N_HEAD_PAIRS, LANES, tile), lambda bi, i: (bi, 0, 0, i))
    seq_spec = pl.BlockSpec((1, N_HEAD_PAIRS, seq_len, LANES), lambda bi, i: (bi, 0, 0, 0))
    ctx_spec = pl.BlockSpec((1, N_HEAD_PAIRS, n_ctx, LANES), lambda bi, i: (bi, 0, 0, 0))
    seq_t_spec = pl.BlockSpec((1, N_HEAD_PAIRS, LANES, seq_len), lambda bi, i: (bi, 0, 0, 0))
    ctx_t_spec = pl.BlockSpec((1, N_HEAD_PAIRS, LANES, n_ctx), lambda bi, i: (bi, 0, 0, 0))
    return pl.pallas_call(
        functools.partial(_attention_kernel, n_rows=n_rows),
        grid=(b, n_rows // ATT_ROWS),
        in_specs=[pl.BlockSpec(memory_space=pltpu.SMEM),
                  q_spec, seq_spec, seq_t_spec, ctx_spec, ctx_t_spec, tile_spec, tile_spec,
                  _mod_block(mods, layer, None),
                  pl.BlockSpec(bias.shape, lambda bi, i: (0, 0, 0, 0),
                               pipeline_mode=pl.Buffered(1)),
                  _layer_block(w_out, j)],
        out_specs=tile_spec,
        out_shape=jax.ShapeDtypeStruct(x.shape, F32),
        scratch_shapes=[pltpu.VMEM((PIPE_SLOTS, n_keys, 4 * GRID_W), F32),
                        pltpu.VMEM((PIPE_SLOTS, n_keys, 4 * GRID_W), BF16),
                        pltpu.VMEM((PIPE_SLOTS, 8, 4 * GRID_W), F32),
                        pltpu.VMEM((N_HEAD_PAIRS, tile, LANES), F32)],
        compiler_params=_params(2),
        name="na_attention",
    )(jnp.asarray(ids), q, k, v, kc, vc, g, x, mods, bias, w_out)


def kernel(x, c, ctx, c_ctx, norm_g, ada_w, ada_b, pool_w_in, pool_w_grp, pool_scale, pool_w_out,
           na_w_in, na_rpb, na_w_out, conv_w_in, conv_dw, conv_db, conv_w_out, final_g):
    depth = norm_g.shape[0]
    batch, _, d = x.shape
    assert batch < COND_ROWS and WIN_ROWS // 2 <= HALO

    cond = jnp.zeros((COND_ROWS, d), F32).at[:batch].set(c).at[batch].set(c_ctx)
    mods = _modulation(cond, ada_w, ada_b).reshape(depth, COND_ROWS, 3, d)
    norm_g = norm_g.reshape(depth, 1, d)

    pool_weights = (pool_w_in.astype(BF16), pool_w_grp.astype(BF16),
                    pool_scale.reshape(-1, 1, WIDTH), pool_w_out.astype(BF16))
    conv_weights = (conv_w_in.astype(BF16), conv_dw, conv_db.reshape(-1, 1, WIDTH),
                    conv_w_out.astype(BF16))
    na_w_in, na_w_out = na_w_in.astype(BF16), na_w_out.astype(BF16)

    last_ctx_reader = max([i for i in range(depth) if i % N_MIXERS == 1], default=-1)
    for i in range(depth):
        kind, j = i % N_MIXERS, i // N_MIXERS
        update_ctx = i < last_ctx_reader
        fg = final_g if i == depth - 1 else None
        if kind != 1:
            kernel_fn, weights = ((_pool_kernel, pool_weights) if kind == 0
                                  else (_conv_kernel, conv_weights))
            run = functools.partial(_mixer_layer, kernel_fn, mods=mods, layer=i, norm_g=norm_g,
                                    weights=weights, j=j)
            if update_ctx:
                ctx = run(x=ctx, mod_row=batch, final_g=None, tile=CTX_TILE, name=f"ctx_layer{i}")
            x = run(x=x, mod_row=None, final_g=fg, tile=X_TILE, name=f"layer{i}")
            continue

        if update_ctx:
            raise NotImplementedError("context output of a neighbourhood-attention layer")
        if fg is not None:
            raise NotImplementedError("final norm after a neighbourhood-attention layer")
        q, k, v, g = _project(x, mods, i, None, norm_g, na_w_in, j, (0, 1, 2, 3),
                              ("pairs_t", "pairs", "pairs_t", "flat"),
                              (HEAD_DIM ** -0.5 * LOG2E, 1.0, 1.0, 1.0), X_TILE, f"na_project{i}")
        kc, vc = _project(ctx, mods, i, batch, norm_g, na_w_in, j, (1, 2), ("pairs", "pairs_t"),
                          (1.0, 1.0), CTX_TILE, f"na_ctx_project{i}")
        x = _attention_layer(x, q, k, v, kc, vc, g, mods, i, na_rpb[j], na_w_out, j)
    return x
```

```python
import functools
import math

import numpy as np

import jax
import jax.numpy as jnp
from jax import lax
from jax.experimental import pallas as pl
from jax.experimental.pallas import tpu as pltpu

D_MODEL = 1024
WIDTH = D_MODEL
GRID_W = 64
N_MIXERS = 3
POOL_WINDOWS = (2, 4, 8, 16)
POOL_GROUP = WIDTH // len(POOL_WINDOWS)
HEAD_DIM = 64
N_HEADS = WIDTH // HEAD_DIM
WIN_ROWS = 8
WIN_COLS = 16
EPS = 1e-6

HALO = 8
LANES = 128
N_HEAD_PAIRS = WIDTH // LANES
MASK_VALUE = -1e30
LOG2E = math.log2(math.e)
VMEM_LIMIT = 56 * 1024 * 1024

X_TILE = 1024
CTX_TILE = 256
ATT_ROWS = 8
SLAB_ROWS = WIN_ROWS + 2
PIPE_SLOTS = 2
PIPE_UNROLL = 10
BIAS_HEADS_PER_STEP = 8
CONV_BLOCK = 256
COND_ROWS = 16

F32 = jnp.float32
BF16 = jnp.bfloat16


def _silu(x):
    return x / (1.0 + jnp.exp(-x))


def _mod_norm(x, norm_g, shift, scale):
    ms = jnp.mean(x * x, axis=-1, keepdims=True)
    return (x * lax.rsqrt(ms + EPS)) * (norm_g * (1.0 + scale)) + shift


def _params(n_axes):
    return pltpu.CompilerParams(
        dimension_semantics=("arbitrary",) * n_axes, vmem_limit_bytes=VMEM_LIMIT)


def _layer_block(arr, j, col=None):
    shape = (1,) + arr.shape[1:]
    index = (j,) + (0,) * (arr.ndim - 1)
    if col is not None:
        shape = shape[:-1] + (WIDTH,)
        index = index[:-1] + (col,)
    return pl.BlockSpec(shape, lambda *_: index, pipeline_mode=pl.Buffered(1))


def _mod_block(mods, layer, row):
    d = mods.shape[-1]
    if row is None:
        return pl.BlockSpec((1, 1, 3, d), lambda bi, i: (layer, bi, 0, 0))
    return pl.BlockSpec((1, 1, 3, d), lambda bi, i: (layer, row, 0, 0))


def _modulation_kernel(cond_ref, w_ref, b_ref, o_ref):
    s = _silu(cond_ref[...]).astype(BF16)
    o_ref[0] = jnp.dot(s, w_ref[0].astype(BF16), preferred_element_type=F32) + b_ref[0]


def _modulation(cond, ada_w, ada_b):
    depth, d, n = ada_w.shape
    rows = cond.shape[0]
    tn = 1024
    return pl.pallas_call(
        _modulation_kernel,
        grid=(depth, n // tn),
        in_specs=[
            pl.BlockSpec((rows, d), lambda i, j: (0, 0)),
            pl.BlockSpec((1, d, tn), lambda i, j: (i, 0, j)),
            pl.BlockSpec((1, 1, tn), lambda i, j: (i, 0, j)),
        ],
        out_specs=pl.BlockSpec((1, rows, tn), lambda i, j: (i, 0, j)),
        out_shape=jax.ShapeDtypeStruct((depth, rows, n), F32),
        compiler_params=_params(2),
        name="modulation",
    )(cond, ada_w, ada_b.reshape(depth, 1, n))


def _normed_rows(x_ref, xp_ref, xn_ref, mod_ref, ng_ref):
    shift, scale = mod_ref[0, 0, 0:1, :], mod_ref[0, 0, 1:2, :]
    xe = jnp.concatenate([xp_ref[0], x_ref[0], xn_ref[0]], axis=0)
    he = _mod_norm(xe, ng_ref[0], shift, scale)
    return he.astype(BF16), he[HALO:he.shape[0] - HALO].astype(BF16)


def _zero_outside_sequence(e, tile):
    i = pl.program_id(1)
    keep_prev = (i > 0).astype(F32)
    keep_next = (i < pl.num_programs(1) - 1).astype(F32)
    return jnp.concatenate(
        [e[:HALO] * keep_prev, e[HALO:HALO + tile], e[HALO + tile:] * keep_next], axis=0)


def _window_sum(e, width, tile):
    n = e.shape[0]
    half = width // 2
    f = e
    k = 1
    while k < half:
        f = f + pltpu.roll(f, n - k, axis=0)
        k *= 2
    if half == HALO:
        return f[:tile] + f[HALO:HALO + tile]
    return (pltpu.roll(f, half, axis=0) + f)[HALO:HALO + tile]


def _window_mean(ws, width, tile, seq_len):
    half = width // 2
    row = lax.broadcasted_iota(jnp.int32, (HALO, 1), 0)

    def inv_count(first_row):
        t = row + (pl.program_id(1) * tile + first_row)
        cnt = jnp.minimum(t + half, seq_len) - jnp.maximum(t - half, 0)
        return 1.0 / cnt.astype(F32)

    return jnp.concatenate([ws[:HALO] * inv_count(0),
                            ws[HALO:tile - HALO] * (1.0 / width),
                            ws[tile - HALO:] * inv_count(tile - HALO)], axis=0)


def _finish(x, y, gate, o_ref, fg_ref):
    out = x + gate * y
    if fg_ref is not None:
        ms = jnp.mean(out * out, axis=-1, keepdims=True)
        out = (out * lax.rsqrt(ms + EPS)) * fg_ref[...]
    o_ref[0] = out


def _fold_pool_kernel(wu_in_ref, wg_in_ref, wgrp_ref, wu_ref, wg_ref):
    folded = jnp.dot(wu_in_ref[0], wgrp_ref[0, 0],
                     preferred_element_type=F32, precision=lax.Precision.HIGHEST)
    wu_ref[0] = folded.astype(BF16)
    wg_ref[0] = wg_in_ref[0].astype(BF16)


def _fold_pool_weights(w_in, w_grp):
    n, d, _ = w_in.shape
    groups = len(POOL_WINDOWS)
    out = jax.ShapeDtypeStruct((n, d, WIDTH), BF16)
    return pl.pallas_call(
        _fold_pool_kernel,
        grid=(n, groups),
        in_specs=[pl.BlockSpec((1, d, POOL_GROUP), lambda i, g: (i, 0, g)),
                  pl.BlockSpec((1, d, POOL_GROUP), lambda i, g: (i, 0, groups + g)),
                  pl.BlockSpec((1, 1, POOL_GROUP, POOL_GROUP), lambda i, g: (i, g, 0, 0))],
        out_specs=[pl.BlockSpec((1, d, POOL_GROUP), lambda i, g: (i, 0, g))] * 2,
        out_shape=[out, out],
        compiler_params=_params(2),
        name="fold_pool_weights",
    )(w_in, w_in, w_grp)


def _pool_kernel(x_ref, xp_ref, xn_ref, mod_ref, ng_ref, wu_ref, wg_ref, ps_ref, wout_ref,
                 *rest, tile, seq_len):
    fg_ref, o_ref = rest if len(rest) == 2 else (None, rest[0])
    he, hm = _normed_rows(x_ref, xp_ref, xn_ref, mod_ref, ng_ref)
    t = jnp.dot(he, wu_ref[0], preferred_element_type=F32)
    t = _zero_outside_sequence(t, tile)
    g = jnp.dot(hm, wg_ref[0], preferred_element_type=F32)
    mixed = []
    for gi, width in enumerate(POOL_WINDOWS):
        te = t[:, gi * POOL_GROUP:(gi + 1) * POOL_GROUP]
        pooled = _window_mean(_window_sum(te, width, tile), width, tile, seq_len)
        mixed.append(pooled - te[HALO:HALO + tile])
    mixed = jnp.concatenate(mixed, axis=1)
    z = (mixed * ps_ref[0] * _silu(g)).astype(BF16)
    y = jnp.dot(z, wout_ref[0], preferred_element_type=F32)
    _finish(x_ref[0], y, mod_ref[0, 0, 2:3, :], o_ref, fg_ref)


def _conv_kernel(x_ref, xp_ref, xn_ref, mod_ref, ng_ref, win_ref, dw_ref, db_ref, wout_ref,
                 *rest, tile, seq_len):
    fg_ref, o_ref = rest if len(rest) == 2 else (None, rest[0])
    he, hm = _normed_rows(x_ref, xp_ref, xn_ref, mod_ref, ng_ref)
    n = tile + 2 * HALO
    rows = slice(HALO, HALO + tile)
    y = None
    for ci in range(WIDTH // CONV_BLOCK):
        cols = slice(ci * CONV_BLOCK, (ci + 1) * CONV_BLOCK)

        def proj(h, part):
            w = win_ref[0, :, part * WIDTH + ci * CONV_BLOCK:part * WIDTH + (ci + 1) * CONV_BLOCK]
            return jnp.dot(h, w, preferred_element_type=F32)

        z = _zero_outside_sequence(proj(he, 1) * proj(he, 2), tile)
        conv = (dw_ref[0, 0:1, cols] * pltpu.roll(z, 1, axis=0)[rows]
                + dw_ref[0, 1:2, cols] * z[rows]
                + dw_ref[0, 2:3, cols] * pltpu.roll(z, n - 1, axis=0)[rows]
                + db_ref[0, :, cols])
        yc = (proj(hm, 0) * conv * _silu(proj(hm, 3))).astype(BF16)
        part = jnp.dot(yc, wout_ref[0, cols, :], preferred_element_type=F32)
        y = part if y is None else y + part
    _finish(x_ref[0], y, mod_ref[0, 0, 2:3, :], o_ref, fg_ref)


def _mixer_layer(kernel_fn, x, mods, layer, mod_row, norm_g, weights, j, final_g, tile, name):
    b, seq_len, d = x.shape
    tile = min(tile, seq_len)
    halo_blocks = seq_len // HALO
    per_tile = tile // HALO
    in_specs = [
        pl.BlockSpec((1, tile, d), lambda bi, i: (bi, i, 0)),
        pl.BlockSpec((1, HALO, d), lambda bi, i: (bi, jnp.maximum(i * per_tile - 1, 0), 0)),
        pl.BlockSpec((1, HALO, d),
                     lambda bi, i: (bi, jnp.minimum((i + 1) * per_tile, halo_blocks - 1), 0)),
        _mod_block(mods, layer, mod_row),
        _layer_block(norm_g, layer),
    ] + [_layer_block(w, j) for w in weights]
    args = [x, x, x, mods, norm_g] + list(weights)
    if final_g is not None:
        in_specs.append(pl.BlockSpec((1, d), lambda bi, i: (0, 0)))
        args.append(final_g.reshape(1, d))
    return pl.pallas_call(
        functools.partial(kernel_fn, tile=tile, seq_len=seq_len),
        grid=(b, seq_len // tile),
        in_specs=in_specs,
        out_specs=pl.BlockSpec((1, tile, d), lambda bi, i: (bi, i, 0)),
        out_shape=jax.ShapeDtypeStruct(x.shape, F32),
        compiler_params=_params(2),
        name=name,
    )(*args)


def _project_kernel(x_ref, mod_ref, ng_ref, *refs, scales, layouts):
    n = len(scales)
    w_refs, o_refs = refs[:n], refs[n:]
    shift, scale = mod_ref[0, 0, 0:1, :], mod_ref[0, 0, 1:2, :]
    h = _mod_norm(x_ref[0], ng_ref[0], shift, scale).astype(BF16)
    for w_ref, o_ref, s, layout in zip(w_refs, o_refs, scales, layouts):
        p = jnp.dot(h, w_ref[0], preferred_element_type=F32)
        if s != 1.0:
            p = p * s
        if layout == "flat":
            o_ref[0] = p.astype(o_ref.dtype)
            continue
        for hp in range(N_HEAD_PAIRS):
            block = p[:, hp * LANES:(hp + 1) * LANES]
            o_ref[0, hp] = (block.T if layout == "pairs_t" else block).astype(o_ref.dtype)


def _project(x, mods, layer, mod_row, norm_g, w, j, cols, layouts, scales, tile, name):
    b, seq_len, d = x.shape
    tile = min(tile, seq_len)
    assert tile % LANES == 0
    specs = {
        "flat": (pl.BlockSpec((1, tile, WIDTH), lambda bi, i: (bi, i, 0)),
                 jax.ShapeDtypeStruct((b, seq_len, WIDTH), F32)),
        "pairs": (pl.BlockSpec((1, N_HEAD_PAIRS, tile, LANES), lambda bi, i: (bi, 0, i, 0)),
                  jax.ShapeDtypeStruct((b, N_HEAD_PAIRS, seq_len, LANES), BF16)),
        "pairs_t": (pl.BlockSpec((1, N_HEAD_PAIRS, LANES, tile), lambda bi, i: (bi, 0, 0, i)),
                    jax.ShapeDtypeStruct((b, N_HEAD_PAIRS, LANES, seq_len), BF16)),
    }
    out_specs = [specs[name_][0] for name_ in layouts]
    out_shape = [specs[name_][1] for name_ in layouts]
    return pl.pallas_call(
        functools.partial(_project_kernel, scales=scales, layouts=layouts),
        grid=(b, seq_len // tile),
        in_specs=[
            pl.BlockSpec((1, tile, d), lambda bi, i: (bi, i, 0)),
            _mod_block(mods, layer, mod_row),
            _layer_block(norm_g, layer),
        ] + [_layer_block(w, j, col) for col in cols],
        out_specs=out_specs,
        out_shape=out_shape,
        compiler_params=_params(2),
        name=name,
    )(x, mods, norm_g, *([w] * len(cols)))


SLAB_TILES = SLAB_ROWS // 2


def _slab_base(pair_row0, n_rows):
    return jnp.clip(pair_row0 - WIN_ROWS // 2, 0, n_rows - SLAB_ROWS)


def _bias_tile_plan(n_rows):
    tiles, ids = [], []
    for r0 in range(0, n_rows, 2):
        base = int(np.clip(r0 - WIN_ROWS // 2, 0, n_rows - SLAB_ROWS))
        for j in range(SLAB_TILES):
            quad = []
            for key_row in (base + 2 * j, base + 2 * j + 1):
                for r in (r0, r0 + 1):
                    start = int(np.clip(r - WIN_ROWS // 2, 0, n_rows - WIN_ROWS))
                    assert base <= start and start + WIN_ROWS <= base + SLAB_ROWS
                    inside = start <= key_row < start + WIN_ROWS
                    quad.append(key_row - r + WIN_ROWS - 1 if inside else None)
            quad = tuple(quad)
            if quad not in tiles:
                tiles.append(quad)
            ids.append(tiles.index(quad))
    return tiles, np.asarray(ids, np.int32)


def _bias_tiles_kernel(w_ref, o_ref, *, tiles):
    k_col = lax.broadcasted_iota(jnp.int32, (GRID_W, LANES), 0)
    lane = lax.broadcasted_iota(jnp.int32, (GRID_W, LANES), 1)
    q_col = lane % GRID_W
    c_start = jnp.clip(q_col - WIN_COLS // 2, 0, GRID_W - WIN_COLS)
    inside = (k_col >= c_start) & (k_col < c_start + WIN_COLS)
    first_query_row = lane < GRID_W
    masked = jnp.full((GRID_W, LANES), MASK_VALUE, F32)

    def toeplitz(head, d, shift):
        if d is None:
            return masked
        row = jnp.broadcast_to(w_ref[head, d:d + 1, :], (GRID_W, LANES))
        return pltpu.roll(row, shift, axis=1, stride=1, stride_axis=0) * LOG2E

    for head in range(w_ref.shape[0]):
        for tile_id, quad in enumerate(tiles):
            halves = [jnp.where(inside,
                                jnp.where(first_query_row, toeplitz(head, quad[2 * kr], 0),
                                          toeplitz(head, quad[2 * kr + 1], GRID_W)),
                                masked) for kr in range(2)]
            o_ref[head, tile_id] = jnp.concatenate(halves, axis=0)


def _attention_bias(rpb, tiles):
    h, n_dr, n_dc = rpb.shape
    assert n_dr == 2 * WIN_ROWS - 1 and n_dc == 2 * WIN_COLS - 1
    rev = rpb[..., ::-1]
    w = jnp.concatenate([rev[..., WIN_COLS - 1:], jnp.zeros((h, n_dr, LANES - n_dc), F32),
                         rev[..., :WIN_COLS - 1]], axis=-1)
    w = jnp.pad(w, ((0, 0), (0, 2 * WIN_ROWS - n_dr), (0, 0)))
    return pl.pallas_call(
        functools.partial(_bias_tiles_kernel, tiles=tiles),
        grid=(h // BIAS_HEADS_PER_STEP,),
        in_specs=[pl.BlockSpec((BIAS_HEADS_PER_STEP, 2 * WIN_ROWS, LANES), lambda i: (i, 0, 0))],
        out_specs=pl.BlockSpec((BIAS_HEADS_PER_STEP, len(tiles), LANES, LANES),
                               lambda i: (i, 0, 0, 0)),
        out_shape=jax.ShapeDtypeStruct((h, len(tiles), LANES, LANES), F32),
        compiler_params=_params(1),
        name="bias_tiles",
    )(w)


def _attention_kernel(ids_ref, q_ref, k_ref, v_ref, kc_ref, vc_ref, g_ref, x_ref, mod_ref,
                      bias_ref, wout_ref, o_ref, s_scr, p_scr, r_scr, o_scr, *, n_rows):
    pair_tokens = 2 * GRID_W
    n_ctx = kc_ref.shape[2]
    slab_keys = SLAB_ROWS * GRID_W
    n_items = (ATT_ROWS // 2) * N_HEAD_PAIRS
    first_head = lax.broadcasted_iota(jnp.int32, (LANES, pair_tokens), 0) < HEAD_DIM

    def locate(item):
        item = jnp.asarray(item, jnp.int32)
        pb, hp = item // N_HEAD_PAIRS, item % N_HEAD_PAIRS
        pair = pl.program_id(1) * (ATT_ROWS // 2) + pb
        q_off = pl.multiple_of(pb * pair_tokens, pair_tokens)
        k_off = pl.multiple_of(_slab_base(2 * pair, n_rows) * GRID_W, pair_tokens)
        return hp, pair, q_off, k_off

    def scores(item, slot):
        hp, _, q_off, k_off = locate(item)
        qt = q_ref[0, hp, :, pl.ds(q_off, pair_tokens)]
        zero = jnp.zeros_like(qt)
        q2 = jnp.concatenate([jnp.where(first_head, qt, zero),
                              jnp.where(first_head, zero, qt)], axis=1)
        s_scr[slot, :n_ctx, :] = jnp.dot(kc_ref[0, hp], q2, preferred_element_type=F32)
        s_scr[slot, n_ctx:, :] = jnp.dot(k_ref[0, hp, pl.ds(k_off, slab_keys), :], q2,
                                         preferred_element_type=F32)

    def softmax(item, slot):
        hp, pair, _, _ = locate(item)
        bias = jnp.concatenate(
            [jnp.concatenate([bias_ref[2 * hp + a, ids_ref[pair * SLAB_TILES + j]]
                              for a in range(2)], axis=1)
             for j in range(SLAB_TILES)], axis=0)
        s_ctx = s_scr[slot, :n_ctx, :]
        s_loc = s_scr[slot, n_ctx:, :] + bias
        m = jnp.maximum(jnp.max(s_loc, axis=0, keepdims=True),
                        jnp.max(s_ctx, axis=0, keepdims=True))
        p_ctx = jnp.exp2(s_ctx - m)
        p_loc = jnp.exp2(s_loc - m)
        denom = jnp.sum(p_loc, axis=0, keepdims=True) + jnp.sum(p_ctx, axis=0, keepdims=True)
        p_scr[slot, :n_ctx, :] = p_ctx.astype(BF16)
        p_scr[slot, n_ctx:, :] = p_loc.astype(BF16)
        r_scr[slot] = jnp.broadcast_to(1.0 / denom, r_scr.shape[1:])

    def values(item, slot):
        hp, _, q_off, k_off = locate(item)
        pv = (jnp.dot(vc_ref[0, hp], p_scr[slot, :n_ctx, :], preferred_element_type=F32)
              + jnp.dot(v_ref[0, hp, :, pl.ds(k_off, slab_keys)], p_scr[slot, n_ctx:, :],
                        preferred_element_type=F32))
        pv = pv * r_scr[slot, 0:1, :]
        out = jnp.where(first_head, pv[:, :pair_tokens], pv[:, pair_tokens:])
        o_scr[hp, pl.ds(q_off, pair_tokens), :] = out.T

    scores(0, 0)
    scores(1, 1)
    softmax(0, 0)

    def trip(t, carry):
        for i in range(PIPE_UNROLL):
            item = PIPE_UNROLL * t + i
            scores(item + 2, (i + 2) % PIPE_SLOTS)
            softmax(item + 1, (i + 1) % PIPE_SLOTS)
            values(item, i % PIPE_SLOTS)
        return carry

    assert (n_items - 2) % PIPE_UNROLL == 0 and PIPE_UNROLL % PIPE_SLOTS == 0
    lax.fori_loop(0, (n_items - 2) // PIPE_UNROLL, trip, 0)
    softmax(n_items - 1, (n_items - 1) % PIPE_SLOTS)
    values(n_items - 2, (n_items - 2) % PIPE_SLOTS)
    values(n_items - 1, (n_items - 1) % PIPE_SLOTS)

    gate = mod_ref[0, 0, 2:3, :]
    o = jnp.concatenate([o_scr[hp] for hp in range(N_HEAD_PAIRS)], axis=1)
    y = (o * _silu(g_ref[0])).astype(BF16)
    y = jnp.dot(y, wout_ref[0], preferred_element_type=F32)
    o_ref[0] = x_ref[0] + gate * y


def _attention_layer(x, q, k, v, kc, vc, g, mods, layer, rpb, w_out, j):
    b, seq_len, d = x.shape
    n_ctx = kc.shape[2]
    n_rows = seq_len // GRID_W
    assert n_rows % ATT_ROWS == 0 and n_rows >= SLAB_ROWS and ATT_ROWS % 2 == 0
    tile = ATT_ROWS * GRID_W
    n_keys = n_ctx + SLAB_ROWS * GRID_W
    tiles, ids = _bias_tile_plan(n_rows)
    bias = _attention_bias(rpb, tiles)
    tile_spec = pl.BlockSpec((1, tile, d), lambda bi, i: (bi, i, 0))
    q_spec = pl.BlockSpec((1, N_HEAD_PAIRS, LANES, tile), lambda bi, i: (bi, 0, 0, i))
    seq_spec = pl.BlockSpec((1, N_HEAD_PAIRS, seq_len, LANES), lambda bi, i: (bi, 0, 0, 0))
    ctx_spec = pl.BlockSpec((1, N_HEAD_PAIRS, n_ctx, LANES), lambda bi, i: (bi, 0, 0, 0))
    seq_t_spec = pl.BlockSpec((1, N_HEAD_PAIRS, LANES, seq_len), lambda bi, i: (bi, 0, 0, 0))
    ctx_t_spec = pl.BlockSpec((1, N_HEAD_PAIRS, LANES, n_ctx), lambda bi, i: (bi, 0, 0, 0))
    return pl.pallas_call(
        functools.partial(_attention_kernel, n_rows=n_rows),
        grid=(b, n_rows // ATT_ROWS),
        in_specs=[pl.BlockSpec(memory_space=pltpu.SMEM),
                  q_spec, seq_spec, seq_t_spec, ctx_spec, ctx_t_spec, tile_spec, tile_spec,
                  _mod_block(mods, layer, None),
                  pl.BlockSpec(bias.shape, lambda bi, i: (0, 0, 0, 0),
                               pipeline_mode=pl.Buffered(1)),
                  _layer_block(w_out, j)],
        out_specs=tile_spec,
        out_shape=jax.ShapeDtypeStruct(x.shape, F32),
        scratch_shapes=[pltpu.VMEM((PIPE_SLOTS, n_keys, 4 * GRID_W), F32),
                        pltpu.VMEM((PIPE_SLOTS, n_keys, 4 * GRID_W), BF16),
                        pltpu.VMEM((PIPE_SLOTS, 8, 4 * GRID_W), F32),
                        pltpu.VMEM((N_HEAD_PAIRS, tile, LANES), F32)],
        compiler_params=_params(2),
        name="na_attention",
    )(jnp.asarray(ids), q, k, v, kc, vc, g, x, mods, bias, w_out)


def kernel(x, c, ctx, c_ctx, norm_g, ada_w, ada_b, pool_w_in, pool_w_grp, pool_scale, pool_w_out,
           na_w_in, na_rpb, na_w_out, conv_w_in, conv_dw, conv_db, conv_w_out, final_g):
    depth = norm_g.shape[0]
    batch, _, d = x.shape
    assert batch < COND_ROWS and WIN_ROWS // 2 <= HALO

    cond = jnp.zeros((COND_ROWS, d), F32).at[:batch].set(c).at[batch].set(c_ctx)
    mods = _modulation(cond, ada_w, ada_b).reshape(depth, COND_ROWS, 3, d)
    norm_g = norm_g.reshape(depth, 1, d)

    pool_weights = (*_fold_pool_weights(pool_w_in, pool_w_grp),
                    pool_scale.reshape(-1, 1, WIDTH), pool_w_out.astype(BF16))
    conv_weights = (conv_w_in.astype(BF16), conv_dw, conv_db.reshape(-1, 1, WIDTH),
                    conv_w_out.astype(BF16))
    na_w_in, na_w_out = na_w_in.astype(BF16), na_w_out.astype(BF16)

    last_ctx_reader = max([i for i in range(depth) if i % N_MIXERS == 1], default=-1)
    for i in range(depth):
        kind, j = i % N_MIXERS, i // N_MIXERS
        update_ctx = i < last_ctx_reader
        fg = final_g if i == depth - 1 else None
        if kind != 1:
            kernel_fn, weights = ((_pool_kernel, pool_weights) if kind == 0
                                  else (_conv_kernel, conv_weights))
            run = functools.partial(_mixer_layer, kernel_fn, mods=mods, layer=i, norm_g=norm_g,
                                    weights=weights, j=j)
            if update_ctx:
                ctx = run(x=ctx, mod_row=batch, final_g=None, tile=CTX_TILE, name=f"ctx_layer{i}")
            x = run(x=x, mod_row=None, final_g=fg, tile=X_TILE, name=f"layer{i}")
            continue

        if update_ctx:
            raise NotImplementedError("context output of a neighbourhood-attention layer")
        if fg is not None:
            raise NotImplementedError("final norm after a neighbourhood-attention layer")
        q, k, v, g = _project(x, mods, i, None, norm_g, na_w_in, j, (0, 1, 2, 3),
                              ("pairs_t", "pairs", "pairs_t", "flat"),
                              (HEAD_DIM ** -0.5 * LOG2E, 1.0, 1.0, 1.0), X_TILE, f"na_project{i}")
        kc, vc = _project(ctx, mods, i, batch, norm_g, na_w_in, j, (1, 2), ("pairs", "pairs_t"),
                          (1.0, 1.0), CTX_TILE, f"na_ctx_project{i}")
        x = _attention_layer(x, q, k, v, kc, vc, g, mods, i, na_rpb[j], na_w_out, j)
    return x
```

```python
import functools
import math

import numpy as np

import jax
import jax.numpy as jnp
from jax import lax
from jax.experimental import pallas as pl
from jax.experimental.pallas import tpu as pltpu

D_MODEL = 1024
WIDTH = D_MODEL
GRID_W = 64
N_MIXERS = 3
POOL_WINDOWS = (2, 4, 8, 16)
POOL_GROUP = WIDTH // len(POOL_WINDOWS)
HEAD_DIM = 64
N_HEADS = WIDTH // HEAD_DIM
WIN_ROWS = 8
WIN_COLS = 16
EPS = 1e-6

HALO = 8
LANES = 128
N_HEAD_PAIRS = WIDTH // LANES
MASK_VALUE = -1e30
LOG2E = math.log2(math.e)
VMEM_LIMIT = 56 * 1024 * 1024

X_TILE = 1024
CTX_TILE = 256
ATT_ROWS = 8
SLAB_ROWS = WIN_ROWS + 2
PIPE_SLOTS = 2
PIPE_UNROLL = 10
BIAS_HEADS_PER_STEP = 8
CONV_BLOCK = 256
COND_ROWS = 16

F32 = jnp.float32
BF16 = jnp.bfloat16


def _silu(x):
    return x / (1.0 + jnp.exp(-x))


def _mod_norm(x, norm_g, shift, scale):
    ms = jnp.mean(x * x, axis=-1, keepdims=True)
    return (x * lax.rsqrt(ms + EPS)) * (norm_g * (1.0 + scale)) + shift


def _params(n_axes):
    return pltpu.CompilerParams(
        dimension_semantics=("arbitrary",) * n_axes, vmem_limit_bytes=VMEM_LIMIT)


def _layer_block(arr, j, col=None):
    shape = (1,) + arr.shape[1:]
    index = (j,) + (0,) * (arr.ndim - 1)
    if col is not None:
        shape = shape[:-1] + (WIDTH,)
        index = index[:-1] + (col,)
    return pl.BlockSpec(shape, lambda *_: index, pipeline_mode=pl.Buffered(1))


def _mod_block(mods, layer, row):
    d = mods.shape[-1]
    if row is None:
        return pl.BlockSpec((1, 1, 3, d), lambda bi, i: (layer, bi, 0, 0))
    return pl.BlockSpec((1, 1, 3, d), lambda bi, i: (layer, row, 0, 0))


def _modulation_kernel(cond_ref, w_ref, b_ref, o_ref):
    s = _silu(cond_ref[...]).astype(BF16)
    o_ref[0] = jnp.dot(s, w_ref[0].astype(BF16), preferred_element_type=F32) + b_ref[0]


def _modulation(cond, ada_w, ada_b):
    depth, d, n = ada_w.shape
    rows = cond.shape[0]
    tn = n
    return pl.pallas_call(
        _modulation_kernel,
        grid=(depth, n // tn),
        in_specs=[
            pl.BlockSpec((rows, d), lambda i, j: (0, 0)),
            pl.BlockSpec((1, d, tn), lambda i, j: (i, 0, j)),
            pl.BlockSpec((1, 1, tn), lambda i, j: (i, 0, j)),
        ],
        out_specs=pl.BlockSpec((1, rows, tn), lambda i, j: (i, 0, j)),
        out_shape=jax.ShapeDtypeStruct((depth, rows, n), F32),
        compiler_params=_params(2),
        name="modulation",
    )(cond, ada_w, ada_b.reshape(depth, 1, n))


def _normed_rows(x_ref, xp_ref, xn_ref, mod_ref, ng_ref):
    shift, scale = mod_ref[0, 0, 0:1, :], mod_ref[0, 0, 1:2, :]
    xe = jnp.concatenate([xp_ref[0], x_ref[0], xn_ref[0]], axis=0)
    he = _mod_norm(xe, ng_ref[0], shift, scale)
    return he.astype(BF16), he[HALO:he.shape[0] - HALO].astype(BF16)


def _zero_outside_sequence(e, tile):
    i = pl.program_id(1)
    keep_prev = (i > 0).astype(F32)
    keep_next = (i < pl.num_programs(1) - 1).astype(F32)
    return jnp.concatenate(
        [e[:HALO] * keep_prev, e[HALO:HALO + tile], e[HALO + tile:] * keep_next], axis=0)


def _window_sum(e, width, tile):
    n = e.shape[0]
    half = width // 2
    f = e
    k = 1
    while k < half:
        f = f + pltpu.roll(f, n - k, axis=0)
        k *= 2
    if half == HALO:
        return f[:tile] + f[HALO:HALO + tile]
    return (pltpu.roll(f, half, axis=0) + f)[HALO:HALO + tile]


def _window_mean(ws, width, tile, seq_len):
    half = width // 2
    row = lax.broadcasted_iota(jnp.int32, (HALO, 1), 0)

    def inv_count(first_row):
        t = row + (pl.program_id(1) * tile + first_row)
        cnt = jnp.minimum(t + half, seq_len) - jnp.maximum(t - half, 0)
        return 1.0 / cnt.astype(F32)

    return jnp.concatenate([ws[:HALO] * inv_count(0),
                            ws[HALO:tile - HALO] * (1.0 / width),
                            ws[tile - HALO:] * inv_count(tile - HALO)], axis=0)


def _finish(x, y, gate, o_ref, fg_ref):
    out = x + gate * y
    if fg_ref is not None:
        ms = jnp.mean(out * out, axis=-1, keepdims=True)
        out = (out * lax.rsqrt(ms + EPS)) * fg_ref[...]
    o_ref[0] = out


def _fold_pool_kernel(win_ref, wgrp_ref, ps_ref, wu_ref, wg_ref):
    for gi in range(len(POOL_WINDOWS)):
        cols = slice(gi * POOL_GROUP, (gi + 1) * POOL_GROUP)
        folded = jnp.dot(win_ref[0, :, cols], wgrp_ref[0, gi],
                         preferred_element_type=F32, precision=lax.Precision.HIGHEST)
        wu_ref[0, :, cols] = (folded * ps_ref[0, :, cols]).astype(BF16)
    wg_ref[0] = win_ref[0, :, WIDTH:].astype(BF16)


def _fold_pool_weights(w_in, w_grp, scale):
    n, d, _ = w_in.shape
    out = jax.ShapeDtypeStruct((n, d, WIDTH), BF16)
    return pl.pallas_call(
        _fold_pool_kernel,
        grid=(n,),
        in_specs=[pl.BlockSpec((1,) + w_in.shape[1:], lambda i: (i, 0, 0)),
                  pl.BlockSpec((1,) + w_grp.shape[1:], lambda i: (i, 0, 0, 0)),
                  pl.BlockSpec((1, 1, WIDTH), lambda i: (i, 0, 0))],
        out_specs=[pl.BlockSpec((1, d, WIDTH), lambda i: (i, 0, 0))] * 2,
        out_shape=[out, out],
        compiler_params=_params(1),
        name="fold_pool_weights",
    )(w_in, w_grp, scale.reshape(n, 1, WIDTH))


def _pool_kernel(x_ref, xp_ref, xn_ref, mod_ref, ng_ref, wu_ref, wg_ref, wout_ref,
                 *rest, tile, seq_len):
    fg_ref, o_ref = rest if len(rest) == 2 else (None, rest[0])
    he, hm = _normed_rows(x_ref, xp_ref, xn_ref, mod_ref, ng_ref)
    t = jnp.dot(he, wu_ref[0], preferred_element_type=F32)
    t = _zero_outside_sequence(t, tile)
    g = jnp.dot(hm, wg_ref[0], preferred_element_type=F32)
    mixed = []
    for gi, width in enumerate(POOL_WINDOWS):
        te = t[:, gi * POOL_GROUP:(gi + 1) * POOL_GROUP]
        pooled = _window_mean(_window_sum(te, width, tile), width, tile, seq_len)
        mixed.append(pooled - te[HALO:HALO + tile])
    mixed = jnp.concatenate(mixed, axis=1)
    z = (mixed * _silu(g)).astype(BF16)
    y = jnp.dot(z, wout_ref[0], preferred_element_type=F32)
    _finish(x_ref[0], y, mod_ref[0, 0, 2:3, :], o_ref, fg_ref)


def _conv_kernel(x_ref, xp_ref, xn_ref, mod_ref, ng_ref, win_ref, dw_ref, db_ref, wout_ref,
                 *rest, tile, seq_len):
    fg_ref, o_ref = rest if len(rest) == 2 else (None, rest[0])
    he, hm = _normed_rows(x_ref, xp_ref, xn_ref, mod_ref, ng_ref)
    n = tile + 2 * HALO
    rows = slice(HALO, HALO + tile)
    y = None
    for ci in range(WIDTH // CONV_BLOCK):
        cols = slice(ci * CONV_BLOCK, (ci + 1) * CONV_BLOCK)

        def proj(h, part):
            w = win_ref[0, :, part * WIDTH + ci * CONV_BLOCK:part * WIDTH + (ci + 1) * CONV_BLOCK]
            return jnp.dot(h, w, preferred_element_type=F32)

        z = _zero_outside_sequence(proj(he, 1) * proj(he, 2), tile)
        conv = (dw_ref[0, 0:1, cols] * pltpu.roll(z, 1, axis=0)[rows]
                + dw_ref[0, 1:2, cols] * z[rows]
                + dw_ref[0, 2:3, cols] * pltpu.roll(z, n - 1, axis=0)[rows]
                + db_ref[0, :, cols])
        yc = (proj(hm, 0) * conv * _silu(proj(hm, 3))).astype(BF16)
        part = jnp.dot(yc, wout_ref[0, cols, :], preferred_element_type=F32)
        y = part if y is None else y + part
    _finish(x_ref[0], y, mod_ref[0, 0, 2:3, :], o_ref, fg_ref)


def _mixer_layer(kernel_fn, x, mods, layer, mod_row, norm_g, weights, j, final_g, tile, name):
    b, seq_len, d = x.shape
    tile = min(tile, seq_len)
    halo_blocks = seq_len // HALO
    per_tile = tile // HALO
    in_specs = [
        pl.BlockSpec((1, tile, d), lambda bi, i: (bi, i, 0)),
        pl.BlockSpec((1, HALO, d), lambda bi, i: (bi, jnp.maximum(i * per_tile - 1, 0), 0)),
        pl.BlockSpec((1, HALO, d),
                     lambda bi, i: (bi, jnp.minimum((i + 1) * per_tile, halo_blocks - 1), 0)),
        _mod_block(mods, layer, mod_row),
        _layer_block(norm_g, layer),
    ] + [_layer_block(w, j) for w in weights]
    args = [x, x, x, mods, norm_g] + list(weights)
    if final_g is not None:
        in_specs.append(pl.BlockSpec((1, d), lambda bi, i: (0, 0)))
        args.append(final_g.reshape(1, d))
    return pl.pallas_call(
        functools.partial(kernel_fn, tile=tile, seq_len=seq_len),
        grid=(b, seq_len // tile),
        in_specs=in_specs,
        out_specs=pl.BlockSpec((1, tile, d), lambda bi, i: (bi, i, 0)),
        out_shape=jax.ShapeDtypeStruct(x.shape, F32),
        compiler_params=_params(2),
        name=name,
    )(*args)


def _project_kernel(x_ref, mod_ref, ng_ref, *refs, scales, layouts):
    n = len(scales)
    w_refs, o_refs = refs[:n], refs[n:]
    shift, scale = mod_ref[0, 0, 0:1, :], mod_ref[0, 0, 1:2, :]
    h = _mod_norm(x_ref[0], ng_ref[0], shift, scale).astype(BF16)
    for w_ref, o_ref, s, layout in zip(w_refs, o_refs, scales, layouts):
        p = jnp.dot(h, w_ref[0], preferred_element_type=F32)
        if s != 1.0:
            p = p * s
        if layout == "flat":
            o_ref[0] = p.astype(o_ref.dtype)
            continue
        for hp in range(N_HEAD_PAIRS):
            block = p[:, hp * LANES:(hp + 1) * LANES]
            o_ref[0, hp] = (block.T if layout == "pairs_t" else block).astype(o_ref.dtype)


def _project(x, mods, layer, mod_row, norm_g, w, j, cols, layouts, scales, tile, name):
    b, seq_len, d = x.shape
    tile = min(tile, seq_len)
    assert tile % LANES == 0
    specs = {
        "flat": (pl.BlockSpec((1, tile, WIDTH), lambda bi, i: (bi, i, 0)),
                 jax.ShapeDtypeStruct((b, seq_len, WIDTH), F32)),
        "pairs": (pl.BlockSpec((1, N_HEAD_PAIRS, tile, LANES), lambda bi, i: (bi, 0, i, 0)),
                  jax.ShapeDtypeStruct((b, N_HEAD_PAIRS, seq_len, LANES), BF16)),
        "pairs_t": (pl.BlockSpec((1, N_HEAD_PAIRS, LANES, tile), lambda bi, i: (bi, 0, 0, i)),
                    jax.ShapeDtypeStruct((b, N_HEAD_PAIRS, LANES, seq_len), BF16)),
    }
    out_specs = [specs[name_][0] for name_ in layouts]
    out_shape = [specs[name_][1] for name_ in layouts]
    return pl.pallas_call(
        functools.partial(_project_kernel, scales=scales, layouts=layouts),
        grid=(b, seq_len // tile),
        in_specs=[
            pl.BlockSpec((1, tile, d), lambda bi, i: (bi, i, 0)),
            _mod_block(mods, layer, mod_row),
            _layer_block(norm_g, layer),
        ] + [_layer_block(w, j, col) for col in cols],
        out_specs=out_specs,
        out_shape=out_shape,
        compiler_params=_params(2),
        name=name,
    )(x, mods, norm_g, *([w] * len(cols)))


SLAB_TILES = SLAB_ROWS // 2


def _slab_base(pair_row0, n_rows):
    return jnp.clip(pair_row0 - WIN_ROWS // 2, 0, n_rows - SLAB_ROWS)


def _bias_tile_plan(n_rows):
    tiles, ids = [], []
    for r0 in range(0, n_rows, 2):
        base = int(np.clip(r0 - WIN_ROWS // 2, 0, n_rows - SLAB_ROWS))
        for j in range(SLAB_TILES):
            quad = []
            for key_row in (base + 2 * j, base + 2 * j + 1):
                for r in (r0, r0 + 1):
                    start = int(np.clip(r - WIN_ROWS // 2, 0, n_rows - WIN_ROWS))
                    assert base <= start and start + WIN_ROWS <= base + SLAB_ROWS
                    inside = start <= key_row < start + WIN_ROWS
                    quad.append(key_row - r + WIN_ROWS - 1 if inside else None)
            quad = tuple(quad)
            if quad not in tiles:
                tiles.append(quad)
            ids.append(tiles.index(quad))
    return tiles, np.asarray(ids, np.int32)


def _bias_tiles_kernel(w_ref, o_ref, *, tiles):
    k_col = lax.broadcasted_iota(jnp.int32, (GRID_W, LANES), 0)
    lane = lax.broadcasted_iota(jnp.int32, (GRID_W, LANES), 1)
    q_col = lane % GRID_W
    c_start = jnp.clip(q_col - WIN_COLS // 2, 0, GRID_W - WIN_COLS)
    inside = (k_col >= c_start) & (k_col < c_start + WIN_COLS)
    first_query_row = lane < GRID_W
    masked = jnp.full((GRID_W, LANES), MASK_VALUE, F32)

    def toeplitz(head, d, shift):
        if d is None:
            return masked
        row = jnp.broadcast_to(w_ref[head, d:d + 1, :], (GRID_W, LANES))
        return pltpu.roll(row, shift, axis=1, stride=1, stride_axis=0) * LOG2E

    for head in range(w_ref.shape[0]):
        for tile_id, quad in enumerate(tiles):
            halves = [jnp.where(inside,
                                jnp.where(first_query_row, toeplitz(head, quad[2 * kr], 0),
                                          toeplitz(head, quad[2 * kr + 1], GRID_W)),
                                masked) for kr in range(2)]
            o_ref[head, tile_id] = jnp.concatenate(halves, axis=0)


def _attention_bias(rpb, tiles):
    h, n_dr, n_dc = rpb.shape
    assert n_dr == 2 * WIN_ROWS - 1 and n_dc == 2 * WIN_COLS - 1
    rev = rpb[..., ::-1]
    w = jnp.concatenate([rev[..., WIN_COLS - 1:], jnp.zeros((h, n_dr, LANES - n_dc), F32),
                         rev[..., :WIN_COLS - 1]], axis=-1)
    w = jnp.pad(w, ((0, 0), (0, 2 * WIN_ROWS - n_dr), (0, 0)))
    return pl.pallas_call(
        functools.partial(_bias_tiles_kernel, tiles=tiles),
        grid=(h // BIAS_HEADS_PER_STEP,),
        in_specs=[pl.BlockSpec((BIAS_HEADS_PER_STEP, 2 * WIN_ROWS, LANES), lambda i: (i, 0, 0))],
        out_specs=pl.BlockSpec((BIAS_HEADS_PER_STEP, len(tiles), LANES, LANES),
                               lambda i: (i, 0, 0, 0)),
        out_shape=jax.ShapeDtypeStruct((h, len(tiles), LANES, LANES), F32),
        compiler_params=_params(1),
        name="bias_tiles",
    )(w)


def _attention_kernel(ids_ref, q_ref, k_ref, v_ref, kc_ref, vc_ref, g_ref, x_ref, mod_ref,
                      bias_ref, wout_ref, o_ref, s_scr, p_scr, r_scr, o_scr, *, n_rows):
    pair_tokens = 2 * GRID_W
    n_ctx = kc_ref.shape[2]
    slab_keys = SLAB_ROWS * GRID_W
    n_items = (ATT_ROWS // 2) * N_HEAD_PAIRS
    first_head = lax.broadcasted_iota(jnp.int32, (LANES, pair_tokens), 0) < HEAD_DIM

    def locate(item):
        item = jnp.asarray(item, jnp.int32)
        pb, hp = item // N_HEAD_PAIRS, item % N_HEAD_PAIRS
        pair = pl.program_id(1) * (ATT_ROWS // 2) + pb
        q_off = pl.multiple_of(pb * pair_tokens, pair_tokens)
        k_off = pl.multiple_of(_slab_base(2 * pair, n_rows) * GRID_W, pair_tokens)
        return hp, pair, q_off, k_off

    def scores(item, slot):
        hp, _, q_off, k_off = locate(item)
        qt = q_ref[0, hp, :, pl.ds(q_off, pair_tokens)]
        zero = jnp.zeros_like(qt)
        q2 = jnp.concatenate([jnp.where(first_head, qt, zero),
                              jnp.where(first_head, zero, qt)], axis=1)
        s_scr[slot, :n_ctx, :] = jnp.dot(kc_ref[0, hp], q2, preferred_element_type=F32)
        s_scr[slot, n_ctx:, :] = jnp.dot(k_ref[0, hp, pl.ds(k_off, slab_keys), :], q2,
                                         preferred_element_type=F32)

    def softmax(item, slot):
        hp, pair, _, _ = locate(item)
        bias = jnp.concatenate(
            [jnp.concatenate([bias_ref[2 * hp + a, ids_ref[pair * SLAB_TILES + j]]
                              for a in range(2)], axis=1)
             for j in range(SLAB_TILES)], axis=0)
        s_ctx = s_scr[slot, :n_ctx, :]
        s_loc = s_scr[slot, n_ctx:, :] + bias
        m = jnp.maximum(jnp.max(s_loc, axis=0, keepdims=True),
                        jnp.max(s_ctx, axis=0, keepdims=True))
        p_ctx = jnp.exp2(s_ctx - m)
        p_loc = jnp.exp2(s_loc - m)
        denom = jnp.sum(p_loc, axis=0, keepdims=True) + jnp.sum(p_ctx, axis=0, keepdims=True)
        p_scr[slot, :n_ctx, :] = p_ctx.astype(BF16)
        p_scr[slot, n_ctx:, :] = p_loc.astype(BF16)
        r_scr[slot] = jnp.broadcast_to(1.0 / denom, r_scr.shape[1:])

    def values(item, slot):
        hp, _, q_off, k_off = locate(item)
        pv = (jnp.dot(vc_ref[0, hp], p_scr[slot, :n_ctx, :], preferred_element_type=F32)
              + jnp.dot(v_ref[0, hp, :, pl.ds(k_off, slab_keys)], p_scr[slot, n_ctx:, :],
                        preferred_element_type=F32))
        pv = pv * r_scr[slot, 0:1, :]
        out = jnp.where(first_head, pv[:, :pair_tokens], pv[:, pair_tokens:])
        o_scr[hp, pl.ds(q_off, pair_tokens), :] = out.T

    scores(0, 0)
    scores(1, 1)
    softmax(0, 0)

    def trip(t, carry):
        for i in range(PIPE_UNROLL):
            item = PIPE_UNROLL * t + i
            scores(item + 2, (i + 2) % PIPE_SLOTS)
            softmax(item + 1, (i + 1) % PIPE_SLOTS)
            values(item, i % PIPE_SLOTS)
        return carry

    assert (n_items - 2) % PIPE_UNROLL == 0 and PIPE_UNROLL % PIPE_SLOTS == 0
    lax.fori_loop(0, (n_items - 2) // PIPE_UNROLL, trip, 0)
    softmax(n_items - 1, (n_items - 1) % PIPE_SLOTS)
    values(n_items - 2, (n_items - 2) % PIPE_SLOTS)
    values(n_items - 1, (n_items - 1) % PIPE_SLOTS)

    gate = mod_ref[0, 0, 2:3, :]
    o = jnp.concatenate([o_scr[hp] for hp in range(N_HEAD_PAIRS)], axis=1)
    y = (o * _silu(g_ref[0])).astype(BF16)
    y = jnp.dot(y, wout_ref[0], preferred_element_type=F32)
    o_ref[0] = x_ref[0] + gate * y


def _attention_layer(x, q, k, v, kc, vc, g, mods, layer, rpb, w_out, j):
    b, seq_len, d = x.shape
    n_ctx = kc.shape[2] // b
    n_rows = seq_len // GRID_W
    assert n_rows % ATT_ROWS == 0 and n_rows >= SLAB_ROWS and ATT_ROWS % 2 == 0
    tile = ATT_ROWS * GRID_W
    n_keys = n_ctx + SLAB_ROWS * GRID_W
    tiles, ids = _bias_tile_plan(n_rows)
    bias = _attention_bias(rpb, tiles)
    tile_spec = pl.BlockSpec((1, tile, d), lambda bi, i: (bi, i, 0))
    q_spec = pl.BlockSpec((1, N_HEAD_PAIRS, LANES, tile), lambda bi, i: (bi, 0, 0, i))
    seq_spec = pl.BlockSpec((1, N_HEAD_PAIRS, seq_len, LANES), lambda bi, i: (bi, 0, 0, 0))
    ctx_spec = pl.BlockSpec((1, N_HEAD_PAIRS, n_ctx, LANES), lambda bi, i: (0, 0, bi, 0))
    seq_t_spec = pl.BlockSpec((1, N_HEAD_PAIRS, LANES, seq_len), lambda bi, i: (bi, 0, 0, 0))
    ctx_t_spec = pl.BlockSpec((1, N_HEAD_PAIRS, LANES, n_ctx), lambda bi, i: (0, 0, 0, bi))
    return pl.pallas_call(
        functools.partial(_attention_kernel, n_rows=n_rows),
        grid=(b, n_rows // ATT_ROWS),
        in_specs=[pl.BlockSpec(memory_space=pltpu.SMEM),
                  q_spec, seq_spec, seq_t_spec, ctx_spec, ctx_t_spec, tile_spec, tile_spec,
                  _mod_block(mods, layer, None),
                  pl.BlockSpec(bias.shape, lambda bi, i: (0, 0, 0, 0),
                               pipeline_mode=pl.Buffered(1)),
                  _layer_block(w_out, j)],
        out_specs=tile_spec,
        out_shape=jax.ShapeDtypeStruct(x.shape, F32),
        scratch_shapes=[pltpu.VMEM((PIPE_SLOTS, n_keys, 4 * GRID_W), F32),
                        pltpu.VMEM((PIPE_SLOTS, n_keys, 4 * GRID_W), BF16),
                        pltpu.VMEM((PIPE_SLOTS, 8, 4 * GRID_W), F32),
                        pltpu.VMEM((N_HEAD_PAIRS, tile, LANES), F32)],
        compiler_params=_params(2),
        name="na_attention",
    )(jnp.asarray(ids), q, k, v, kc, vc, g, x, mods, bias, w_out)


def kernel(x, c, ctx, c_ctx, norm_g, ada_w, ada_b, pool_w_in, pool_w_grp, pool_scale, pool_w_out,
           na_w_in, na_rpb, na_w_out, conv_w_in, conv_dw, conv_db, conv_w_out, final_g):
    depth = norm_g.shape[0]
    batch, _, d = x.shape
    assert batch < COND_ROWS and WIN_ROWS // 2 <= HALO

    cond = jnp.zeros((COND_ROWS, d), F32).at[:batch].set(c).at[batch].set(c_ctx)
    mods = _modulation(cond, ada_w, ada_b).reshape(depth, COND_ROWS, 3, d)
    norm_g = norm_g.reshape(depth, 1, d)

    pool_weights = (*_fold_pool_weights(pool_w_in, pool_w_grp, pool_scale),
                    pool_w_out.astype(BF16))
    conv_weights = (conv_w_in.astype(BF16), conv_dw, conv_db.reshape(-1, 1, WIDTH),
                    conv_w_out.astype(BF16))
    na_w_in, na_w_out = na_w_in.astype(BF16), na_w_out.astype(BF16)

    last_ctx_reader = max([i for i in range(depth) if i % N_MIXERS == 1], default=-1)
    for i in range(depth):
        kind, j = i % N_MIXERS, i // N_MIXERS
        update_ctx = i < last_ctx_reader
        fg = final_g if i == depth - 1 else None
        if kind != 1:
            kernel_fn, weights = ((_pool_kernel, pool_weights) if kind == 0
                                  else (_conv_kernel, conv_weights))
            run = functools.partial(_mixer_layer, kernel_fn, mods=mods, layer=i, norm_g=norm_g,
                                    weights=weights, j=j)
            if update_ctx:
                ctx = run(x=ctx, mod_row=batch, final_g=None, tile=CTX_TILE, name=f"ctx_layer{i}")
            x = run(x=x, mod_row=None, final_g=fg, tile=X_TILE, name=f"layer{i}")
            continue

        if update_ctx:
            raise NotImplementedError("context output of a neighbourhood-attention layer")
        if fg is not None:
            raise NotImplementedError("final norm after a neighbourhood-attention layer")
        q, k, v, g = _project(x, mods, i, None, norm_g, na_w_in, j, (0, 1, 2, 3),
                              ("pairs_t", "pairs", "pairs_t", "flat"),
                              (HEAD_DIM ** -0.5 * LOG2E, 1.0, 1.0, 1.0), X_TILE, f"na_project{i}")
        kc, vc = _project(ctx.reshape(1, -1, d), mods, i, batch, norm_g, na_w_in, j, (1, 2),
                          ("pairs", "pairs_t"), (1.0, 1.0), X_TILE, f"na_ctx_project{i}")
        x = _attention_layer(x, q, k, v, kc, vc, g, mods, i, na_rpb[j], na_w_out, j)
    return x
```

```python
import functools
import math

import numpy as np

import jax
import jax.numpy as jnp
from jax import lax
from jax.experimental import pallas as pl
from jax.experimental.pallas import tpu as pltpu

D_MODEL = 1024
WIDTH = D_MODEL
GRID_W = 64
N_MIXERS = 3
POOL_WINDOWS = (2, 4, 8, 16)
POOL_GROUP = WIDTH // len(POOL_WINDOWS)
HEAD_DIM = 64
N_HEADS = WIDTH // HEAD_DIM
WIN_ROWS = 8
WIN_COLS = 16
EPS = 1e-6

HALO = 8
LANES = 128
N_HEAD_PAIRS = WIDTH // LANES
MASK_VALUE = -1e30
LOG2E = math.log2(math.e)
VMEM_LIMIT = 56 * 1024 * 1024

X_TILE = 1024
CTX_TILE = 256
ATT_ROWS = 8
SLAB_ROWS = WIN_ROWS + 2
PIPE_SLOTS = 2
PIPE_UNROLL = 10
BIAS_HEADS_PER_STEP = 8
CONV_BLOCK = 256
COND_ROWS = 16

F32 = jnp.float32
BF16 = jnp.bfloat16


def _silu(x):
    return x / (1.0 + jnp.exp(-x))


def _mod_norm(x, norm_g, shift, scale):
    ms = jnp.mean(x * x, axis=-1, keepdims=True)
    return (x * lax.rsqrt(ms + EPS)) * (norm_g * (1.0 + scale)) + shift


def _params(n_axes):
    return pltpu.CompilerParams(
        dimension_semantics=("arbitrary",) * n_axes, vmem_limit_bytes=VMEM_LIMIT)


def _layer_block(arr, j, col=None):
    shape = (1,) + arr.shape[1:]
    index = (j,) + (0,) * (arr.ndim - 1)
    if col is not None:
        shape = shape[:-1] + (WIDTH,)
        index = index[:-1] + (col,)
    return pl.BlockSpec(shape, lambda *_: index, pipeline_mode=pl.Buffered(1))


def _mod_block(mods, layer, row):
    d = mods.shape[-1]
    if row is None:
        return pl.BlockSpec((1, 1, 3, d), lambda bi, i: (layer, bi, 0, 0))
    return pl.BlockSpec((1, 1, 3, d), lambda bi, i: (layer, row, 0, 0))


def _modulation_kernel(cond_ref, w_ref, b_ref, o_ref):
    s = _silu(cond_ref[...]).astype(BF16)
    o_ref[0] = jnp.dot(s, w_ref[0].astype(BF16), preferred_element_type=F32) + b_ref[0]


def _modulation(cond, ada_w, ada_b):
    depth, d, n = ada_w.shape
    rows = cond.shape[0]
    tn = n
    return pl.pallas_call(
        _modulation_kernel,
        grid=(depth, n // tn),
        in_specs=[
            pl.BlockSpec((rows, d), lambda i, j: (0, 0)),
            pl.BlockSpec((1, d, tn), lambda i, j: (i, 0, j)),
            pl.BlockSpec((1, 1, tn), lambda i, j: (i, 0, j)),
        ],
        out_specs=pl.BlockSpec((1, rows, tn), lambda i, j: (i, 0, j)),
        out_shape=jax.ShapeDtypeStruct((depth, rows, n), F32),
        compiler_params=_params(2),
        name="modulation",
    )(cond, ada_w, ada_b.reshape(depth, 1, n))


def _normed_rows(x_ref, xp_ref, xn_ref, mod_ref, ng_ref):
    shift, scale = mod_ref[0, 0, 0:1, :], mod_ref[0, 0, 1:2, :]
    xe = jnp.concatenate([xp_ref[0], x_ref[0], xn_ref[0]], axis=0)
    he = _mod_norm(xe, ng_ref[0], shift, scale)
    return he.astype(BF16), he[HALO:he.shape[0] - HALO].astype(BF16)


def _zero_outside_sequence(e, tile):
    i = pl.program_id(1)
    keep_prev = (i > 0).astype(F32)
    keep_next = (i < pl.num_programs(1) - 1).astype(F32)
    return jnp.concatenate(
        [e[:HALO] * keep_prev, e[HALO:HALO + tile], e[HALO + tile:] * keep_next], axis=0)


def _window_sum(e, width, tile):
    n = e.shape[0]
    half = width // 2
    f = e
    k = 1
    while k < half:
        f = f + pltpu.roll(f, n - k, axis=0)
        k *= 2
    if half == HALO:
        return f[:tile] + f[HALO:HALO + tile]
    return (pltpu.roll(f, half, axis=0) + f)[HALO:HALO + tile]


def _window_mean(ws, width, tile, seq_len):
    half = width // 2
    row = lax.broadcasted_iota(jnp.int32, (HALO, 1), 0)

    def inv_count(first_row):
        t = row + (pl.program_id(1) * tile + first_row)
        cnt = jnp.minimum(t + half, seq_len) - jnp.maximum(t - half, 0)
        return 1.0 / cnt.astype(F32)

    return jnp.concatenate([ws[:HALO] * inv_count(0),
                            ws[HALO:tile - HALO] * (1.0 / width),
                            ws[tile - HALO:] * inv_count(tile - HALO)], axis=0)


def _finish(x, y, gate, o_ref, fg_ref):
    out = x + gate * y
    if fg_ref is not None:
        ms = jnp.mean(out * out, axis=-1, keepdims=True)
        out = (out * lax.rsqrt(ms + EPS)) * fg_ref[...]
    o_ref[0] = out


def _fold_pool_kernel(win_ref, wgrp_ref, ps_ref, wu_ref, wg_ref):
    for gi in range(len(POOL_WINDOWS)):
        cols = slice(gi * POOL_GROUP, (gi + 1) * POOL_GROUP)
        folded = jnp.dot(win_ref[0, :, cols], wgrp_ref[0, gi],
                         preferred_element_type=F32, precision=lax.Precision.HIGHEST)
        wu_ref[0, :, cols] = (folded * ps_ref[0, :, cols]).astype(BF16)
    wg_ref[0] = win_ref[0, :, WIDTH:].astype(BF16)


def _fold_pool_weights(w_in, w_grp, scale):
    n, d, _ = w_in.shape
    out = jax.ShapeDtypeStruct((n, d, WIDTH), BF16)
    return pl.pallas_call(
        _fold_pool_kernel,
        grid=(n,),
        in_specs=[pl.BlockSpec((1,) + w_in.shape[1:], lambda i: (i, 0, 0)),
                  pl.BlockSpec((1,) + w_grp.shape[1:], lambda i: (i, 0, 0, 0)),
                  pl.BlockSpec((1, 1, WIDTH), lambda i: (i, 0, 0))],
        out_specs=[pl.BlockSpec((1, d, WIDTH), lambda i: (i, 0, 0))] * 2,
        out_shape=[out, out],
        compiler_params=_params(1),
        name="fold_pool_weights",
    )(w_in, w_grp, scale.reshape(n, 1, WIDTH))


def _pool_kernel(x_ref, xp_ref, xn_ref, mod_ref, ng_ref, wu_ref, wg_ref, wout_ref,
                 *rest, tile, seq_len):
    fg_ref, o_ref = rest if len(rest) == 2 else (None, rest[0])
    he, hm = _normed_rows(x_ref, xp_ref, xn_ref, mod_ref, ng_ref)
    t = jnp.dot(he, wu_ref[0], preferred_element_type=F32)
    t = _zero_outside_sequence(t, tile)
    g = jnp.dot(hm, wg_ref[0], preferred_element_type=F32)
    mixed = []
    for gi, width in enumerate(POOL_WINDOWS):
        te = t[:, gi * POOL_GROUP:(gi + 1) * POOL_GROUP]
        pooled = _window_mean(_window_sum(te, width, tile), width, tile, seq_len)
        mixed.append(pooled - te[HALO:HALO + tile])
    mixed = jnp.concatenate(mixed, axis=1)
    z = (mixed * _silu(g)).astype(BF16)
    y = jnp.dot(z, wout_ref[0], preferred_element_type=F32)
    _finish(x_ref[0], y, mod_ref[0, 0, 2:3, :], o_ref, fg_ref)


def _conv_kernel(x_ref, xp_ref, xn_ref, mod_ref, ng_ref, win_ref, dw_ref, db_ref, wout_ref,
                 *rest, tile, seq_len):
    fg_ref, o_ref = rest if len(rest) == 2 else (None, rest[0])
    he, hm = _normed_rows(x_ref, xp_ref, xn_ref, mod_ref, ng_ref)
    n = tile + 2 * HALO
    rows = slice(HALO, HALO + tile)
    y = None
    for ci in range(WIDTH // CONV_BLOCK):
        cols = slice(ci * CONV_BLOCK, (ci + 1) * CONV_BLOCK)

        def proj(h, part):
            w = win_ref[0, :, part * WIDTH + ci * CONV_BLOCK:part * WIDTH + (ci + 1) * CONV_BLOCK]
            return jnp.dot(h, w, preferred_element_type=F32)

        z = _zero_outside_sequence(proj(he, 1) * proj(he, 2), tile)
        conv = (dw_ref[0, 0:1, cols] * pltpu.roll(z, 1, axis=0)[rows]
                + dw_ref[0, 1:2, cols] * z[rows]
                + dw_ref[0, 2:3, cols] * pltpu.roll(z, n - 1, axis=0)[rows]
                + db_ref[0, :, cols])
        yc = (proj(hm, 0) * conv * _silu(proj(hm, 3))).astype(BF16)
        part = jnp.dot(yc, wout_ref[0, cols, :], preferred_element_type=F32)
        y = part if y is None else y + part
    _finish(x_ref[0], y, mod_ref[0, 0, 2:3, :], o_ref, fg_ref)


def _mixer_layer(kernel_fn, x, mods, layer, mod_row, norm_g, weights, j, final_g, tile, name):
    b, seq_len, d = x.shape
    tile = min(tile, seq_len)
    halo_blocks = seq_len // HALO
    per_tile = tile // HALO
    in_specs = [
        pl.BlockSpec((1, tile, d), lambda bi, i: (bi, i, 0)),
        pl.BlockSpec((1, HALO, d), lambda bi, i: (bi, jnp.maximum(i * per_tile - 1, 0), 0)),
        pl.BlockSpec((1, HALO, d),
                     lambda bi, i: (bi, jnp.minimum((i + 1) * per_tile, halo_blocks - 1), 0)),
        _mod_block(mods, layer, mod_row),
        _layer_block(norm_g, layer),
    ] + [_layer_block(w, j) for w in weights]
    args = [x, x, x, mods, norm_g] + list(weights)
    if final_g is not None:
        in_specs.append(pl.BlockSpec((1, d), lambda bi, i: (0, 0)))
        args.append(final_g.reshape(1, d))
    return pl.pallas_call(
        functools.partial(kernel_fn, tile=tile, seq_len=seq_len),
        grid=(b, seq_len // tile),
        in_specs=in_specs,
        out_specs=pl.BlockSpec((1, tile, d), lambda bi, i: (bi, i, 0)),
        out_shape=jax.ShapeDtypeStruct(x.shape, F32),
        compiler_params=_params(2),
        name=name,
    )(*args)


def _project_kernel(x_ref, mod_ref, ng_ref, *refs, scales, layouts):
    n = len(scales)
    w_refs, o_refs = refs[:n], refs[n:]
    shift, scale = mod_ref[0, 0, 0:1, :], mod_ref[0, 0, 1:2, :]
    h = _mod_norm(x_ref[0], ng_ref[0], shift, scale).astype(BF16)
    for w_ref, o_ref, s, layout in zip(w_refs, o_refs, scales, layouts):
        p = jnp.dot(h, w_ref[0], preferred_element_type=F32)
        if s != 1.0:
            p = p * s
        if layout == "flat":
            o_ref[0] = p.astype(o_ref.dtype)
            continue
        for hp in range(N_HEAD_PAIRS):
            block = p[:, hp * LANES:(hp + 1) * LANES]
            o_ref[0, hp] = (block.T if layout == "pairs_t" else block).astype(o_ref.dtype)


def _project(x, mods, layer, mod_row, norm_g, w, j, cols, layouts, scales, tile, name):
    b, seq_len, d = x.shape
    tile = min(tile, seq_len)
    assert tile % LANES == 0
    specs = {
        "flat": (pl.BlockSpec((1, tile, WIDTH), lambda bi, i: (bi, i, 0)),
                 jax.ShapeDtypeStruct((b, seq_len, WIDTH), F32)),
        "pairs": (pl.BlockSpec((1, N_HEAD_PAIRS, tile, LANES), lambda bi, i: (bi, 0, i, 0)),
                  jax.ShapeDtypeStruct((b, N_HEAD_PAIRS, seq_len, LANES), BF16)),
        "pairs_t": (pl.BlockSpec((1, N_HEAD_PAIRS, LANES, tile), lambda bi, i: (bi, 0, 0, i)),
                    jax.ShapeDtypeStruct((b, N_HEAD_PAIRS, LANES, seq_len), BF16)),
    }
    out_specs = [specs[name_][0] for name_ in layouts]
    out_shape = [specs[name_][1] for name_ in layouts]
    return pl.pallas_call(
        functools.partial(_project_kernel, scales=scales, layouts=layouts),
        grid=(b, seq_len // tile),
        in_specs=[
            pl.BlockSpec((1, tile, d), lambda bi, i: (bi, i, 0)),
            _mod_block(mods, layer, mod_row),
            _layer_block(norm_g, layer),
        ] + [_layer_block(w, j, col) for col in cols],
        out_specs=out_specs,
        out_shape=out_shape,
        compiler_params=_params(2),
        name=name,
    )(x, mods, norm_g, *([w] * len(cols)))


SLAB_TILES = SLAB_ROWS // 2


def _slab_base(pair_row0, n_rows):
    return jnp.clip(pair_row0 - WIN_ROWS // 2, 0, n_rows - SLAB_ROWS)


def _bias_tile_plan(n_rows):
    tiles, ids = [], []
    for r0 in range(0, n_rows, 2):
        base = int(np.clip(r0 - WIN_ROWS // 2, 0, n_rows - SLAB_ROWS))
        for j in range(SLAB_TILES):
            quad = []
            for key_row in (base + 2 * j, base + 2 * j + 1):
                for r in (r0, r0 + 1):
                    start = int(np.clip(r - WIN_ROWS // 2, 0, n_rows - WIN_ROWS))
                    assert base <= start and start + WIN_ROWS <= base + SLAB_ROWS
                    inside = start <= key_row < start + WIN_ROWS
                    quad.append(key_row - r + WIN_ROWS - 1 if inside else None)
            quad = tuple(quad)
            if quad not in tiles:
                tiles.append(quad)
            ids.append(tiles.index(quad))
    return tiles, np.asarray(ids, np.int32)


def _bias_tiles_kernel(w_ref, o_ref, *, tiles):
    k_col = lax.broadcasted_iota(jnp.int32, (GRID_W, LANES), 0)
    lane = lax.broadcasted_iota(jnp.int32, (GRID_W, LANES), 1)
    q_col = lane % GRID_W
    c_start = jnp.clip(q_col - WIN_COLS // 2, 0, GRID_W - WIN_COLS)
    inside = (k_col >= c_start) & (k_col < c_start + WIN_COLS)
    first_query_row = lane < GRID_W
    masked = jnp.full((GRID_W, LANES), MASK_VALUE, F32)

    def toeplitz(head, d, shift):
        if d is None:
            return masked
        row = jnp.broadcast_to(w_ref[head, d:d + 1, :], (GRID_W, LANES))
        return pltpu.roll(row, shift, axis=1, stride=1, stride_axis=0) * LOG2E

    for head in range(w_ref.shape[0]):
        for tile_id, quad in enumerate(tiles):
            halves = [jnp.where(inside,
                                jnp.where(first_query_row, toeplitz(head, quad[2 * kr], 0),
                                          toeplitz(head, quad[2 * kr + 1], GRID_W)),
                                masked) for kr in range(2)]
            o_ref[head, tile_id] = jnp.concatenate(halves, axis=0)


def _attention_bias(rpb, tiles):
    h, n_dr, n_dc = rpb.shape
    assert n_dr == 2 * WIN_ROWS - 1 and n_dc == 2 * WIN_COLS - 1
    rev = rpb[..., ::-1]
    w = jnp.concatenate([rev[..., WIN_COLS - 1:], jnp.zeros((h, n_dr, LANES - n_dc), F32),
                         rev[..., :WIN_COLS - 1]], axis=-1)
    w = jnp.pad(w, ((0, 0), (0, 2 * WIN_ROWS - n_dr), (0, 0)))
    return pl.pallas_call(
        functools.partial(_bias_tiles_kernel, tiles=tiles),
        grid=(h // BIAS_HEADS_PER_STEP,),
        in_specs=[pl.BlockSpec((BIAS_HEADS_PER_STEP, 2 * WIN_ROWS, LANES), lambda i: (i, 0, 0))],
        out_specs=pl.BlockSpec((BIAS_HEADS_PER_STEP, len(tiles), LANES, LANES),
                               lambda i: (i, 0, 0, 0)),
        out_shape=jax.ShapeDtypeStruct((h, len(tiles), LANES, LANES), F32),
        compiler_params=_params(1),
        name="bias_tiles",
    )(w)


def _attention_kernel(ids_ref, q_ref, k_ref, v_ref, kc_ref, vc_ref, g_ref, x_ref, mod_ref,
                      bias_ref, wout_ref, o_ref, s_scr, p_scr, r_scr, o_scr, *, n_rows):
    pair_tokens = 2 * GRID_W
    n_ctx = kc_ref.shape[2]
    slab_keys = SLAB_ROWS * GRID_W
    n_items = (ATT_ROWS // 2) * N_HEAD_PAIRS
    w2 = 2 * pair_tokens
    first_head = lax.broadcasted_iota(jnp.int32, (LANES, pair_tokens), 0) < HEAD_DIM

    def locate(item):
        item = jnp.asarray(item, jnp.int32)
        pb, hp = item // N_HEAD_PAIRS, item % N_HEAD_PAIRS
        pair = pl.program_id(1) * (ATT_ROWS // 2) + pb
        q_off = pl.multiple_of(pb * pair_tokens, pair_tokens)
        k_off = pl.multiple_of(_slab_base(2 * pair, n_rows) * GRID_W, pair_tokens)
        return hp, pair, q_off, k_off

    def scores(item, slot):
        hp, _, q_off, k_off = locate(item)
        qt = q_ref[0, hp, :, pl.ds(q_off, pair_tokens)]
        zero = jnp.zeros_like(qt)
        q2 = jnp.concatenate([jnp.where(first_head, qt, zero),
                              jnp.where(first_head, zero, qt)], axis=1)
        s_scr[slot, :n_ctx, :w2] = jnp.dot(kc_ref[0, hp], q2, preferred_element_type=F32)
        s_scr[slot, n_ctx:, :w2] = jnp.dot(k_ref[0, hp, pl.ds(k_off, slab_keys), :], q2,
                                           preferred_element_type=F32)

    def softmax(item, slot):
        hp, pair, _, _ = locate(item)
        bias = jnp.concatenate(
            [jnp.concatenate([bias_ref[2 * hp + a, ids_ref[pair * SLAB_TILES + j]]
                              for a in range(2)], axis=1)
             for j in range(SLAB_TILES)], axis=0)
        s_ctx = s_scr[slot, :n_ctx, :w2]
        s_loc = s_scr[slot, n_ctx:, :w2] + bias
        m = jnp.maximum(jnp.max(s_loc, axis=0, keepdims=True),
                        jnp.max(s_ctx, axis=0, keepdims=True))
        p_ctx = jnp.exp2(s_ctx - m)
        p_loc = jnp.exp2(s_loc - m)
        denom = jnp.sum(p_loc, axis=0, keepdims=True) + jnp.sum(p_ctx, axis=0, keepdims=True)
        p_scr[slot, :n_ctx, :w2] = p_ctx.astype(BF16)
        p_scr[slot, n_ctx:, :w2] = p_loc.astype(BF16)
        r_scr[slot] = jnp.broadcast_to(1.0 / denom, r_scr.shape[1:])

    def values(item, slot):
        hp, _, q_off, k_off = locate(item)
        pv = (jnp.dot(vc_ref[0, hp], p_scr[slot, :n_ctx, :w2], preferred_element_type=F32)
              + jnp.dot(v_ref[0, hp, :, pl.ds(k_off, slab_keys)], p_scr[slot, n_ctx:, :w2],
                        preferred_element_type=F32))
        pv = pv * r_scr[slot, 0:1, :]
        out = jnp.where(first_head, pv[:, :pair_tokens], pv[:, pair_tokens:])
        o_scr[hp, pl.ds(q_off, pair_tokens), :] = out.T

    scores(0, 0)
    scores(1, 1)
    softmax(0, 0)

    def trip(t, carry):
        for i in range(PIPE_UNROLL):
            item = PIPE_UNROLL * t + i
            scores(item + 2, (i + 2) % PIPE_SLOTS)
            softmax(item + 1, (i + 1) % PIPE_SLOTS)
            values(item, i % PIPE_SLOTS)
        return carry

    assert (n_items - 2) % PIPE_UNROLL == 0 and PIPE_UNROLL % PIPE_SLOTS == 0
    lax.fori_loop(0, (n_items - 2) // PIPE_UNROLL, trip, 0)
    softmax(n_items - 1, (n_items - 1) % PIPE_SLOTS)
    values(n_items - 2, (n_items - 2) % PIPE_SLOTS)
    values(n_items - 1, (n_items - 1) % PIPE_SLOTS)

    gate = mod_ref[0, 0, 2:3, :]
    o = jnp.concatenate([o_scr[hp] for hp in range(N_HEAD_PAIRS)], axis=1)
    y = (o * _silu(g_ref[0])).astype(BF16)
    y = jnp.dot(y, wout_ref[0], preferred_element_type=F32)
    o_ref[0] = x_ref[0] + gate * y


def _attention_layer(x, q, k, v, kc, vc, g, mods, layer, rpb, w_out, j):
    b, seq_len, d = x.shape
    n_ctx = kc.shape[2] // b
    n_rows = seq_len // GRID_W
    assert n_rows % ATT_ROWS == 0 and n_rows >= SLAB_ROWS and ATT_ROWS % 2 == 0
    tile = ATT_ROWS * GRID_W
    n_keys = n_ctx + SLAB_ROWS * GRID_W
    tiles, ids = _bias_tile_plan(n_rows)
    bias = _attention_bias(rpb, tiles)
    tile_spec = pl.BlockSpec((1, tile, d), lambda bi, i: (bi, i, 0))
    q_spec = pl.BlockSpec((1, N_HEAD_PAIRS, LANES, tile), lambda bi, i: (bi, 0, 0, i))
    seq_spec = pl.BlockSpec((1, N_HEAD_PAIRS, seq_len, LANES), lambda bi, i: (bi, 0, 0, 0))
    ctx_spec = pl.BlockSpec((1, N_HEAD_PAIRS, n_ctx, LANES), lambda bi, i: (0, 0, bi, 0))
    seq_t_spec = pl.BlockSpec((1, N_HEAD_PAIRS, LANES, seq_len), lambda bi, i: (bi, 0, 0, 0))
    ctx_t_spec = pl.BlockSpec((1, N_HEAD_PAIRS, LANES, n_ctx), lambda bi, i: (0, 0, 0, bi))
    return pl.pallas_call(
        functools.partial(_attention_kernel, n_rows=n_rows),
        grid=(b, n_rows // ATT_ROWS),
        in_specs=[pl.BlockSpec(memory_space=pltpu.SMEM),
                  q_spec, seq_spec, seq_t_spec, ctx_spec, ctx_t_spec, tile_spec, tile_spec,
                  _mod_block(mods, layer, None),
                  pl.BlockSpec(bias.shape, lambda bi, i: (0, 0, 0, 0),
                               pipeline_mode=pl.Buffered(1)),
                  _layer_block(w_out, j)],
        out_specs=tile_spec,
        out_shape=jax.ShapeDtypeStruct(x.shape, F32),
        scratch_shapes=[pltpu.VMEM((PIPE_SLOTS, n_keys, 4 * GRID_W + LANES), F32),
                        pltpu.VMEM((PIPE_SLOTS, n_keys, 4 * GRID_W + LANES), BF16),
                        pltpu.VMEM((PIPE_SLOTS, 8, 4 * GRID_W), F32),
                        pltpu.VMEM((N_HEAD_PAIRS, tile, LANES), F32)],
        compiler_params=_params(2),
        name="na_attention",
    )(jnp.asarray(ids), q, k, v, kc, vc, g, x, mods, bias, w_out)


def kernel(x, c, ctx, c_ctx, norm_g, ada_w, ada_b, pool_w_in, pool_w_grp, pool_scale, pool_w_out,
           na_w_in, na_rpb, na_w_out, conv_w_in, conv_dw, conv_db, conv_w_out, final_g):
    depth = norm_g.shape[0]
    batch, _, d = x.shape
    assert batch < COND_ROWS and WIN_ROWS // 2 <= HALO

    cond = jnp.zeros((COND_ROWS, d), F32).at[:batch].set(c).at[batch].set(c_ctx)
    mods = _modulation(cond, ada_w, ada_b).reshape(depth, COND_ROWS, 3, d)
    norm_g = norm_g.reshape(depth, 1, d)

    pool_weights = (*_fold_pool_weights(pool_w_in, pool_w_grp, pool_scale),
                    pool_w_out.astype(BF16))
    conv_weights = (conv_w_in.astype(BF16), conv_dw, conv_db.reshape(-1, 1, WIDTH),
                    conv_w_out.astype(BF16))
    na_w_in, na_w_out = na_w_in.astype(BF16), na_w_out.astype(BF16)

    last_ctx_reader = max([i for i in range(depth) if i % N_MIXERS == 1], default=-1)
    for i in range(depth):
        kind, j = i % N_MIXERS, i // N_MIXERS
        update_ctx = i < last_ctx_reader
        fg = final_g if i == depth - 1 else None
        if kind != 1:
            kernel_fn, weights = ((_pool_kernel, pool_weights) if kind == 0
                                  else (_conv_kernel, conv_weights))
            run = functools.partial(_mixer_layer, kernel_fn, mods=mods, layer=i, norm_g=norm_g,
                                    weights=weights, j=j)
            if update_ctx:
                ctx = run(x=ctx, mod_row=batch, final_g=None, tile=CTX_TILE, name=f"ctx_layer{i}")
            x = run(x=x, mod_row=None, final_g=fg, tile=X_TILE, name=f"layer{i}")
            continue

        if update_ctx:
            raise NotImplementedError("context output of a neighbourhood-attention layer")
        if fg is not None:
            raise NotImplementedError("final norm after a neighbourhood-attention layer")
        q, k, v, g = _project(x, mods, i, None, norm_g, na_w_in, j, (0, 1, 2, 3),
                              ("pairs_t", "pairs", "pairs_t", "flat"),
                              (HEAD_DIM ** -0.5 * LOG2E, 1.0, 1.0, 1.0), X_TILE, f"na_project{i}")
        kc, vc = _project(ctx.reshape(1, -1, d), mods, i, batch, norm_g, na_w_in, j, (1, 2),
                          ("pairs", "pairs_t"), (1.0, 1.0), X_TILE, f"na_ctx_project{i}")
        x = _attention_layer(x, q, k, v, kc, vc, g, mods, i, na_rpb[j], na_w_out, j)
    return x
```

```python
import functools
import math

import numpy as np

import jax
import jax.numpy as jnp
from jax import lax
from jax.experimental import pallas as pl
from jax.experimental.pallas import tpu as pltpu

D_MODEL = 1024
WIDTH = D_MODEL
GRID_W = 64
N_MIXERS = 3
POOL_WINDOWS = (2, 4, 8, 16)
POOL_GROUP = WIDTH // len(POOL_WINDOWS)
HEAD_DIM = 64
N_HEADS = WIDTH // HEAD_DIM
WIN_ROWS = 8
WIN_COLS = 16
EPS = 1e-6

HALO = 8
LANES = 128
N_HEAD_PAIRS = WIDTH // LANES
MASK_VALUE = -1e30
LOG2E = math.log2(math.e)
VMEM_LIMIT = 56 * 1024 * 1024

X_TILE = 1024
CTX_TILE = 256
ATT_ROWS = 8
SLAB_ROWS = WIN_ROWS + 2
PIPE_SLOTS = 2
PIPE_UNROLL = 10
BIAS_HEADS_PER_STEP = 8
ONES_ROWS = 16
CONV_BLOCK = 256
COND_ROWS = 16

F32 = jnp.float32
BF16 = jnp.bfloat16


def _silu(x):
    return x / (1.0 + jnp.exp(-x))


def _mod_norm(x, norm_g, shift, scale):
    ms = jnp.mean(x * x, axis=-1, keepdims=True)
    return (x * lax.rsqrt(ms + EPS)) * (norm_g * (1.0 + scale)) + shift


def _params(n_axes):
    return pltpu.CompilerParams(
        dimension_semantics=("arbitrary",) * n_axes, vmem_limit_bytes=VMEM_LIMIT)


def _layer_block(arr, j, col=None):
    shape = (1,) + arr.shape[1:]
    index = (j,) + (0,) * (arr.ndim - 1)
    if col is not None:
        shape = shape[:-1] + (WIDTH,)
        index = index[:-1] + (col,)
    return pl.BlockSpec(shape, lambda *_: index, pipeline_mode=pl.Buffered(1))


def _mod_block(mods, layer, row):
    d = mods.shape[-1]
    if row is None:
        return pl.BlockSpec((1, 1, 3, d), lambda bi, i: (layer, bi, 0, 0))
    return pl.BlockSpec((1, 1, 3, d), lambda bi, i: (layer, row, 0, 0))


def _modulation_kernel(cond_ref, w_ref, b_ref, o_ref):
    s = _silu(cond_ref[...]).astype(BF16)
    o_ref[0] = jnp.dot(s, w_ref[0].astype(BF16), preferred_element_type=F32) + b_ref[0]


def _modulation(cond, ada_w, ada_b):
    depth, d, n = ada_w.shape
    rows = cond.shape[0]
    tn = n
    return pl.pallas_call(
        _modulation_kernel,
        grid=(depth, n // tn),
        in_specs=[
            pl.BlockSpec((rows, d), lambda i, j: (0, 0)),
            pl.BlockSpec((1, d, tn), lambda i, j: (i, 0, j)),
            pl.BlockSpec((1, 1, tn), lambda i, j: (i, 0, j)),
        ],
        out_specs=pl.BlockSpec((1, rows, tn), lambda i, j: (i, 0, j)),
        out_shape=jax.ShapeDtypeStruct((depth, rows, n), F32),
        compiler_params=_params(2),
        name="modulation",
    )(cond, ada_w, ada_b.reshape(depth, 1, n))


def _normed_rows(x_ref, xp_ref, xn_ref, mod_ref, ng_ref):
    shift, scale = mod_ref[0, 0, 0:1, :], mod_ref[0, 0, 1:2, :]
    xe = jnp.concatenate([xp_ref[0], x_ref[0], xn_ref[0]], axis=0)
    he = _mod_norm(xe, ng_ref[0], shift, scale)
    return he.astype(BF16), he[HALO:he.shape[0] - HALO].astype(BF16)


def _zero_outside_sequence(e, tile):
    i = pl.program_id(1)
    keep_prev = (i > 0).astype(F32)
    keep_next = (i < pl.num_programs(1) - 1).astype(F32)
    return jnp.concatenate(
        [e[:HALO] * keep_prev, e[HALO:HALO + tile], e[HALO + tile:] * keep_next], axis=0)


def _window_sum(e, width, tile):
    n = e.shape[0]
    half = width // 2
    f = e
    k = 1
    while k < half:
        f = f + pltpu.roll(f, n - k, axis=0)
        k *= 2
    if half == HALO:
        return f[:tile] + f[HALO:HALO + tile]
    return (pltpu.roll(f, half, axis=0) + f)[HALO:HALO + tile]


def _window_mean(ws, width, tile, seq_len):
    half = width // 2
    row = lax.broadcasted_iota(jnp.int32, (HALO, 1), 0)

    def inv_count(first_row):
        t = row + (pl.program_id(1) * tile + first_row)
        cnt = jnp.minimum(t + half, seq_len) - jnp.maximum(t - half, 0)
        return 1.0 / cnt.astype(F32)

    return jnp.concatenate([ws[:HALO] * inv_count(0),
                            ws[HALO:tile - HALO] * (1.0 / width),
                            ws[tile - HALO:] * inv_count(tile - HALO)], axis=0)


def _finish(x, y, gate, o_ref, fg_ref):
    out = x + gate * y
    if fg_ref is not None:
        ms = jnp.mean(out * out, axis=-1, keepdims=True)
        out = (out * lax.rsqrt(ms + EPS)) * fg_ref[...]
    o_ref[0] = out


def _fold_pool_kernel(win_ref, wgrp_ref, ps_ref, wu_ref, wg_ref):
    for gi in range(len(POOL_WINDOWS)):
        cols = slice(gi * POOL_GROUP, (gi + 1) * POOL_GROUP)
        folded = jnp.dot(win_ref[0, :, cols], wgrp_ref[0, gi],
                         preferred_element_type=F32, precision=lax.Precision.HIGHEST)
        wu_ref[0, :, cols] = (folded * ps_ref[0, :, cols]).astype(BF16)
    wg_ref[0] = win_ref[0, :, WIDTH:].astype(BF16)


def _fold_pool_weights(w_in, w_grp, scale):
    n, d, _ = w_in.shape
    out = jax.ShapeDtypeStruct((n, d, WIDTH), BF16)
    return pl.pallas_call(
        _fold_pool_kernel,
        grid=(n,),
        in_specs=[pl.BlockSpec((1,) + w_in.shape[1:], lambda i: (i, 0, 0)),
                  pl.BlockSpec((1,) + w_grp.shape[1:], lambda i: (i, 0, 0, 0)),
                  pl.BlockSpec((1, 1, WIDTH), lambda i: (i, 0, 0))],
        out_specs=[pl.BlockSpec((1, d, WIDTH), lambda i: (i, 0, 0))] * 2,
        out_shape=[out, out],
        compiler_params=_params(1),
        name="fold_pool_weights",
    )(w_in, w_grp, scale.reshape(n, 1, WIDTH))


def _pool_kernel(x_ref, xp_ref, xn_ref, mod_ref, ng_ref, wu_ref, wg_ref, wout_ref,
                 *rest, tile, seq_len):
    fg_ref, o_ref = rest if len(rest) == 2 else (None, rest[0])
    he, hm = _normed_rows(x_ref, xp_ref, xn_ref, mod_ref, ng_ref)
    t = jnp.dot(he, wu_ref[0], preferred_element_type=F32)
    t = _zero_outside_sequence(t, tile)
    g = jnp.dot(hm, wg_ref[0], preferred_element_type=F32)
    mixed = []
    for gi, width in enumerate(POOL_WINDOWS):
        te = t[:, gi * POOL_GROUP:(gi + 1) * POOL_GROUP]
        pooled = _window_mean(_window_sum(te, width, tile), width, tile, seq_len)
        mixed.append(pooled - te[HALO:HALO + tile])
    mixed = jnp.concatenate(mixed, axis=1)
    z = (mixed * _silu(g)).astype(BF16)
    y = jnp.dot(z, wout_ref[0], preferred_element_type=F32)
    _finish(x_ref[0], y, mod_ref[0, 0, 2:3, :], o_ref, fg_ref)


def _conv_kernel(x_ref, xp_ref, xn_ref, mod_ref, ng_ref, win_ref, dw_ref, db_ref, wout_ref,
                 *rest, tile, seq_len):
    fg_ref, o_ref = rest if len(rest) == 2 else (None, rest[0])
    he, hm = _normed_rows(x_ref, xp_ref, xn_ref, mod_ref, ng_ref)
    n = tile + 2 * HALO
    rows = slice(HALO, HALO + tile)
    y = None
    for ci in range(WIDTH // CONV_BLOCK):
        cols = slice(ci * CONV_BLOCK, (ci + 1) * CONV_BLOCK)

        def proj(h, part):
            w = win_ref[0, :, part * WIDTH + ci * CONV_BLOCK:part * WIDTH + (ci + 1) * CONV_BLOCK]
            return jnp.dot(h, w, preferred_element_type=F32)

        z = _zero_outside_sequence(proj(he, 1) * proj(he, 2), tile)
        conv = (dw_ref[0, 0:1, cols] * pltpu.roll(z, 1, axis=0)[rows]
                + dw_ref[0, 1:2, cols] * z[rows]
                + dw_ref[0, 2:3, cols] * pltpu.roll(z, n - 1, axis=0)[rows]
                + db_ref[0, :, cols])
        yc = (proj(hm, 0) * conv * _silu(proj(hm, 3))).astype(BF16)
        part = jnp.dot(yc, wout_ref[0, cols, :], preferred_element_type=F32)
        y = part if y is None else y + part
    _finish(x_ref[0], y, mod_ref[0, 0, 2:3, :], o_ref, fg_ref)


def _mixer_layer(kernel_fn, x, mods, layer, mod_row, norm_g, weights, j, final_g, tile, name):
    b, seq_len, d = x.shape
    tile = min(tile, seq_len)
    halo_blocks = seq_len // HALO
    per_tile = tile // HALO
    in_specs = [
        pl.BlockSpec((1, tile, d), lambda bi, i: (bi, i, 0)),
        pl.BlockSpec((1, HALO, d), lambda bi, i: (bi, jnp.maximum(i * per_tile - 1, 0), 0)),
        pl.BlockSpec((1, HALO, d),
                     lambda bi, i: (bi, jnp.minimum((i + 1) * per_tile, halo_blocks - 1), 0)),
        _mod_block(mods, layer, mod_row),
        _layer_block(norm_g, layer),
    ] + [_layer_block(w, j) for w in weights]
    args = [x, x, x, mods, norm_g] + list(weights)
    if final_g is not None:
        in_specs.append(pl.BlockSpec((1, d), lambda bi, i: (0, 0)))
        args.append(final_g.reshape(1, d))
    return pl.pallas_call(
        functools.partial(kernel_fn, tile=tile, seq_len=seq_len),
        grid=(b, seq_len // tile),
        in_specs=in_specs,
        out_specs=pl.BlockSpec((1, tile, d), lambda bi, i: (bi, i, 0)),
        out_shape=jax.ShapeDtypeStruct(x.shape, F32),
        compiler_params=_params(2),
        name=name,
    )(*args)


def _project_kernel(x_ref, mod_ref, ng_ref, *refs, scales, layouts):
    n = len(scales)
    w_refs, o_refs = refs[:n], refs[n:]
    shift, scale = mod_ref[0, 0, 0:1, :], mod_ref[0, 0, 1:2, :]
    h = _mod_norm(x_ref[0], ng_ref[0], shift, scale).astype(BF16)
    for w_ref, o_ref, s, layout in zip(w_refs, o_refs, scales, layouts):
        p = jnp.dot(h, w_ref[0], preferred_element_type=F32)
        if s != 1.0:
            p = p * s
        if layout == "flat":
            o_ref[0] = p.astype(o_ref.dtype)
            continue
        for hp in range(N_HEAD_PAIRS):
            block = p[:, hp * LANES:(hp + 1) * LANES]
            if layout == "pairs":
                o_ref[0, hp] = block.astype(o_ref.dtype)
            else:
                o_ref[0, hp, :LANES, :] = block.T.astype(o_ref.dtype)
            if layout == "pairs_t1":
                o_ref[0, hp, LANES:, :] = jnp.ones((ONES_ROWS, block.shape[0]), o_ref.dtype)


def _project(x, mods, layer, mod_row, norm_g, w, j, cols, layouts, scales, tile, name):
    b, seq_len, d = x.shape
    tile = min(tile, seq_len)
    assert tile % LANES == 0
    specs = {
        "flat": (pl.BlockSpec((1, tile, WIDTH), lambda bi, i: (bi, i, 0)),
                 jax.ShapeDtypeStruct((b, seq_len, WIDTH), F32)),
        "pairs": (pl.BlockSpec((1, N_HEAD_PAIRS, tile, LANES), lambda bi, i: (bi, 0, i, 0)),
                  jax.ShapeDtypeStruct((b, N_HEAD_PAIRS, seq_len, LANES), BF16)),
        "pairs_t": (pl.BlockSpec((1, N_HEAD_PAIRS, LANES, tile), lambda bi, i: (bi, 0, 0, i)),
                    jax.ShapeDtypeStruct((b, N_HEAD_PAIRS, LANES, seq_len), BF16)),
        "pairs_t1": (pl.BlockSpec((1, N_HEAD_PAIRS, LANES + ONES_ROWS, tile),
                                  lambda bi, i: (bi, 0, 0, i)),
                     jax.ShapeDtypeStruct((b, N_HEAD_PAIRS, LANES + ONES_ROWS, seq_len), BF16)),
    }
    out_specs = [specs[name_][0] for name_ in layouts]
    out_shape = [specs[name_][1] for name_ in layouts]
    return pl.pallas_call(
        functools.partial(_project_kernel, scales=scales, layouts=layouts),
        grid=(b, seq_len // tile),
        in_specs=[
            pl.BlockSpec((1, tile, d), lambda bi, i: (bi, i, 0)),
            _mod_block(mods, layer, mod_row),
            _layer_block(norm_g, layer),
        ] + [_layer_block(w, j, col) for col in cols],
        out_specs=out_specs,
        out_shape=out_shape,
        compiler_params=_params(2),
        name=name,
    )(x, mods, norm_g, *([w] * len(cols)))


SLAB_TILES = SLAB_ROWS // 2


def _slab_base(pair_row0, n_rows):
    return jnp.clip(pair_row0 - WIN_ROWS // 2, 0, n_rows - SLAB_ROWS)


def _bias_tile_plan(n_rows):
    tiles, ids = [], []
    for r0 in range(0, n_rows, 2):
        base = int(np.clip(r0 - WIN_ROWS // 2, 0, n_rows - SLAB_ROWS))
        for j in range(SLAB_TILES):
            quad = []
            for key_row in (base + 2 * j, base + 2 * j + 1):
                for r in (r0, r0 + 1):
                    start = int(np.clip(r - WIN_ROWS // 2, 0, n_rows - WIN_ROWS))
                    assert base <= start and start + WIN_ROWS <= base + SLAB_ROWS
                    inside = start <= key_row < start + WIN_ROWS
                    quad.append(key_row - r + WIN_ROWS - 1 if inside else None)
            quad = tuple(quad)
            if quad not in tiles:
                tiles.append(quad)
            ids.append(tiles.index(quad))
    return tiles, np.asarray(ids, np.int32)


def _bias_tiles_kernel(w_ref, o_ref, *, tiles):
    k_col = lax.broadcasted_iota(jnp.int32, (GRID_W, LANES), 0)
    lane = lax.broadcasted_iota(jnp.int32, (GRID_W, LANES), 1)
    q_col = lane % GRID_W
    c_start = jnp.clip(q_col - WIN_COLS // 2, 0, GRID_W - WIN_COLS)
    inside = (k_col >= c_start) & (k_col < c_start + WIN_COLS)
    first_query_row = lane < GRID_W
    masked = jnp.full((GRID_W, LANES), MASK_VALUE, F32)

    def toeplitz(head, d, shift):
        if d is None:
            return masked
        row = jnp.broadcast_to(w_ref[head, d:d + 1, :], (GRID_W, LANES))
        return pltpu.roll(row, shift, axis=1, stride=1, stride_axis=0) * LOG2E

    for head in range(w_ref.shape[0]):
        for tile_id, quad in enumerate(tiles):
            halves = [jnp.where(inside,
                                jnp.where(first_query_row, toeplitz(head, quad[2 * kr], 0),
                                          toeplitz(head, quad[2 * kr + 1], GRID_W)),
                                masked) for kr in range(2)]
            o_ref[head, tile_id] = jnp.concatenate(halves, axis=0)


def _attention_bias(rpb, tiles):
    h, n_dr, n_dc = rpb.shape
    assert n_dr == 2 * WIN_ROWS - 1 and n_dc == 2 * WIN_COLS - 1
    rev = rpb[..., ::-1]
    w = jnp.concatenate([rev[..., WIN_COLS - 1:], jnp.zeros((h, n_dr, LANES - n_dc), F32),
                         rev[..., :WIN_COLS - 1]], axis=-1)
    w = jnp.pad(w, ((0, 0), (0, 2 * WIN_ROWS - n_dr), (0, 0)))
    return pl.pallas_call(
        functools.partial(_bias_tiles_kernel, tiles=tiles),
        grid=(h // BIAS_HEADS_PER_STEP,),
        in_specs=[pl.BlockSpec((BIAS_HEADS_PER_STEP, 2 * WIN_ROWS, LANES), lambda i: (i, 0, 0))],
        out_specs=pl.BlockSpec((BIAS_HEADS_PER_STEP, len(tiles), LANES, LANES),
                               lambda i: (i, 0, 0, 0)),
        out_shape=jax.ShapeDtypeStruct((h, len(tiles), LANES, LANES), F32),
        compiler_params=_params(1),
        name="bias_tiles",
    )(w)


def _attention_kernel(ids_ref, q_ref, k_ref, v_ref, kc_ref, vc_ref, g_ref, x_ref, mod_ref,
                      bias_ref, wout_ref, o_ref, s_scr, p_scr, o_scr, *, n_rows):
    pair_tokens = 2 * GRID_W
    n_ctx = kc_ref.shape[2]
    slab_keys = SLAB_ROWS * GRID_W
    n_items = (ATT_ROWS // 2) * N_HEAD_PAIRS
    first_head = lax.broadcasted_iota(jnp.int32, (LANES, pair_tokens), 0) < HEAD_DIM

    def locate(item):
        item = jnp.asarray(item, jnp.int32)
        pb, hp = item // N_HEAD_PAIRS, item % N_HEAD_PAIRS
        pair = pl.program_id(1) * (ATT_ROWS // 2) + pb
        q_off = pl.multiple_of(pb * pair_tokens, pair_tokens)
        k_off = pl.multiple_of(_slab_base(2 * pair, n_rows) * GRID_W, pair_tokens)
        return hp, pair, q_off, k_off

    def scores(item, slot):
        hp, _, q_off, k_off = locate(item)
        qt = q_ref[0, hp, :, pl.ds(q_off, pair_tokens)]
        zero = jnp.zeros_like(qt)
        q2 = jnp.concatenate([jnp.where(first_head, qt, zero),
                              jnp.where(first_head, zero, qt)], axis=1)
        s_scr[slot, :n_ctx, :] = jnp.dot(kc_ref[0, hp], q2, preferred_element_type=F32)
        s_scr[slot, n_ctx:, :] = jnp.dot(k_ref[0, hp, pl.ds(k_off, slab_keys), :], q2,
                                           preferred_element_type=F32)

    def softmax(item, slot):
        hp, pair, _, _ = locate(item)
        bias = jnp.concatenate(
            [jnp.concatenate([bias_ref[2 * hp + a, ids_ref[pair * SLAB_TILES + j]]
                              for a in range(2)], axis=1)
             for j in range(SLAB_TILES)], axis=0)
        s_ctx = s_scr[slot, :n_ctx, :]
        s_loc = s_scr[slot, n_ctx:, :] + bias
        m = jnp.maximum(jnp.max(s_loc, axis=0, keepdims=True),
                        jnp.max(s_ctx, axis=0, keepdims=True))
        p_scr[slot, :n_ctx, :] = jnp.exp2(s_ctx - m).astype(BF16)
        p_scr[slot, n_ctx:, :] = jnp.exp2(s_loc - m).astype(BF16)

    def values(item, slot):
        hp, _, q_off, k_off = locate(item)
        pv = (jnp.dot(vc_ref[0, hp], p_scr[slot, :n_ctx, :], preferred_element_type=F32)
              + jnp.dot(v_ref[0, hp, :, pl.ds(k_off, slab_keys)], p_scr[slot, n_ctx:, :],
                        preferred_element_type=F32))
        pv = pv[:LANES] * (1.0 / pv[LANES:LANES + 1])
        out = jnp.where(first_head, pv[:, :pair_tokens], pv[:, pair_tokens:])
        o_scr[hp, pl.ds(q_off, pair_tokens), :] = out.T

    scores(0, 0)
    scores(1, 1)
    softmax(0, 0)

    def trip(t, carry):
        for i in range(PIPE_UNROLL):
            item = PIPE_UNROLL * t + i
            scores(item + 2, (i + 2) % PIPE_SLOTS)
            softmax(item + 1, (i + 1) % PIPE_SLOTS)
            values(item, i % PIPE_SLOTS)
        return carry

    assert (n_items - 2) % PIPE_UNROLL == 0 and PIPE_UNROLL % PIPE_SLOTS == 0
    lax.fori_loop(0, (n_items - 2) // PIPE_UNROLL, trip, 0)
    softmax(n_items - 1, (n_items - 1) % PIPE_SLOTS)
    values(n_items - 2, (n_items - 2) % PIPE_SLOTS)
    values(n_items - 1, (n_items - 1) % PIPE_SLOTS)

    gate = mod_ref[0, 0, 2:3, :]
    o = jnp.concatenate([o_scr[hp] for hp in range(N_HEAD_PAIRS)], axis=1)
    y = (o * _silu(g_ref[0])).astype(BF16)
    y = jnp.dot(y, wout_ref[0], preferred_element_type=F32)
    o_ref[0] = x_ref[0] + gate * y


def _attention_layer(x, q, k, v, kc, vc, g, mods, layer, rpb, w_out, j):
    b, seq_len, d = x.shape
    n_ctx = kc.shape[2] // b
    n_rows = seq_len // GRID_W
    assert n_rows % ATT_ROWS == 0 and n_rows >= SLAB_ROWS and ATT_ROWS % 2 == 0
    tile = ATT_ROWS * GRID_W
    n_keys = n_ctx + SLAB_ROWS * GRID_W
    tiles, ids = _bias_tile_plan(n_rows)
    bias = _attention_bias(rpb, tiles)
    tile_spec = pl.BlockSpec((1, tile, d), lambda bi, i: (bi, i, 0))
    q_spec = pl.BlockSpec((1, N_HEAD_PAIRS, LANES, tile), lambda bi, i: (bi, 0, 0, i))
    seq_spec = pl.BlockSpec((1, N_HEAD_PAIRS, seq_len, LANES), lambda bi, i: (bi, 0, 0, 0))
    ctx_spec = pl.BlockSpec((1, N_HEAD_PAIRS, n_ctx, LANES), lambda bi, i: (0, 0, bi, 0))
    v_rows = LANES + ONES_ROWS
    seq_t_spec = pl.BlockSpec((1, N_HEAD_PAIRS, v_rows, seq_len), lambda bi, i: (bi, 0, 0, 0))
    ctx_t_spec = pl.BlockSpec((1, N_HEAD_PAIRS, v_rows, n_ctx), lambda bi, i: (0, 0, 0, bi))
    return pl.pallas_call(
        functools.partial(_attention_kernel, n_rows=n_rows),
        grid=(b, n_rows // ATT_ROWS),
        in_specs=[pl.BlockSpec(memory_space=pltpu.SMEM),
                  q_spec, seq_spec, seq_t_spec, ctx_spec, ctx_t_spec, tile_spec, tile_spec,
                  _mod_block(mods, layer, None),
                  pl.BlockSpec(bias.shape, lambda bi, i: (0, 0, 0, 0),
                               pipeline_mode=pl.Buffered(1)),
                  _layer_block(w_out, j)],
        out_specs=tile_spec,
        out_shape=jax.ShapeDtypeStruct(x.shape, F32),
        scratch_shapes=[pltpu.VMEM((PIPE_SLOTS, n_keys, 4 * GRID_W), F32),
                        pltpu.VMEM((PIPE_SLOTS, n_keys, 4 * GRID_W), BF16),
                        pltpu.VMEM((N_HEAD_PAIRS, tile, LANES), F32)],
        compiler_params=_params(2),
        name="na_attention",
    )(jnp.asarray(ids), q, k, v, kc, vc, g, x, mods, bias, w_out)


def kernel(x, c, ctx, c_ctx, norm_g, ada_w, ada_b, pool_w_in, pool_w_grp, pool_scale, pool_w_out,
           na_w_in, na_rpb, na_w_out, conv_w_in, conv_dw, conv_db, conv_w_out, final_g):
    depth = norm_g.shape[0]
    batch, _, d = x.shape
    assert batch < COND_ROWS and WIN_ROWS // 2 <= HALO

    cond = jnp.zeros((COND_ROWS, d), F32).at[:batch].set(c).at[batch].set(c_ctx)
    mods = _modulation(cond, ada_w, ada_b).reshape(depth, COND_ROWS, 3, d)
    norm_g = norm_g.reshape(depth, 1, d)

    pool_weights = (*_fold_pool_weights(pool_w_in, pool_w_grp, pool_scale),
                    pool_w_out.astype(BF16))
    conv_weights = (conv_w_in.astype(BF16), conv_dw, conv_db.reshape(-1, 1, WIDTH),
                    conv_w_out.astype(BF16))
    na_w_in, na_w_out = na_w_in.astype(BF16), na_w_out.astype(BF16)

    last_ctx_reader = max([i for i in range(depth) if i % N_MIXERS == 1], default=-1)
    for i in range(depth):
        kind, j = i % N_MIXERS, i // N_MIXERS
        update_ctx = i < last_ctx_reader
        fg = final_g if i == depth - 1 else None
        if kind != 1:
            kernel_fn, weights = ((_pool_kernel, pool_weights) if kind == 0
                                  else (_conv_kernel, conv_weights))
            run = functools.partial(_mixer_layer, kernel_fn, mods=mods, layer=i, norm_g=norm_g,
                                    weights=weights, j=j)
            if update_ctx:
                ctx = run(x=ctx, mod_row=batch, final_g=None, tile=CTX_TILE, name=f"ctx_layer{i}")
            x = run(x=x, mod_row=None, final_g=fg, tile=X_TILE, name=f"layer{i}")
            continue

        if update_ctx:
            raise NotImplementedError("context output of a neighbourhood-attention layer")
        if fg is not None:
            raise NotImplementedError("final norm after a neighbourhood-attention layer")
        q, k, v, g = _project(x, mods, i, None, norm_g, na_w_in, j, (0, 1, 2, 3),
                              ("pairs_t", "pairs", "pairs_t1", "flat"),
                              (HEAD_DIM ** -0.5 * LOG2E, 1.0, 1.0, 1.0), X_TILE, f"na_project{i}")
        kc, vc = _project(ctx.reshape(1, -1, d), mods, i, batch, norm_g, na_w_in, j, (1, 2),
                          ("pairs", "pairs_t1"), (1.0, 1.0), X_TILE, f"na_ctx_project{i}")
        x = _attention_layer(x, q, k, v, kc, vc, g, mods, i, na_rpb[j], na_w_out, j)
    return x
```

```python
import functools
import math

import numpy as np

import jax
import jax.numpy as jnp
from jax import lax
from jax.experimental import pallas as pl
from jax.experimental.pallas import tpu as pltpu

D_MODEL = 1024
WIDTH = D_MODEL
GRID_W = 64
N_MIXERS = 3
POOL_WINDOWS = (2, 4, 8, 16)
POOL_GROUP = WIDTH // len(POOL_WINDOWS)
HEAD_DIM = 64
N_HEADS = WIDTH // HEAD_DIM
WIN_ROWS = 8
WIN_COLS = 16
EPS = 1e-6

HALO = 8
LANES = 128
N_HEAD_PAIRS = WIDTH // LANES
MASK_VALUE = -1e30
LOG2E = math.log2(math.e)
VMEM_LIMIT = 56 * 1024 * 1024

X_TILE = 1024
CTX_TILE = 256
ATT_ROWS = 8
SLAB_ROWS = WIN_ROWS + 2
PIPE_SLOTS = 2
PIPE_UNROLL = 30
BIAS_HEADS_PER_STEP = 8
CONV_BLOCK = 256
COND_ROWS = 16

F32 = jnp.float32
BF16 = jnp.bfloat16


def _silu(x):
    return x / (1.0 + jnp.exp(-x))


def _mod_norm(x, norm_g, shift, scale):
    ms = jnp.mean(x * x, axis=-1, keepdims=True)
    return (x * lax.rsqrt(ms + EPS)) * (norm_g * (1.0 + scale)) + shift


def _params(n_axes):
    return pltpu.CompilerParams(
        dimension_semantics=("arbitrary",) * n_axes, vmem_limit_bytes=VMEM_LIMIT)


def _layer_block(arr, j, col=None):
    shape = (1,) + arr.shape[1:]
    index = (j,) + (0,) * (arr.ndim - 1)
    if col is not None:
        shape = shape[:-1] + (WIDTH,)
        index = index[:-1] + (col,)
    return pl.BlockSpec(shape, lambda *_: index, pipeline_mode=pl.Buffered(1))


def _mod_block(mods, layer, row):
    d = mods.shape[-1]
    if row is None:
        return pl.BlockSpec((1, 1, 3, d), lambda bi, i: (layer, bi, 0, 0))
    return pl.BlockSpec((1, 1, 3, d), lambda bi, i: (layer, row, 0, 0))


def _modulation_kernel(cond_ref, w_ref, b_ref, o_ref):
    s = _silu(cond_ref[...]).astype(BF16)
    o_ref[0] = jnp.dot(s, w_ref[0].astype(BF16), preferred_element_type=F32) + b_ref[0]


def _modulation(cond, ada_w, ada_b):
    depth, d, n = ada_w.shape
    rows = cond.shape[0]
    tn = n
    return pl.pallas_call(
        _modulation_kernel,
        grid=(depth, n // tn),
        in_specs=[
            pl.BlockSpec((rows, d), lambda i, j: (0, 0)),
            pl.BlockSpec((1, d, tn), lambda i, j: (i, 0, j)),
            pl.BlockSpec((1, 1, tn), lambda i, j: (i, 0, j)),
        ],
        out_specs=pl.BlockSpec((1, rows, tn), lambda i, j: (i, 0, j)),
        out_shape=jax.ShapeDtypeStruct((depth, rows, n), F32),
        compiler_params=_params(2),
        name="modulation",
    )(cond, ada_w, ada_b.reshape(depth, 1, n))


def _normed_rows(x_ref, xp_ref, xn_ref, mod_ref, ng_ref):
    shift, scale = mod_ref[0, 0, 0:1, :], mod_ref[0, 0, 1:2, :]
    xe = jnp.concatenate([xp_ref[0], x_ref[0], xn_ref[0]], axis=0)
    he = _mod_norm(xe, ng_ref[0], shift, scale)
    return he.astype(BF16), he[HALO:he.shape[0] - HALO].astype(BF16)


def _zero_outside_sequence(e, tile):
    i = pl.program_id(1)
    keep_prev = (i > 0).astype(F32)
    keep_next = (i < pl.num_programs(1) - 1).astype(F32)
    return jnp.concatenate(
        [e[:HALO] * keep_prev, e[HALO:HALO + tile], e[HALO + tile:] * keep_next], axis=0)


def _window_sum(e, width, tile):
    n = e.shape[0]
    half = width // 2
    f = e
    k = 1
    while k < half:
        f = f + pltpu.roll(f, n - k, axis=0)
        k *= 2
    if half == HALO:
        return f[:tile] + f[HALO:HALO + tile]
    return (pltpu.roll(f, half, axis=0) + f)[HALO:HALO + tile]


def _window_mean(ws, width, tile, seq_len):
    half = width // 2
    row = lax.broadcasted_iota(jnp.int32, (HALO, 1), 0)

    def inv_count(first_row):
        t = row + (pl.program_id(1) * tile + first_row)
        cnt = jnp.minimum(t + half, seq_len) - jnp.maximum(t - half, 0)
        return 1.0 / cnt.astype(F32)

    return jnp.concatenate([ws[:HALO] * inv_count(0),
                            ws[HALO:tile - HALO] * (1.0 / width),
                            ws[tile - HALO:] * inv_count(tile - HALO)], axis=0)


def _finish(x, y, gate, o_ref, fg_ref):
    out = x + gate * y
    if fg_ref is not None:
        ms = jnp.mean(out * out, axis=-1, keepdims=True)
        out = (out * lax.rsqrt(ms + EPS)) * fg_ref[...]
    o_ref[0] = out


def _fold_pool_kernel(win_ref, wgrp_ref, ps_ref, wu_ref, wg_ref):
    for gi in range(len(POOL_WINDOWS)):
        cols = slice(gi * POOL_GROUP, (gi + 1) * POOL_GROUP)
        folded = jnp.dot(win_ref[0, :, cols], wgrp_ref[0, gi],
                         preferred_element_type=F32, precision=lax.Precision.HIGHEST)
        wu_ref[0, :, cols] = (folded * ps_ref[0, :, cols]).astype(BF16)
    wg_ref[0] = win_ref[0, :, WIDTH:].astype(BF16)


def _fold_pool_weights(w_in, w_grp, scale):
    n, d, _ = w_in.shape
    out = jax.ShapeDtypeStruct((n, d, WIDTH), BF16)
    return pl.pallas_call(
        _fold_pool_kernel,
        grid=(n,),
        in_specs=[pl.BlockSpec((1,) + w_in.shape[1:], lambda i: (i, 0, 0)),
                  pl.BlockSpec((1,) + w_grp.shape[1:], lambda i: (i, 0, 0, 0)),
                  pl.BlockSpec((1, 1, WIDTH), lambda i: (i, 0, 0))],
        out_specs=[pl.BlockSpec((1, d, WIDTH), lambda i: (i, 0, 0))] * 2,
        out_shape=[out, out],
        compiler_params=_params(1),
        name="fold_pool_weights",
    )(w_in, w_grp, scale.reshape(n, 1, WIDTH))


def _pool_kernel(x_ref, xp_ref, xn_ref, mod_ref, ng_ref, wu_ref, wg_ref, wout_ref,
                 *rest, tile, seq_len):
    fg_ref, o_ref = rest if len(rest) == 2 else (None, rest[0])
    he, hm = _normed_rows(x_ref, xp_ref, xn_ref, mod_ref, ng_ref)
    t = jnp.dot(he, wu_ref[0], preferred_element_type=F32)
    t = _zero_outside_sequence(t, tile)
    g = jnp.dot(hm, wg_ref[0], preferred_element_type=F32)
    mixed = []
    for gi, width in enumerate(POOL_WINDOWS):
        te = t[:, gi * POOL_GROUP:(gi + 1) * POOL_GROUP]
        pooled = _window_mean(_window_sum(te, width, tile), width, tile, seq_len)
        mixed.append(pooled - te[HALO:HALO + tile])
    mixed = jnp.concatenate(mixed, axis=1)
    z = (mixed * _silu(g)).astype(BF16)
    y = jnp.dot(z, wout_ref[0], preferred_element_type=F32)
    _finish(x_ref[0], y, mod_ref[0, 0, 2:3, :], o_ref, fg_ref)


def _conv_kernel(x_ref, xp_ref, xn_ref, mod_ref, ng_ref, win_ref, dw_ref, db_ref, wout_ref,
                 *rest, tile, seq_len):
    fg_ref, o_ref = rest if len(rest) == 2 else (None, rest[0])
    he, hm = _normed_rows(x_ref, xp_ref, xn_ref, mod_ref, ng_ref)
    n = tile + 2 * HALO
    rows = slice(HALO, HALO + tile)
    y = None
    for ci in range(WIDTH // CONV_BLOCK):
        cols = slice(ci * CONV_BLOCK, (ci + 1) * CONV_BLOCK)

        def proj(h, part):
            w = win_ref[0, :, part * WIDTH + ci * CONV_BLOCK:part * WIDTH + (ci + 1) * CONV_BLOCK]
            return jnp.dot(h, w, preferred_element_type=F32)

        z = _zero_outside_sequence(proj(he, 1) * proj(he, 2), tile)
        conv = (dw_ref[0, 0:1, cols] * pltpu.roll(z, 1, axis=0)[rows]
                + dw_ref[0, 1:2, cols] * z[rows]
                + dw_ref[0, 2:3, cols] * pltpu.roll(z, n - 1, axis=0)[rows]
                + db_ref[0, :, cols])
        yc = (proj(hm, 0) * conv * _silu(proj(hm, 3))).astype(BF16)
        part = jnp.dot(yc, wout_ref[0, cols, :], preferred_element_type=F32)
        y = part if y is None else y + part
    _finish(x_ref[0], y, mod_ref[0, 0, 2:3, :], o_ref, fg_ref)


def _mixer_layer(kernel_fn, x, mods, layer, mod_row, norm_g, weights, j, final_g, tile, name):
    b, seq_len, d = x.shape
    tile = min(tile, seq_len)
    halo_blocks = seq_len // HALO
    per_tile = tile // HALO
    in_specs = [
        pl.BlockSpec((1, tile, d), lambda bi, i: (bi, i, 0)),
        pl.BlockSpec((1, HALO, d), lambda bi, i: (bi, jnp.maximum(i * per_tile - 1, 0), 0)),
        pl.BlockSpec((1, HALO, d),
                     lambda bi, i: (bi, jnp.minimum((i + 1) * per_tile, halo_blocks - 1), 0)),
        _mod_block(mods, layer, mod_row),
        _layer_block(norm_g, layer),
    ] + [_layer_block(w, j) for w in weights]
    args = [x, x, x, mods, norm_g] + list(weights)
    if final_g is not None:
        in_specs.append(pl.BlockSpec((1, d), lambda bi, i: (0, 0)))
        args.append(final_g.reshape(1, d))
    return pl.pallas_call(
        functools.partial(kernel_fn, tile=tile, seq_len=seq_len),
        grid=(b, seq_len // tile),
        in_specs=in_specs,
        out_specs=pl.BlockSpec((1, tile, d), lambda bi, i: (bi, i, 0)),
        out_shape=jax.ShapeDtypeStruct(x.shape, F32),
        compiler_params=_params(2),
        name=name,
    )(*args)


def _project_kernel(x_ref, mod_ref, ng_ref, *refs, scales, layouts):
    n = len(scales)
    w_refs, o_refs = refs[:n], refs[n:]
    shift, scale = mod_ref[0, 0, 0:1, :], mod_ref[0, 0, 1:2, :]
    h = _mod_norm(x_ref[0], ng_ref[0], shift, scale).astype(BF16)
    for w_ref, o_ref, s, layout in zip(w_refs, o_refs, scales, layouts):
        p = jnp.dot(h, w_ref[0], preferred_element_type=F32)
        if s != 1.0:
            p = p * s
        if layout == "flat":
            o_ref[0] = p.astype(o_ref.dtype)
            continue
        for hp in range(N_HEAD_PAIRS):
            block = p[:, hp * LANES:(hp + 1) * LANES]
            o_ref[0, hp] = (block.T if layout == "pairs_t" else block).astype(o_ref.dtype)


def _project(x, mods, layer, mod_row, norm_g, w, j, cols, layouts, scales, tile, name):
    b, seq_len, d = x.shape
    tile = min(tile, seq_len)
    assert tile % LANES == 0
    specs = {
        "flat": (pl.BlockSpec((1, tile, WIDTH), lambda bi, i: (bi, i, 0)),
                 jax.ShapeDtypeStruct((b, seq_len, WIDTH), F32)),
        "pairs": (pl.BlockSpec((1, N_HEAD_PAIRS, tile, LANES), lambda bi, i: (bi, 0, i, 0)),
                  jax.ShapeDtypeStruct((b, N_HEAD_PAIRS, seq_len, LANES), BF16)),
        "pairs_t": (pl.BlockSpec((1, N_HEAD_PAIRS, LANES, tile), lambda bi, i: (bi, 0, 0, i)),
                    jax.ShapeDtypeStruct((b, N_HEAD_PAIRS, LANES, seq_len), BF16)),
    }
    out_specs = [specs[name_][0] for name_ in layouts]
    out_shape = [specs[name_][1] for name_ in layouts]
    return pl.pallas_call(
        functools.partial(_project_kernel, scales=scales, layouts=layouts),
        grid=(b, seq_len // tile),
        in_specs=[
            pl.BlockSpec((1, tile, d), lambda bi, i: (bi, i, 0)),
            _mod_block(mods, layer, mod_row),
            _layer_block(norm_g, layer),
        ] + [_layer_block(w, j, col) for col in cols],
        out_specs=out_specs,
        out_shape=out_shape,
        compiler_params=_params(2),
        name=name,
    )(x, mods, norm_g, *([w] * len(cols)))


SLAB_TILES = SLAB_ROWS // 2


def _slab_base(pair_row0, n_rows):
    return jnp.clip(pair_row0 - WIN_ROWS // 2, 0, n_rows - SLAB_ROWS)


def _bias_tile_plan(n_rows):
    tiles, ids = [], []
    for r0 in range(0, n_rows, 2):
        base = int(np.clip(r0 - WIN_ROWS // 2, 0, n_rows - SLAB_ROWS))
        for j in range(SLAB_TILES):
            quad = []
            for key_row in (base + 2 * j, base + 2 * j + 1):
                for r in (r0, r0 + 1):
                    start = int(np.clip(r - WIN_ROWS // 2, 0, n_rows - WIN_ROWS))
                    assert base <= start and start + WIN_ROWS <= base + SLAB_ROWS
                    inside = start <= key_row < start + WIN_ROWS
                    quad.append(key_row - r + WIN_ROWS - 1 if inside else None)
            quad = tuple(quad)
            if quad not in tiles:
                tiles.append(quad)
            ids.append(tiles.index(quad))
    return tiles, np.asarray(ids, np.int32)


def _bias_tiles_kernel(w_ref, o_ref, *, tiles):
    k_col = lax.broadcasted_iota(jnp.int32, (GRID_W, LANES), 0)
    lane = lax.broadcasted_iota(jnp.int32, (GRID_W, LANES), 1)
    q_col = lane % GRID_W
    c_start = jnp.clip(q_col - WIN_COLS // 2, 0, GRID_W - WIN_COLS)
    inside = (k_col >= c_start) & (k_col < c_start + WIN_COLS)
    first_query_row = lane < GRID_W
    masked = jnp.full((GRID_W, LANES), MASK_VALUE, F32)

    def toeplitz(head, d, shift):
        if d is None:
            return masked
        row = jnp.broadcast_to(w_ref[head, d:d + 1, :], (GRID_W, LANES))
        return pltpu.roll(row, shift, axis=1, stride=1, stride_axis=0) * LOG2E

    for head in range(w_ref.shape[0]):
        for tile_id, quad in enumerate(tiles):
            halves = [jnp.where(inside,
                                jnp.where(first_query_row, toeplitz(head, quad[2 * kr], 0),
                                          toeplitz(head, quad[2 * kr + 1], GRID_W)),
                                masked) for kr in range(2)]
            o_ref[head, tile_id] = jnp.concatenate(halves, axis=0)


def _attention_bias(rpb, tiles):
    h, n_dr, n_dc = rpb.shape
    assert n_dr == 2 * WIN_ROWS - 1 and n_dc == 2 * WIN_COLS - 1
    rev = rpb[..., ::-1]
    w = jnp.concatenate([rev[..., WIN_COLS - 1:], jnp.zeros((h, n_dr, LANES - n_dc), F32),
                         rev[..., :WIN_COLS - 1]], axis=-1)
    w = jnp.pad(w, ((0, 0), (0, 2 * WIN_ROWS - n_dr), (0, 0)))
    return pl.pallas_call(
        functools.partial(_bias_tiles_kernel, tiles=tiles),
        grid=(h // BIAS_HEADS_PER_STEP,),
        in_specs=[pl.BlockSpec((BIAS_HEADS_PER_STEP, 2 * WIN_ROWS, LANES), lambda i: (i, 0, 0))],
        out_specs=pl.BlockSpec((BIAS_HEADS_PER_STEP, len(tiles), LANES, LANES),
                               lambda i: (i, 0, 0, 0)),
        out_shape=jax.ShapeDtypeStruct((h, len(tiles), LANES, LANES), F32),
        compiler_params=_params(1),
        name="bias_tiles",
    )(w)


def _attention_kernel(ids_ref, q_ref, k_ref, v_ref, kc_ref, vc_ref, g_ref, x_ref, mod_ref,
                      bias_ref, wout_ref, o_ref, s_scr, p_scr, r_scr, o_scr, *, n_rows):
    pair_tokens = 2 * GRID_W
    n_ctx = kc_ref.shape[2]
    slab_keys = SLAB_ROWS * GRID_W
    n_items = (ATT_ROWS // 2) * N_HEAD_PAIRS
    first_head = lax.broadcasted_iota(jnp.int32, (LANES, pair_tokens), 0) < HEAD_DIM

    def locate(item):
        item = jnp.asarray(item, jnp.int32)
        pb, hp = item // N_HEAD_PAIRS, item % N_HEAD_PAIRS
        pair = pl.program_id(1) * (ATT_ROWS // 2) + pb
        q_off = pl.multiple_of(pb * pair_tokens, pair_tokens)
        k_off = pl.multiple_of(_slab_base(2 * pair, n_rows) * GRID_W, pair_tokens)
        return hp, pair, q_off, k_off

    def scores(item, slot):
        hp, _, q_off, k_off = locate(item)
        qt = q_ref[0, hp, :, pl.ds(q_off, pair_tokens)]
        zero = jnp.zeros_like(qt)
        q2 = jnp.concatenate([jnp.where(first_head, qt, zero),
                              jnp.where(first_head, zero, qt)], axis=1)
        s_scr[slot, :n_ctx, :] = jnp.dot(kc_ref[0, hp], q2, preferred_element_type=F32)
        s_scr[slot, n_ctx:, :] = jnp.dot(k_ref[0, hp, pl.ds(k_off, slab_keys), :], q2,
                                           preferred_element_type=F32)

    def softmax(item, slot):
        hp, pair, _, _ = locate(item)
        bias = jnp.concatenate(
            [jnp.concatenate([bias_ref[2 * hp + a, ids_ref[pair * SLAB_TILES + j]]
                              for a in range(2)], axis=1)
             for j in range(SLAB_TILES)], axis=0)
        s_ctx = s_scr[slot, :n_ctx, :]
        s_loc = s_scr[slot, n_ctx:, :] + bias
        m = jnp.maximum(jnp.max(s_loc, axis=0, keepdims=True),
                        jnp.max(s_ctx, axis=0, keepdims=True))
        p_ctx = jnp.exp2(s_ctx - m)
        p_loc = jnp.exp2(s_loc - m)
        denom = jnp.sum(p_loc, axis=0, keepdims=True) + jnp.sum(p_ctx, axis=0, keepdims=True)
        p_scr[slot, :n_ctx, :] = p_ctx.astype(BF16)
        p_scr[slot, n_ctx:, :] = p_loc.astype(BF16)
        r_scr[slot] = jnp.broadcast_to(1.0 / denom, r_scr.shape[1:])

    def values(item, slot):
        hp, _, q_off, k_off = locate(item)
        pv = (jnp.dot(vc_ref[0, hp], p_scr[slot, :n_ctx, :], preferred_element_type=F32)
              + jnp.dot(v_ref[0, hp, :, pl.ds(k_off, slab_keys)], p_scr[slot, n_ctx:, :],
                        preferred_element_type=F32))
        pv = pv * r_scr[slot, 0:1, :]
        out = jnp.where(first_head, pv[:, :pair_tokens], pv[:, pair_tokens:])
        o_scr[hp, pl.ds(q_off, pair_tokens), :] = out.T

    scores(0, 0)
    scores(1, 1)
    softmax(0, 0)

    def trip(t, carry):
        for i in range(PIPE_UNROLL):
            item = PIPE_UNROLL * t + i
            scores(item + 2, (i + 2) % PIPE_SLOTS)
            softmax(item + 1, (i + 1) % PIPE_SLOTS)
            values(item, i % PIPE_SLOTS)
        return carry

    assert (n_items - 2) % PIPE_UNROLL == 0 and PIPE_UNROLL % PIPE_SLOTS == 0
    lax.fori_loop(0, (n_items - 2) // PIPE_UNROLL, trip, 0)
    softmax(n_items - 1, (n_items - 1) % PIPE_SLOTS)
    values(n_items - 2, (n_items - 2) % PIPE_SLOTS)
    values(n_items - 1, (n_items - 1) % PIPE_SLOTS)

    gate = mod_ref[0, 0, 2:3, :]
    o = jnp.concatenate([o_scr[hp] for hp in range(N_HEAD_PAIRS)], axis=1)
    y = (o * _silu(g_ref[0])).astype(BF16)
    y = jnp.dot(y, wout_ref[0], preferred_element_type=F32)
    o_ref[0] = x_ref[0] + gate * y


def _attention_layer(x, q, k, v, kc, vc, g, mods, layer, rpb, w_out, j):
    b, seq_len, d = x.shape
    n_ctx = kc.shape[2] // b
    n_rows = seq_len // GRID_W
    assert n_rows % ATT_ROWS == 0 and n_rows >= SLAB_ROWS and ATT_ROWS % 2 == 0
    tile = ATT_ROWS * GRID_W
    n_keys = n_ctx + SLAB_ROWS * GRID_W
    tiles, ids = _bias_tile_plan(n_rows)
    bias = _attention_bias(rpb, tiles)
    tile_spec = pl.BlockSpec((1, tile, d), lambda bi, i: (bi, i, 0))
    q_spec = pl.BlockSpec((1, N_HEAD_PAIRS, LANES, tile), lambda bi, i: (bi, 0, 0, i))
    seq_spec = pl.BlockSpec((1, N_HEAD_PAIRS, seq_len, LANES), lambda bi, i: (bi, 0, 0, 0))
    ctx_spec = pl.BlockSpec((1, N_HEAD_PAIRS, n_ctx, LANES), lambda bi, i: (0, 0, bi, 0))
    seq_t_spec = pl.BlockSpec((1, N_HEAD_PAIRS, LANES, seq_len), lambda bi, i: (bi, 0, 0, 0))
    ctx_t_spec = pl.BlockSpec((1, N_HEAD_PAIRS, LANES, n_ctx), lambda bi, i: (0, 0, 0, bi))
    return pl.pallas_call(
        functools.partial(_attention_kernel, n_rows=n_rows),
        grid=(b, n_rows // ATT_ROWS),
        in_specs=[pl.BlockSpec(memory_space=pltpu.SMEM),
                  q_spec, seq_spec, seq_t_spec, ctx_spec, ctx_t_spec, tile_spec, tile_spec,
                  _mod_block(mods, layer, None),
                  pl.BlockSpec(bias.shape, lambda bi, i: (0, 0, 0, 0),
                               pipeline_mode=pl.Buffered(1)),
                  _layer_block(w_out, j)],
        out_specs=tile_spec,
        out_shape=jax.ShapeDtypeStruct(x.shape, F32),
        scratch_shapes=[pltpu.VMEM((PIPE_SLOTS, n_keys, 4 * GRID_W), F32),
                        pltpu.VMEM((PIPE_SLOTS, n_keys, 4 * GRID_W), BF16),
                        pltpu.VMEM((PIPE_SLOTS, 8, 4 * GRID_W), F32),
                        pltpu.VMEM((N_HEAD_PAIRS, tile, LANES), F32)],
        compiler_params=_params(2),
        name="na_attention",
    )(jnp.asarray(ids), q, k, v, kc, vc, g, x, mods, bias, w_out)


def kernel(x, c, ctx, c_ctx, norm_g, ada_w, ada_b, pool_w_in, pool_w_grp, pool_scale, pool_w_out,
           na_w_in, na_rpb, na_w_out, conv_w_in, conv_dw, conv_db, conv_w_out, final_g):
    depth = norm_g.shape[0]
    batch, _, d = x.shape
    assert batch < COND_ROWS and WIN_ROWS // 2 <= HALO

    cond = jnp.zeros((COND_ROWS, d), F32).at[:batch].set(c).at[batch].set(c_ctx)
    mods = _modulation(cond, ada_w, ada_b).reshape(depth, COND_ROWS, 3, d)
    norm_g = norm_g.reshape(depth, 1, d)

    pool_weights = (*_fold_pool_weights(pool_w_in, pool_w_grp, pool_scale),
                    pool_w_out.astype(BF16))
    conv_weights = (conv_w_in.astype(BF16), conv_dw, conv_db.reshape(-1, 1, WIDTH),
                    conv_w_out.astype(BF16))
    na_w_in, na_w_out = na_w_in.astype(BF16), na_w_out.astype(BF16)

    last_ctx_reader = max([i for i in range(depth) if i % N_MIXERS == 1], default=-1)
    for i in range(depth):
        kind, j = i % N_MIXERS, i // N_MIXERS
        update_ctx = i < last_ctx_reader
        fg = final_g if i == depth - 1 else None
        if kind != 1:
            kernel_fn, weights = ((_pool_kernel, pool_weights) if kind == 0
                                  else (_conv_kernel, conv_weights))
            run = functools.partial(_mixer_layer, kernel_fn, mods=mods, layer=i, norm_g=norm_g,
                                    weights=weights, j=j)
            if update_ctx:
                ctx = run(x=ctx, mod_row=batch, final_g=None, tile=CTX_TILE, name=f"ctx_layer{i}")
            x = run(x=x, mod_row=None, final_g=fg, tile=X_TILE, name=f"layer{i}")
            continue

        if update_ctx:
            raise NotImplementedError("context output of a neighbourhood-attention layer")
        if fg is not None:
            raise NotImplementedError("final norm after a neighbourhood-attention layer")
        q, k, v, g = _project(x, mods, i, None, norm_g, na_w_in, j, (0, 1, 2, 3),
                              ("pairs_t", "pairs", "pairs_t", "flat"),
                              (HEAD_DIM ** -0.5 * LOG2E, 1.0, 1.0, 1.0), X_TILE, f"na_project{i}")
        kc, vc = _project(ctx.reshape(1, -1, d), mods, i, batch, norm_g, na_w_in, j, (1, 2),
                          ("pairs", "pairs_t"), (1.0, 1.0), X_TILE, f"na_ctx_project{i}")
        x = _attention_layer(x, q, k, v, kc, vc, g, mods, i, na_rpb[j], na_w_out, j)
    return x
```

```python
import functools
import math

import numpy as np

import jax
import jax.numpy as jnp
from jax import lax
from jax.experimental import pallas as pl
from jax.experimental.pallas import tpu as pltpu

D_MODEL = 1024
WIDTH = D_MODEL
GRID_W = 64
N_MIXERS = 3
POOL_WINDOWS = (2, 4, 8, 16)
POOL_GROUP = WIDTH // len(POOL_WINDOWS)
HEAD_DIM = 64
N_HEADS = WIDTH // HEAD_DIM
WIN_ROWS = 8
WIN_COLS = 16
EPS = 1e-6

HALO = 8
LANES = 128
N_HEAD_PAIRS = WIDTH // LANES
MASK_VALUE = -1e30
LOG2E = math.log2(math.e)
VMEM_LIMIT = 56 * 1024 * 1024

X_TILE = 1024
CTX_TILE = 256
ATT_ROWS = 8
SLAB_ROWS = WIN_ROWS + 2
PIPE_SLOTS = 2
BIAS_HEADS_PER_STEP = 8
CONV_BLOCK = 256
COND_ROWS = 16

F32 = jnp.float32
BF16 = jnp.bfloat16


def _silu(x):
    return x / (1.0 + jnp.exp(-x))


def _mod_norm(x, norm_g, shift, scale):
    ms = jnp.mean(x * x, axis=-1, keepdims=True)
    return (x * lax.rsqrt(ms + EPS)) * (norm_g * (1.0 + scale)) + shift


def _params(n_axes):
    return pltpu.CompilerParams(
        dimension_semantics=("arbitrary",) * n_axes, vmem_limit_bytes=VMEM_LIMIT)


def _layer_block(arr, j, col=None):
    shape = (1,) + arr.shape[1:]
    index = (j,) + (0,) * (arr.ndim - 1)
    if col is not None:
        shape = shape[:-1] + (WIDTH,)
        index = index[:-1] + (col,)
    return pl.BlockSpec(shape, lambda *_: index, pipeline_mode=pl.Buffered(1))


def _mod_block(mods, layer, row):
    d = mods.shape[-1]
    if row is None:
        return pl.BlockSpec((1, 1, 3, d), lambda bi, i: (layer, bi, 0, 0))
    return pl.BlockSpec((1, 1, 3, d), lambda bi, i: (layer, row, 0, 0))


def _modulation_kernel(cond_ref, w_ref, b_ref, o_ref):
    s = _silu(cond_ref[...]).astype(BF16)
    o_ref[0] = jnp.dot(s, w_ref[0].astype(BF16), preferred_element_type=F32) + b_ref[0]


def _modulation(cond, ada_w, ada_b):
    depth, d, n = ada_w.shape
    rows = cond.shape[0]
    tn = n
    return pl.pallas_call(
        _modulation_kernel,
        grid=(depth, n // tn),
        in_specs=[
            pl.BlockSpec((rows, d), lambda i, j: (0, 0)),
            pl.BlockSpec((1, d, tn), lambda i, j: (i, 0, j)),
            pl.BlockSpec((1, 1, tn), lambda i, j: (i, 0, j)),
        ],
        out_specs=pl.BlockSpec((1, rows, tn), lambda i, j: (i, 0, j)),
        out_shape=jax.ShapeDtypeStruct((depth, rows, n), F32),
        compiler_params=_params(2),
        name="modulation",
    )(cond, ada_w, ada_b.reshape(depth, 1, n))


def _normed_rows(x_ref, xp_ref, xn_ref, mod_ref, ng_ref):
    shift, scale = mod_ref[0, 0, 0:1, :], mod_ref[0, 0, 1:2, :]
    xe = jnp.concatenate([xp_ref[0], x_ref[0], xn_ref[0]], axis=0)
    he = _mod_norm(xe, ng_ref[0], shift, scale)
    return he.astype(BF16), he[HALO:he.shape[0] - HALO].astype(BF16)


def _zero_outside_sequence(e, tile):
    i = pl.program_id(1)
    keep_prev = (i > 0).astype(F32)
    keep_next = (i < pl.num_programs(1) - 1).astype(F32)
    return jnp.concatenate(
        [e[:HALO] * keep_prev, e[HALO:HALO + tile], e[HALO + tile:] * keep_next], axis=0)


def _window_sum(e, width, tile):
    n = e.shape[0]
    half = width // 2
    f = e
    k = 1
    while k < half:
        f = f + pltpu.roll(f, n - k, axis=0)
        k *= 2
    if half == HALO:
        return f[:tile] + f[HALO:HALO + tile]
    return (pltpu.roll(f, half, axis=0) + f)[HALO:HALO + tile]


def _window_mean(ws, width, tile, seq_len):
    half = width // 2
    row = lax.broadcasted_iota(jnp.int32, (HALO, 1), 0)

    def inv_count(first_row):
        t = row + (pl.program_id(1) * tile + first_row)
        cnt = jnp.minimum(t + half, seq_len) - jnp.maximum(t - half, 0)
        return 1.0 / cnt.astype(F32)

    return jnp.concatenate([ws[:HALO] * inv_count(0),
                            ws[HALO:tile - HALO] * (1.0 / width),
                            ws[tile - HALO:] * inv_count(tile - HALO)], axis=0)


def _finish(x, y, gate, o_ref, fg_ref):
    out = x + gate * y
    if fg_ref is not None:
        ms = jnp.mean(out * out, axis=-1, keepdims=True)
        out = (out * lax.rsqrt(ms + EPS)) * fg_ref[...]
    o_ref[0] = out


def _cast_blocks(arr, n_inner, n_steps):
    layers, rows, cols = arr.shape
    if cols % (n_steps * LANES) == 0:
        shape = (layers, rows, cols // n_steps)
        index = lambda bi, i: (0, 0, bi * n_inner + i)
    else:
        assert rows % (n_steps * 16) == 0
        shape = (layers, rows // n_steps, cols)
        index = lambda bi, i: (0, bi * n_inner + i, 0)
    return (pl.BlockSpec(shape, index), pl.BlockSpec(shape, index),
            jax.ShapeDtypeStruct(arr.shape, BF16))


def _run_casts(cast_refs):
    n = len(cast_refs) // 2
    for src, dst in zip(cast_refs[:n], cast_refs[n:]):
        dst[...] = src[...].astype(BF16)


def _fold_pool_kernel(win_ref, wgrp_ref, ps_ref, wout_in_ref, wu_ref, wg_ref, wout_ref):
    for gi in range(len(POOL_WINDOWS)):
        cols = slice(gi * POOL_GROUP, (gi + 1) * POOL_GROUP)
        folded = jnp.dot(win_ref[0, :, cols], wgrp_ref[0, gi],
                         preferred_element_type=F32, precision=lax.Precision.HIGHEST)
        wu_ref[0, :, cols] = (folded * ps_ref[0, :, cols]).astype(BF16)
    wg_ref[0] = win_ref[0, :, WIDTH:].astype(BF16)
    wout_ref[0] = wout_in_ref[0].astype(BF16)


def _fold_pool_weights(w_in, w_grp, scale, w_out):
    n, d, _ = w_in.shape
    out = jax.ShapeDtypeStruct((n, d, WIDTH), BF16)
    return pl.pallas_call(
        _fold_pool_kernel,
        grid=(n,),
        in_specs=[pl.BlockSpec((1,) + w_in.shape[1:], lambda i: (i, 0, 0)),
                  pl.BlockSpec((1,) + w_grp.shape[1:], lambda i: (i, 0, 0, 0)),
                  pl.BlockSpec((1, 1, WIDTH), lambda i: (i, 0, 0)),
                  pl.BlockSpec((1,) + w_out.shape[1:], lambda i: (i, 0, 0))],
        out_specs=[pl.BlockSpec((1, d, WIDTH), lambda i: (i, 0, 0))] * 2
        + [pl.BlockSpec((1,) + w_out.shape[1:], lambda i: (i, 0, 0))],
        out_shape=[out, out, jax.ShapeDtypeStruct(w_out.shape, BF16)],
        compiler_params=_params(1),
        name="fold_pool_weights",
    )(w_in, w_grp, scale.reshape(n, 1, WIDTH), w_out)


def _split_rest(rest, n_casts):
    has_gain = len(rest) - 1 - 2 * n_casts
    fg_ref = rest[0] if has_gain else None
    cast_refs = rest[has_gain:has_gain + n_casts] + rest[has_gain + n_casts + 1:]
    return fg_ref, rest[has_gain + n_casts], cast_refs


def _pool_kernel(x_ref, xp_ref, xn_ref, mod_ref, ng_ref, wu_ref, wg_ref, wout_ref,
                 *rest, tile, seq_len, n_casts):
    fg_ref, o_ref, cast_refs = _split_rest(rest, n_casts)
    _run_casts(cast_refs)
    he, hm = _normed_rows(x_ref, xp_ref, xn_ref, mod_ref, ng_ref)
    t = jnp.dot(he, wu_ref[0], preferred_element_type=F32)
    t = _zero_outside_sequence(t, tile)
    g = jnp.dot(hm, wg_ref[0], preferred_element_type=F32)
    mixed = []
    for gi, width in enumerate(POOL_WINDOWS):
        te = t[:, gi * POOL_GROUP:(gi + 1) * POOL_GROUP]
        pooled = _window_mean(_window_sum(te, width, tile), width, tile, seq_len)
        mixed.append(pooled - te[HALO:HALO + tile])
    mixed = jnp.concatenate(mixed, axis=1)
    z = (mixed * _silu(g)).astype(BF16)
    y = jnp.dot(z, wout_ref[0], preferred_element_type=F32)
    _finish(x_ref[0], y, mod_ref[0, 0, 2:3, :], o_ref, fg_ref)


def _conv_kernel(x_ref, xp_ref, xn_ref, mod_ref, ng_ref, win_ref, dw_ref, db_ref, wout_ref,
                 *rest, tile, seq_len, n_casts):
    fg_ref, o_ref, cast_refs = _split_rest(rest, n_casts)
    _run_casts(cast_refs)
    he, hm = _normed_rows(x_ref, xp_ref, xn_ref, mod_ref, ng_ref)
    n = tile + 2 * HALO
    rows = slice(HALO, HALO + tile)
    y = None
    for ci in range(WIDTH // CONV_BLOCK):
        cols = slice(ci * CONV_BLOCK, (ci + 1) * CONV_BLOCK)

        def proj(h, part):
            w = win_ref[0, :, part * WIDTH + ci * CONV_BLOCK:part * WIDTH + (ci + 1) * CONV_BLOCK]
            return jnp.dot(h, w, preferred_element_type=F32)

        z = _zero_outside_sequence(proj(he, 1) * proj(he, 2), tile)
        conv = (dw_ref[0, 0:1, cols] * pltpu.roll(z, 1, axis=0)[rows]
                + dw_ref[0, 1:2, cols] * z[rows]
                + dw_ref[0, 2:3, cols] * pltpu.roll(z, n - 1, axis=0)[rows]
                + db_ref[0, :, cols])
        yc = (proj(hm, 0) * conv * _silu(proj(hm, 3))).astype(BF16)
        part = jnp.dot(yc, wout_ref[0, cols, :], preferred_element_type=F32)
        y = part if y is None else y + part
    _finish(x_ref[0], y, mod_ref[0, 0, 2:3, :], o_ref, fg_ref)


def _mixer_layer(kernel_fn, x, mods, layer, mod_row, norm_g, weights, j, final_g, tile, name,
                 casts=()):
    b, seq_len, d = x.shape
    tile = min(tile, seq_len)
    halo_blocks = seq_len // HALO
    per_tile = tile // HALO
    n_inner = seq_len // tile
    cast_blocks = [_cast_blocks(a, n_inner, b * n_inner) for a in casts]
    in_specs = [
        pl.BlockSpec((1, tile, d), lambda bi, i: (bi, i, 0)),
        pl.BlockSpec((1, HALO, d), lambda bi, i: (bi, jnp.maximum(i * per_tile - 1, 0), 0)),
        pl.BlockSpec((1, HALO, d),
                     lambda bi, i: (bi, jnp.minimum((i + 1) * per_tile, halo_blocks - 1), 0)),
        _mod_block(mods, layer, mod_row),
        _layer_block(norm_g, layer),
    ] + [_layer_block(w, j) for w in weights]
    args = [x, x, x, mods, norm_g] + list(weights)
    if final_g is not None:
        in_specs.append(pl.BlockSpec((1, d), lambda bi, i: (0, 0)))
        args.append(final_g.reshape(1, d))
    outs = pl.pallas_call(
        functools.partial(kernel_fn, tile=tile, seq_len=seq_len, n_casts=len(casts)),
        grid=(b, n_inner),
        in_specs=in_specs + [blk[0] for blk in cast_blocks],
        out_specs=[pl.BlockSpec((1, tile, d), lambda bi, i: (bi, i, 0))]
        + [blk[1] for blk in cast_blocks],
        out_shape=[jax.ShapeDtypeStruct(x.shape, F32)] + [blk[2] for blk in cast_blocks],
        compiler_params=_params(2),
        name=name,
    )(*args, *casts)
    return outs if casts else outs[0]


def _project_kernel(x_ref, mod_ref, ng_ref, *refs, scales, layouts):
    n = len(scales)
    n_casts = (len(refs) - 2 * n) // 2
    w_refs, o_refs = refs[:n], refs[n + n_casts:2 * n + n_casts]
    _run_casts(refs[n:n + n_casts] + refs[2 * n + n_casts:])
    shift, scale = mod_ref[0, 0, 0:1, :], mod_ref[0, 0, 1:2, :]
    h = _mod_norm(x_ref[0], ng_ref[0], shift, scale).astype(BF16)
    for w_ref, o_ref, s, layout in zip(w_refs, o_refs, scales, layouts):
        p = jnp.dot(h, w_ref[0], preferred_element_type=F32)
        if s != 1.0:
            p = p * s
        if layout == "flat":
            o_ref[0] = p.astype(o_ref.dtype)
            continue
        for hp in range(N_HEAD_PAIRS):
            block = p[:, hp * LANES:(hp + 1) * LANES]
            o_ref[0, hp] = (block.T if layout == "pairs_t" else block).astype(o_ref.dtype)


def _project(x, mods, layer, mod_row, norm_g, w, j, cols, layouts, scales, tile, name,
             casts=()):
    b, seq_len, d = x.shape
    tile = min(tile, seq_len)
    assert tile % LANES == 0
    specs = {
        "flat": (pl.BlockSpec((1, tile, WIDTH), lambda bi, i: (bi, i, 0)),
                 jax.ShapeDtypeStruct((b, seq_len, WIDTH), F32)),
        "pairs": (pl.BlockSpec((1, N_HEAD_PAIRS, tile, LANES), lambda bi, i: (bi, 0, i, 0)),
                  jax.ShapeDtypeStruct((b, N_HEAD_PAIRS, seq_len, LANES), BF16)),
        "pairs_t": (pl.BlockSpec((1, N_HEAD_PAIRS, LANES, tile), lambda bi, i: (bi, 0, 0, i)),
                    jax.ShapeDtypeStruct((b, N_HEAD_PAIRS, LANES, seq_len), BF16)),
    }
    n_inner = seq_len // tile
    cast_blocks = [_cast_blocks(a, n_inner, b * n_inner) for a in casts]
    out_specs = [specs[name_][0] for name_ in layouts] + [blk[1] for blk in cast_blocks]
    out_shape = [specs[name_][1] for name_ in layouts] + [blk[2] for blk in cast_blocks]
    return pl.pallas_call(
        functools.partial(_project_kernel, scales=scales, layouts=layouts),
        grid=(b, n_inner),
        in_specs=[
            pl.BlockSpec((1, tile, d), lambda bi, i: (bi, i, 0)),
            _mod_block(mods, layer, mod_row),
            _layer_block(norm_g, layer),
        ] + [_layer_block(w, j, col) for col in cols] + [blk[0] for blk in cast_blocks],
        out_specs=out_specs,
        out_shape=out_shape,
        compiler_params=_params(2),
        name=name,
    )(x, mods, norm_g, *([w] * len(cols)), *casts)


SLAB_TILES = SLAB_ROWS // 2


def _slab_base(pair_row0, n_rows):
    return jnp.clip(pair_row0 - WIN_ROWS // 2, 0, n_rows - SLAB_ROWS)


def _bias_tile_plan(n_rows):
    tiles, ids = [], []
    for r0 in range(0, n_rows, 2):
        base = int(np.clip(r0 - WIN_ROWS // 2, 0, n_rows - SLAB_ROWS))
        for j in range(SLAB_TILES):
            quad = []
            for key_row in (base + 2 * j, base + 2 * j + 1):
                for r in (r0, r0 + 1):
                    start = int(np.clip(r - WIN_ROWS // 2, 0, n_rows - WIN_ROWS))
                    assert base <= start and start + WIN_ROWS <= base + SLAB_ROWS
                    inside = start <= key_row < start + WIN_ROWS
                    quad.append(key_row - r + WIN_ROWS - 1 if inside else None)
            quad = tuple(quad)
            if quad not in tiles:
                tiles.append(quad)
            ids.append(tiles.index(quad))
    return tiles, np.asarray(ids, np.int32)


def _bias_tiles_kernel(w_ref, o_ref, *, tiles):
    k_col = lax.broadcasted_iota(jnp.int32, (GRID_W, LANES), 0)
    lane = lax.broadcasted_iota(jnp.int32, (GRID_W, LANES), 1)
    q_col = lane % GRID_W
    c_start = jnp.clip(q_col - WIN_COLS // 2, 0, GRID_W - WIN_COLS)
    inside = (k_col >= c_start) & (k_col < c_start + WIN_COLS)
    first_query_row = lane < GRID_W
    masked = jnp.full((GRID_W, LANES), MASK_VALUE, F32)

    def toeplitz(head, d, shift):
        if d is None:
            return masked
        row = jnp.broadcast_to(w_ref[head, d:d + 1, :], (GRID_W, LANES))
        return pltpu.roll(row, shift, axis=1, stride=1, stride_axis=0) * LOG2E

    for head in range(w_ref.shape[0]):
        for tile_id, quad in enumerate(tiles):
            halves = [jnp.where(inside,
                                jnp.where(first_query_row, toeplitz(head, quad[2 * kr], 0),
                                          toeplitz(head, quad[2 * kr + 1], GRID_W)),
                                masked) for kr in range(2)]
            o_ref[head, tile_id] = jnp.concatenate(halves, axis=0)


def _attention_bias(rpb, tiles):
    h, n_dr, n_dc = rpb.shape
    assert n_dr == 2 * WIN_ROWS - 1 and n_dc == 2 * WIN_COLS - 1
    rev = rpb[..., ::-1]
    w = jnp.concatenate([rev[..., WIN_COLS - 1:], jnp.zeros((h, n_dr, LANES - n_dc), F32),
                         rev[..., :WIN_COLS - 1]], axis=-1)
    w = jnp.pad(w, ((0, 0), (0, 2 * WIN_ROWS - n_dr), (0, 0)))
    return pl.pallas_call(
        functools.partial(_bias_tiles_kernel, tiles=tiles),
        grid=(h // BIAS_HEADS_PER_STEP,),
        in_specs=[pl.BlockSpec((BIAS_HEADS_PER_STEP, 2 * WIN_ROWS, LANES), lambda i: (i, 0, 0))],
        out_specs=pl.BlockSpec((BIAS_HEADS_PER_STEP, len(tiles), LANES, LANES),
                               lambda i: (i, 0, 0, 0)),
        out_shape=jax.ShapeDtypeStruct((h, len(tiles), LANES, LANES), F32),
        compiler_params=_params(1),
        name="bias_tiles",
    )(w)


def _attention_kernel(ids_ref, q_ref, k_ref, v_ref, kc_ref, vc_ref, g_ref, x_ref, mod_ref,
                      bias_ref, wout_ref, o_ref, s_scr, p_scr, r_scr, o_scr, *, n_rows):
    pair_tokens = 2 * GRID_W
    n_ctx = kc_ref.shape[2]
    slab_keys = SLAB_ROWS * GRID_W
    n_items = (ATT_ROWS // 2) * N_HEAD_PAIRS
    first_head = lax.broadcasted_iota(jnp.int32, (LANES, pair_tokens), 0) < HEAD_DIM

    def locate(item):
        pb, hp = item // N_HEAD_PAIRS, item % N_HEAD_PAIRS
        pair = pl.program_id(1) * (ATT_ROWS // 2) + pb
        k_off = pl.multiple_of(_slab_base(2 * pair, n_rows) * GRID_W, pair_tokens)
        return hp, pair, pb * pair_tokens, k_off

    def scores(item, slot):
        hp, _, q_off, k_off = locate(item)
        qt = q_ref[0, hp, :, pl.ds(q_off, pair_tokens)]
        zero = jnp.zeros_like(qt)
        q2 = jnp.concatenate([jnp.where(first_head, qt, zero),
                              jnp.where(first_head, zero, qt)], axis=1)
        s_scr[slot, :n_ctx, :] = jnp.dot(kc_ref[0, hp], q2, preferred_element_type=F32)
        s_scr[slot, n_ctx:, :] = jnp.dot(k_ref[0, hp, pl.ds(k_off, slab_keys), :], q2,
                                           preferred_element_type=F32)

    def softmax(item, slot):
        hp, pair, _, _ = locate(item)
        bias = jnp.concatenate(
            [jnp.concatenate([bias_ref[2 * hp + a, ids_ref[pair * SLAB_TILES + j]]
                              for a in range(2)], axis=1)
             for j in range(SLAB_TILES)], axis=0)
        s_ctx = s_scr[slot, :n_ctx, :]
        s_loc = s_scr[slot, n_ctx:, :] + bias
        m = jnp.maximum(jnp.max(s_loc, axis=0, keepdims=True),
                        jnp.max(s_ctx, axis=0, keepdims=True))
        p_ctx = jnp.exp2(s_ctx - m)
        p_loc = jnp.exp2(s_loc - m)
        denom = jnp.sum(p_loc, axis=0, keepdims=True) + jnp.sum(p_ctx, axis=0, keepdims=True)
        p_scr[slot, :n_ctx, :] = p_ctx.astype(BF16)
        p_scr[slot, n_ctx:, :] = p_loc.astype(BF16)
        r_scr[slot] = jnp.broadcast_to(1.0 / denom, r_scr.shape[1:])

    def values(item, slot):
        hp, _, q_off, k_off = locate(item)
        pv = (jnp.dot(vc_ref[0, hp], p_scr[slot, :n_ctx, :], preferred_element_type=F32)
              + jnp.dot(v_ref[0, hp, :, pl.ds(k_off, slab_keys)], p_scr[slot, n_ctx:, :],
                        preferred_element_type=F32))
        pv = pv * r_scr[slot, 0:1, :]
        out = jnp.where(first_head, pv[:, :pair_tokens], pv[:, pair_tokens:])
        o_scr[hp, pl.ds(q_off, pair_tokens), :] = out.T

    for t in range(-2, n_items):
        if t + 2 < n_items:
            scores(t + 2, (t + 2) % PIPE_SLOTS)
        if 0 <= t + 1 < n_items:
            softmax(t + 1, (t + 1) % PIPE_SLOTS)
        if t >= 0:
            values(t, t % PIPE_SLOTS)

    gate = mod_ref[0, 0, 2:3, :]
    o = jnp.concatenate([o_scr[hp] for hp in range(N_HEAD_PAIRS)], axis=1)
    y = (o * _silu(g_ref[0])).astype(BF16)
    y = jnp.dot(y, wout_ref[0], preferred_element_type=F32)
    o_ref[0] = x_ref[0] + gate * y


def _attention_layer(x, q, k, v, kc, vc, g, mods, layer, rpb, w_out, j):
    b, seq_len, d = x.shape
    n_ctx = kc.shape[2] // b
    n_rows = seq_len // GRID_W
    assert n_rows % ATT_ROWS == 0 and n_rows >= SLAB_ROWS and ATT_ROWS % 2 == 0
    tile = ATT_ROWS * GRID_W
    n_keys = n_ctx + SLAB_ROWS * GRID_W
    tiles, ids = _bias_tile_plan(n_rows)
    bias = _attention_bias(rpb, tiles)
    tile_spec = pl.BlockSpec((1, tile, d), lambda bi, i: (bi, i, 0))
    q_spec = pl.BlockSpec((1, N_HEAD_PAIRS, LANES, tile), lambda bi, i: (bi, 0, 0, i))
    seq_spec = pl.BlockSpec((1, N_HEAD_PAIRS, seq_len, LANES), lambda bi, i: (bi, 0, 0, 0))
    ctx_spec = pl.BlockSpec((1, N_HEAD_PAIRS, n_ctx, LANES), lambda bi, i: (0, 0, bi, 0))
    seq_t_spec = pl.BlockSpec((1, N_HEAD_PAIRS, LANES, seq_len), lambda bi, i: (bi, 0, 0, 0))
    ctx_t_spec = pl.BlockSpec((1, N_HEAD_PAIRS, LANES, n_ctx), lambda bi, i: (0, 0, 0, bi))
    return pl.pallas_call(
        functools.partial(_attention_kernel, n_rows=n_rows),
        grid=(b, n_rows // ATT_ROWS),
        in_specs=[pl.BlockSpec(memory_space=pltpu.SMEM),
                  q_spec, seq_spec, seq_t_spec, ctx_spec, ctx_t_spec, tile_spec, tile_spec,
                  _mod_block(mods, layer, None),
                  pl.BlockSpec(bias.shape, lambda bi, i: (0, 0, 0, 0),
                               pipeline_mode=pl.Buffered(1)),
                  _layer_block(w_out, j)],
        out_specs=tile_spec,
        out_shape=jax.ShapeDtypeStruct(x.shape, F32),
        scratch_shapes=[pltpu.VMEM((PIPE_SLOTS, n_keys, 4 * GRID_W), F32),
                        pltpu.VMEM((PIPE_SLOTS, n_keys, 4 * GRID_W), BF16),
                        pltpu.VMEM((PIPE_SLOTS, 8, 4 * GRID_W), F32),
                        pltpu.VMEM((N_HEAD_PAIRS, tile, LANES), F32)],
        compiler_params=_params(2),
        name="na_attention",
    )(jnp.asarray(ids), q, k, v, kc, vc, g, x, mods, bias, w_out)


def kernel(x, c, ctx, c_ctx, norm_g, ada_w, ada_b, pool_w_in, pool_w_grp, pool_scale, pool_w_out,
           na_w_in, na_rpb, na_w_out, conv_w_in, conv_dw, conv_db, conv_w_out, final_g):
    depth = norm_g.shape[0]
    batch, _, d = x.shape
    assert batch < COND_ROWS and WIN_ROWS // 2 <= HALO

    cond = jnp.zeros((COND_ROWS, d), F32).at[:batch].set(c).at[batch].set(c_ctx)
    mods = _modulation(cond, ada_w, ada_b).reshape(depth, COND_ROWS, 3, d)
    norm_g = norm_g.reshape(depth, 1, d)

    f32_weights = {1: (na_w_in, na_w_out), 2: (conv_w_in, conv_w_out)}
    bf16_weights = {0: _fold_pool_weights(pool_w_in, pool_w_grp, pool_scale, pool_w_out)}

    def weights_of(kind):
        if kind not in bf16_weights:
            bf16_weights[kind] = tuple(w.astype(BF16) for w in f32_weights[kind])
        return bf16_weights[kind]

    def casts_for_next(i):
        kind = (i + 1) % N_MIXERS
        return f32_weights[kind] if i + 1 < depth and kind not in bf16_weights else ()

    last_ctx_reader = max([i for i in range(depth) if i % N_MIXERS == 1], default=-1)
    for i in range(depth):
        kind, j = i % N_MIXERS, i // N_MIXERS
        update_ctx = i < last_ctx_reader
        fg = final_g if i == depth - 1 else None
        w_bf16 = weights_of(kind)
        casts = casts_for_next(i)
        if kind != 1:
            if kind == 0:
                kernel_fn, weights = _pool_kernel, w_bf16
            else:
                kernel_fn = _conv_kernel
                weights = (w_bf16[0], conv_dw, conv_db.reshape(-1, 1, WIDTH), w_bf16[1])
            run = functools.partial(_mixer_layer, kernel_fn, mods=mods, layer=i, norm_g=norm_g,
                                    weights=weights, j=j)
            if update_ctx:
                ctx = run(x=ctx, mod_row=batch, final_g=None, tile=CTX_TILE, name=f"ctx_layer{i}")
            x = run(x=x, mod_row=None, final_g=fg, tile=X_TILE, name=f"layer{i}", casts=casts)
            if casts:
                x, *cast_out = x
        else:
            if update_ctx:
                raise NotImplementedError("context output of a neighbourhood-attention layer")
            if fg is not None:
                raise NotImplementedError("final norm after a neighbourhood-attention layer")
            w_in, w_out = w_bf16
            q, k, v, g, *cast_out = _project(
                x, mods, i, None, norm_g, w_in, j, (0, 1, 2, 3),
                ("pairs_t", "pairs", "pairs_t", "flat"),
                (HEAD_DIM ** -0.5 * LOG2E, 1.0, 1.0, 1.0), X_TILE, f"na_project{i}", casts=casts)
            kc, vc = _project(ctx.reshape(1, -1, d), mods, i, batch, norm_g, w_in, j, (1, 2),
                              ("pairs", "pairs_t"), (1.0, 1.0), X_TILE, f"na_ctx_project{i}")
            x = _attention_layer(x, q, k, v, kc, vc, g, mods, i, na_rpb[j], w_out, j)
        if casts:
            bf16_weights[(i + 1) % N_MIXERS] = tuple(cast_out)
    return x
```

```python
import functools
import math

import numpy as np

import jax
import jax.numpy as jnp
from jax import lax
from jax.experimental import pallas as pl
from jax.experimental.pallas import tpu as pltpu

D_MODEL = 1024
WIDTH = D_MODEL
GRID_W = 64
N_MIXERS = 3
POOL_WINDOWS = (2, 4, 8, 16)
POOL_GROUP = WIDTH // len(POOL_WINDOWS)
HEAD_DIM = 64
N_HEADS = WIDTH // HEAD_DIM
WIN_ROWS = 8
WIN_COLS = 16
EPS = 1e-6

LANES = 128
SUBLANES = 8
BF16_SUBLANES = 16
MXU_TILE = 256
VMEM_LIMIT = 56 * 1024 * 1024

HALO = SUBLANES
N_HEAD_PAIRS = WIDTH // LANES
MASK_VALUE = -1e30
LOG2E = math.log2(math.e)

X_TILE = 1024
CTX_TILE = 256
ATT_ROWS = 8
SLAB_ROWS = WIN_ROWS + 2
PIPE_SLOTS = 2
BIAS_HEADS_PER_STEP = 8
CONV_BLOCK = MXU_TILE
COND_ROWS = BF16_SUBLANES

F32 = jnp.float32
BF16 = jnp.bfloat16


def _silu(x):
    return x / (1.0 + jnp.exp(-x))


def _mod_norm(x, norm_g, shift, scale):
    ms = jnp.mean(x * x, axis=-1, keepdims=True)
    return (x * lax.rsqrt(ms + EPS)) * (norm_g * (1.0 + scale)) + shift


def _params(n_axes):
    return pltpu.CompilerParams(
        dimension_semantics=("arbitrary",) * n_axes, vmem_limit_bytes=VMEM_LIMIT)


def _layer_block(arr, j, col=None):
    shape = (1,) + arr.shape[1:]
    index = (j,) + (0,) * (arr.ndim - 1)
    if col is not None:
        shape = shape[:-1] + (WIDTH,)
        index = index[:-1] + (col,)
    return pl.BlockSpec(shape, lambda *_: index, pipeline_mode=pl.Buffered(1))


def _mod_block(mods, layer, row):
    d = mods.shape[-1]
    if row is None:
        return pl.BlockSpec((1, 1, 3, d), lambda bi, i: (layer, bi, 0, 0))
    return pl.BlockSpec((1, 1, 3, d), lambda bi, i: (layer, row, 0, 0))


def _modulation_kernel(cond_ref, w_ref, b_ref, o_ref):
    s = _silu(cond_ref[...]).astype(BF16)
    o_ref[0] = jnp.dot(s, w_ref[0].astype(BF16), preferred_element_type=F32) + b_ref[0]


def _modulation(cond, ada_w, ada_b):
    depth, d, n = ada_w.shape
    rows = cond.shape[0]
    tn = n
    return pl.pallas_call(
        _modulation_kernel,
        grid=(depth, n // tn),
        in_specs=[
            pl.BlockSpec((rows, d), lambda i, j: (0, 0)),
            pl.BlockSpec((1, d, tn), lambda i, j: (i, 0, j)),
            pl.BlockSpec((1, 1, tn), lambda i, j: (i, 0, j)),
        ],
        out_specs=pl.BlockSpec((1, rows, tn), lambda i, j: (i, 0, j)),
        out_shape=jax.ShapeDtypeStruct((depth, rows, n), F32),
        compiler_params=_params(2),
        name="modulation",
    )(cond, ada_w, ada_b.reshape(depth, 1, n))


def _normed_rows(x_ref, xp_ref, xn_ref, mod_ref, ng_ref):
    shift, scale = mod_ref[0, 0, 0:1, :], mod_ref[0, 0, 1:2, :]
    xe = jnp.concatenate([xp_ref[0], x_ref[0], xn_ref[0]], axis=0)
    he = _mod_norm(xe, ng_ref[0], shift, scale)
    return he.astype(BF16), he[HALO:he.shape[0] - HALO].astype(BF16)


def _zero_outside_sequence(e, tile):
    i = pl.program_id(1)
    keep_prev = (i > 0).astype(F32)
    keep_next = (i < pl.num_programs(1) - 1).astype(F32)
    return jnp.concatenate(
        [e[:HALO] * keep_prev, e[HALO:HALO + tile], e[HALO + tile:] * keep_next], axis=0)


def _window_sum(e, width, tile):
    n = e.shape[0]
    half = width // 2
    f = e
    k = 1
    while k < half:
        f = f + pltpu.roll(f, n - k, axis=0)
        k *= 2
    if half == HALO:
        return f[:tile] + f[HALO:HALO + tile]
    return (pltpu.roll(f, half, axis=0) + f)[HALO:HALO + tile]


def _window_mean(ws, width, tile, seq_len):
    half = width // 2
    row = lax.broadcasted_iota(jnp.int32, (HALO, 1), 0)

    def inv_count(first_row):
        t = row + (pl.program_id(1) * tile + first_row)
        cnt = jnp.minimum(t + half, seq_len) - jnp.maximum(t - half, 0)
        return 1.0 / cnt.astype(F32)

    return jnp.concatenate([ws[:HALO] * inv_count(0),
                            ws[HALO:tile - HALO] * (1.0 / width),
                            ws[tile - HALO:] * inv_count(tile - HALO)], axis=0)


def _finish(x, y, gate, o_ref, fg_ref):
    out = x + gate * y
    if fg_ref is not None:
        ms = jnp.mean(out * out, axis=-1, keepdims=True)
        out = (out * lax.rsqrt(ms + EPS)) * fg_ref[...]
    o_ref[0] = out


def _cast_blocks(arr, n_inner, n_steps):
    layers, rows, cols = arr.shape
    if cols % (n_steps * LANES) == 0:
        shape = (layers, rows, cols // n_steps)
        index = lambda bi, i: (0, 0, bi * n_inner + i)
    else:
        assert rows % (n_steps * BF16_SUBLANES) == 0
        shape = (layers, rows // n_steps, cols)
        index = lambda bi, i: (0, bi * n_inner + i, 0)
    return (pl.BlockSpec(shape, index), pl.BlockSpec(shape, index),
            jax.ShapeDtypeStruct(arr.shape, BF16))


def _run_casts(cast_refs):
    n = len(cast_refs) // 2
    for src, dst in zip(cast_refs[:n], cast_refs[n:]):
        dst[...] = src[...].astype(BF16)


def _fold_pool_kernel(win_ref, wgrp_ref, ps_ref, wout_in_ref, wu_ref, wg_ref, wout_ref):
    for gi in range(len(POOL_WINDOWS)):
        cols = slice(gi * POOL_GROUP, (gi + 1) * POOL_GROUP)
        folded = jnp.dot(win_ref[0, :, cols], wgrp_ref[0, gi],
                         preferred_element_type=F32, precision=lax.Precision.HIGHEST)
        wu_ref[0, :, cols] = (folded * ps_ref[0, :, cols]).astype(BF16)
    wg_ref[0] = win_ref[0, :, WIDTH:].astype(BF16)
    wout_ref[0] = wout_in_ref[0].astype(BF16)


def _fold_pool_weights(w_in, w_grp, scale, w_out):
    n, d, _ = w_in.shape
    out = jax.ShapeDtypeStruct((n, d, WIDTH), BF16)
    return pl.pallas_call(
        _fold_pool_kernel,
        grid=(n,),
        in_specs=[pl.BlockSpec((1,) + w_in.shape[1:], lambda i: (i, 0, 0)),
                  pl.BlockSpec((1,) + w_grp.shape[1:], lambda i: (i, 0, 0, 0)),
                  pl.BlockSpec((1, 1, WIDTH), lambda i: (i, 0, 0)),
                  pl.BlockSpec((1,) + w_out.shape[1:], lambda i: (i, 0, 0))],
        out_specs=[pl.BlockSpec((1, d, WIDTH), lambda i: (i, 0, 0))] * 2
        + [pl.BlockSpec((1,) + w_out.shape[1:], lambda i: (i, 0, 0))],
        out_shape=[out, out, jax.ShapeDtypeStruct(w_out.shape, BF16)],
        compiler_params=_params(1),
        name="fold_pool_weights",
    )(w_in, w_grp, scale.reshape(n, 1, WIDTH), w_out)


def _split_rest(rest, n_casts):
    has_gain = len(rest) - 1 - 2 * n_casts
    fg_ref = rest[0] if has_gain else None
    cast_refs = rest[has_gain:has_gain + n_casts] + rest[has_gain + n_casts + 1:]
    return fg_ref, rest[has_gain + n_casts], cast_refs


def _pool_kernel(x_ref, xp_ref, xn_ref, mod_ref, ng_ref, wu_ref, wg_ref, wout_ref,
                 *rest, tile, seq_len, n_casts):
    fg_ref, o_ref, cast_refs = _split_rest(rest, n_casts)
    _run_casts(cast_refs)
    he, hm = _normed_rows(x_ref, xp_ref, xn_ref, mod_ref, ng_ref)
    t = jnp.dot(he, wu_ref[0], preferred_element_type=F32)
    t = _zero_outside_sequence(t, tile)
    g = jnp.dot(hm, wg_ref[0], preferred_element_type=F32)
    mixed = []
    for gi, width in enumerate(POOL_WINDOWS):
        te = t[:, gi * POOL_GROUP:(gi + 1) * POOL_GROUP]
        pooled = _window_mean(_window_sum(te, width, tile), width, tile, seq_len)
        mixed.append(pooled - te[HALO:HALO + tile])
    mixed = jnp.concatenate(mixed, axis=1)
    z = (mixed * _silu(g)).astype(BF16)
    y = jnp.dot(z, wout_ref[0], preferred_element_type=F32)
    _finish(x_ref[0], y, mod_ref[0, 0, 2:3, :], o_ref, fg_ref)


def _conv_kernel(x_ref, xp_ref, xn_ref, mod_ref, ng_ref, win_ref, dw_ref, db_ref, wout_ref,
                 *rest, tile, seq_len, n_casts):
    fg_ref, o_ref, cast_refs = _split_rest(rest, n_casts)
    _run_casts(cast_refs)
    he, hm = _normed_rows(x_ref, xp_ref, xn_ref, mod_ref, ng_ref)
    n = tile + 2 * HALO
    rows = slice(HALO, HALO + tile)
    y = None
    for ci in range(WIDTH // CONV_BLOCK):
        cols = slice(ci * CONV_BLOCK, (ci + 1) * CONV_BLOCK)

        def proj(h, part):
            w = win_ref[0, :, part * WIDTH + ci * CONV_BLOCK:part * WIDTH + (ci + 1) * CONV_BLOCK]
            return jnp.dot(h, w, preferred_element_type=F32)

        z = _zero_outside_sequence(proj(he, 1) * proj(he, 2), tile)
        conv = (dw_ref[0, 0:1, cols] * pltpu.roll(z, 1, axis=0)[rows]
                + dw_ref[0, 1:2, cols] * z[rows]
                + dw_ref[0, 2:3, cols] * pltpu.roll(z, n - 1, axis=0)[rows]
                + db_ref[0, :, cols])
        yc = (proj(hm, 0) * conv * _silu(proj(hm, 3))).astype(BF16)
        part = jnp.dot(yc, wout_ref[0, cols, :], preferred_element_type=F32)
        y = part if y is None else y + part
    _finish(x_ref[0], y, mod_ref[0, 0, 2:3, :], o_ref, fg_ref)


def _mixer_layer(kernel_fn, x, mods, layer, mod_row, norm_g, weights, j, final_g, tile, name,
                 casts=()):
    b, seq_len, d = x.shape
    tile = min(tile, seq_len)
    halo_blocks = seq_len // HALO
    per_tile = tile // HALO
    n_inner = seq_len // tile
    cast_blocks = [_cast_blocks(a, n_inner, b * n_inner) for a in casts]
    in_specs = [
        pl.BlockSpec((1, tile, d), lambda bi, i: (bi, i, 0)),
        pl.BlockSpec((1, HALO, d), lambda bi, i: (bi, jnp.maximum(i * per_tile - 1, 0), 0)),
        pl.BlockSpec((1, HALO, d),
                     lambda bi, i: (bi, jnp.minimum((i + 1) * per_tile, halo_blocks - 1), 0)),
        _mod_block(mods, layer, mod_row),
        _layer_block(norm_g, layer),
    ] + [_layer_block(w, j) for w in weights]
    args = [x, x, x, mods, norm_g] + list(weights)
    if final_g is not None:
        in_specs.append(pl.BlockSpec((1, d), lambda bi, i: (0, 0)))
        args.append(final_g.reshape(1, d))
    outs = pl.pallas_call(
        functools.partial(kernel_fn, tile=tile, seq_len=seq_len, n_casts=len(casts)),
        grid=(b, n_inner),
        in_specs=in_specs + [blk[0] for blk in cast_blocks],
        out_specs=[pl.BlockSpec((1, tile, d), lambda bi, i: (bi, i, 0))]
        + [blk[1] for blk in cast_blocks],
        out_shape=[jax.ShapeDtypeStruct(x.shape, F32)] + [blk[2] for blk in cast_blocks],
        compiler_params=_params(2),
        name=name,
    )(*args, *casts)
    return outs if casts else outs[0]


def _project_kernel(x_ref, mod_ref, ng_ref, *refs, scales, layouts):
    n = len(scales)
    n_casts = (len(refs) - 2 * n) // 2
    w_refs, o_refs = refs[:n], refs[n + n_casts:2 * n + n_casts]
    _run_casts(refs[n:n + n_casts] + refs[2 * n + n_casts:])
    shift, scale = mod_ref[0, 0, 0:1, :], mod_ref[0, 0, 1:2, :]
    h = _mod_norm(x_ref[0], ng_ref[0], shift, scale).astype(BF16)
    for w_ref, o_ref, s, layout in zip(w_refs, o_refs, scales, layouts):
        p = jnp.dot(h, w_ref[0], preferred_element_type=F32)
        if s != 1.0:
            p = p * s
        if layout == "flat":
            o_ref[0] = p.astype(o_ref.dtype)
            continue
        for hp in range(N_HEAD_PAIRS):
            block = p[:, hp * LANES:(hp + 1) * LANES]
            o_ref[0, hp] = (block.T if layout == "pairs_t" else block).astype(o_ref.dtype)


def _project(x, mods, layer, mod_row, norm_g, w, j, cols, layouts, scales, tile, name,
             casts=()):
    b, seq_len, d = x.shape
    tile = min(tile, seq_len)
    assert tile % LANES == 0
    specs = {
        "flat": (pl.BlockSpec((1, tile, WIDTH), lambda bi, i: (bi, i, 0)),
                 jax.ShapeDtypeStruct((b, seq_len, WIDTH), F32)),
        "pairs": (pl.BlockSpec((1, N_HEAD_PAIRS, tile, LANES), lambda bi, i: (bi, 0, i, 0)),
                  jax.ShapeDtypeStruct((b, N_HEAD_PAIRS, seq_len, LANES), BF16)),
        "pairs_t": (pl.BlockSpec((1, N_HEAD_PAIRS, LANES, tile), lambda bi, i: (bi, 0, 0, i)),
                    jax.ShapeDtypeStruct((b, N_HEAD_PAIRS, LANES, seq_len), BF16)),
    }
    n_inner = seq_len // tile
    cast_blocks = [_cast_blocks(a, n_inner, b * n_inner) for a in casts]
    out_specs = [specs[name_][0] for name_ in layouts] + [blk[1] for blk in cast_blocks]
    out_shape = [specs[name_][1] for name_ in layouts] + [blk[2] for blk in cast_blocks]
    return pl.pallas_call(
        functools.partial(_project_kernel, scales=scales, layouts=layouts),
        grid=(b, n_inner),
        in_specs=[
            pl.BlockSpec((1, tile, d), lambda bi, i: (bi, i, 0)),
            _mod_block(mods, layer, mod_row),
            _layer_block(norm_g, layer),
        ] + [_layer_block(w, j, col) for col in cols] + [blk[0] for blk in cast_blocks],
        out_specs=out_specs,
        out_shape=out_shape,
        compiler_params=_params(2),
        name=name,
    )(x, mods, norm_g, *([w] * len(cols)), *casts)


SLAB_TILES = SLAB_ROWS // 2


def _slab_base(pair_row0, n_rows):
    return jnp.clip(pair_row0 - WIN_ROWS // 2, 0, n_rows - SLAB_ROWS)


def _bias_tile_plan(n_rows):
    tiles, ids = [], []
    for r0 in range(0, n_rows, 2):
        base = int(np.clip(r0 - WIN_ROWS // 2, 0, n_rows - SLAB_ROWS))
        for j in range(SLAB_TILES):
            quad = []
            for key_row in (base + 2 * j, base + 2 * j + 1):
                for r in (r0, r0 + 1):
                    start = int(np.clip(r - WIN_ROWS // 2, 0, n_rows - WIN_ROWS))
                    assert base <= start and start + WIN_ROWS <= base + SLAB_ROWS
                    inside = start <= key_row < start + WIN_ROWS
                    quad.append(key_row - r + WIN_ROWS - 1 if inside else None)
            quad = tuple(quad)
            if quad not in tiles:
                tiles.append(quad)
            ids.append(tiles.index(quad))
    return tiles, np.asarray(ids, np.int32)


def _bias_tiles_kernel(w_ref, o_ref, *, tiles):
    k_col = lax.broadcasted_iota(jnp.int32, (GRID_W, LANES), 0)
    lane = lax.broadcasted_iota(jnp.int32, (GRID_W, LANES), 1)
    q_col = lane % GRID_W
    c_start = jnp.clip(q_col - WIN_COLS // 2, 0, GRID_W - WIN_COLS)
    inside = (k_col >= c_start) & (k_col < c_start + WIN_COLS)
    first_query_row = lane < GRID_W
    masked = jnp.full((GRID_W, LANES), MASK_VALUE, F32)

    def toeplitz(head, d, shift):
        if d is None:
            return masked
        row = jnp.broadcast_to(w_ref[head, d:d + 1, :], (GRID_W, LANES))
        return pltpu.roll(row, shift, axis=1, stride=1, stride_axis=0) * LOG2E

    for head in range(w_ref.shape[0]):
        for tile_id, quad in enumerate(tiles):
            halves = [jnp.where(inside,
                                jnp.where(first_query_row, toeplitz(head, quad[2 * kr], 0),
                                          toeplitz(head, quad[2 * kr + 1], GRID_W)),
                                masked) for kr in range(2)]
            o_ref[head, tile_id] = jnp.concatenate(halves, axis=0)


def _attention_bias(rpb, tiles):
    h, n_dr, n_dc = rpb.shape
    assert n_dr == 2 * WIN_ROWS - 1 and n_dc == 2 * WIN_COLS - 1
    rev = rpb[..., ::-1]
    w = jnp.concatenate([rev[..., WIN_COLS - 1:], jnp.zeros((h, n_dr, LANES - n_dc), F32),
                         rev[..., :WIN_COLS - 1]], axis=-1)
    w = jnp.pad(w, ((0, 0), (0, 2 * WIN_ROWS - n_dr), (0, 0)))
    return pl.pallas_call(
        functools.partial(_bias_tiles_kernel, tiles=tiles),
        grid=(h // BIAS_HEADS_PER_STEP,),
        in_specs=[pl.BlockSpec((BIAS_HEADS_PER_STEP, 2 * WIN_ROWS, LANES), lambda i: (i, 0, 0))],
        out_specs=pl.BlockSpec((BIAS_HEADS_PER_STEP, len(tiles), LANES, LANES),
                               lambda i: (i, 0, 0, 0)),
        out_shape=jax.ShapeDtypeStruct((h, len(tiles), LANES, LANES), F32),
        compiler_params=_params(1),
        name="bias_tiles",
    )(w)


def _attention_kernel(ids_ref, q_ref, k_ref, v_ref, kc_ref, vc_ref, g_ref, x_ref, mod_ref,
                      bias_ref, wout_ref, o_ref, s_scr, p_scr, r_scr, o_scr, *, n_rows):
    pair_tokens = 2 * GRID_W
    n_ctx = kc_ref.shape[2]
    slab_keys = SLAB_ROWS * GRID_W
    n_items = (ATT_ROWS // 2) * N_HEAD_PAIRS
    first_head = lax.broadcasted_iota(jnp.int32, (LANES, pair_tokens), 0) < HEAD_DIM

    def locate(item):
        pb, hp = item // N_HEAD_PAIRS, item % N_HEAD_PAIRS
        pair = pl.program_id(1) * (ATT_ROWS // 2) + pb
        k_off = pl.multiple_of(_slab_base(2 * pair, n_rows) * GRID_W, pair_tokens)
        return hp, pair, pb * pair_tokens, k_off

    def scores(item, slot):
        hp, _, q_off, k_off = locate(item)
        qt = q_ref[0, hp, :, pl.ds(q_off, pair_tokens)]
        zero = jnp.zeros_like(qt)
        q2 = jnp.concatenate([jnp.where(first_head, qt, zero),
                              jnp.where(first_head, zero, qt)], axis=1)
        s_scr[slot, :n_ctx, :] = jnp.dot(kc_ref[0, hp], q2, preferred_element_type=F32)
        s_scr[slot, n_ctx:, :] = jnp.dot(k_ref[0, hp, pl.ds(k_off, slab_keys), :], q2,
                                         preferred_element_type=F32)

    def softmax(item, slot):
        hp, pair, _, _ = locate(item)
        bias = jnp.concatenate(
            [jnp.concatenate([bias_ref[2 * hp + a, ids_ref[pair * SLAB_TILES + j]]
                              for a in range(2)], axis=1)
             for j in range(SLAB_TILES)], axis=0)
        s_ctx = s_scr[slot, :n_ctx, :]
        s_loc = s_scr[slot, n_ctx:, :] + bias
        m = jnp.maximum(jnp.max(s_loc, axis=0, keepdims=True),
                        jnp.max(s_ctx, axis=0, keepdims=True))
        p_ctx = jnp.exp2(s_ctx - m)
        p_loc = jnp.exp2(s_loc - m)
        denom = jnp.sum(p_loc, axis=0, keepdims=True) + jnp.sum(p_ctx, axis=0, keepdims=True)
        p_scr[slot, :n_ctx, :] = p_ctx.astype(BF16)
        p_scr[slot, n_ctx:, :] = p_loc.astype(BF16)
        r_scr[slot] = jnp.broadcast_to(1.0 / denom, r_scr.shape[1:])

    def values(item, slot):
        hp, _, q_off, k_off = locate(item)
        pv = (jnp.dot(vc_ref[0, hp], p_scr[slot, :n_ctx, :], preferred_element_type=F32)
              + jnp.dot(v_ref[0, hp, :, pl.ds(k_off, slab_keys)], p_scr[slot, n_ctx:, :],
                        preferred_element_type=F32))
        pv = pv * r_scr[slot, 0:1, :]
        out = jnp.where(first_head, pv[:, :pair_tokens], pv[:, pair_tokens:])
        o_scr[hp, pl.ds(q_off, pair_tokens), :] = out.T

    for t in range(-2, n_items):
        if t + 2 < n_items:
            scores(t + 2, (t + 2) % PIPE_SLOTS)
        if 0 <= t + 1 < n_items:
            softmax(t + 1, (t + 1) % PIPE_SLOTS)
        if t >= 0:
            values(t, t % PIPE_SLOTS)

    gate = mod_ref[0, 0, 2:3, :]
    o = jnp.concatenate([o_scr[hp] for hp in range(N_HEAD_PAIRS)], axis=1)
    y = (o * _silu(g_ref[0])).astype(BF16)
    y = jnp.dot(y, wout_ref[0], preferred_element_type=F32)
    o_ref[0] = x_ref[0] + gate * y


def _attention_layer(x, q, k, v, kc, vc, g, mods, layer, rpb, w_out, j):
    b, seq_len, d = x.shape
    n_ctx = kc.shape[2] // b
    n_rows = seq_len // GRID_W
    assert n_rows % ATT_ROWS == 0 and n_rows >= SLAB_ROWS and ATT_ROWS % 2 == 0
    tile = ATT_ROWS * GRID_W
    n_keys = n_ctx + SLAB_ROWS * GRID_W
    tiles, ids = _bias_tile_plan(n_rows)
    bias = _attention_bias(rpb, tiles)
    tile_spec = pl.BlockSpec((1, tile, d), lambda bi, i: (bi, i, 0))
    q_spec = pl.BlockSpec((1, N_HEAD_PAIRS, LANES, tile), lambda bi, i: (bi, 0, 0, i))
    seq_spec = pl.BlockSpec((1, N_HEAD_PAIRS, seq_len, LANES), lambda bi, i: (bi, 0, 0, 0))
    ctx_spec = pl.BlockSpec((1, N_HEAD_PAIRS, n_ctx, LANES), lambda bi, i: (0, 0, bi, 0))
    seq_t_spec = pl.BlockSpec((1, N_HEAD_PAIRS, LANES, seq_len), lambda bi, i: (bi, 0, 0, 0))
    ctx_t_spec = pl.BlockSpec((1, N_HEAD_PAIRS, LANES, n_ctx), lambda bi, i: (0, 0, 0, bi))
    return pl.pallas_call(
        functools.partial(_attention_kernel, n_rows=n_rows),
        grid=(b, n_rows // ATT_ROWS),
        in_specs=[pl.BlockSpec(memory_space=pltpu.SMEM),
                  q_spec, seq_spec, seq_t_spec, ctx_spec, ctx_t_spec, tile_spec, tile_spec,
                  _mod_block(mods, layer, None),
                  pl.BlockSpec(bias.shape, lambda bi, i: (0, 0, 0, 0),
                               pipeline_mode=pl.Buffered(1)),
                  _layer_block(w_out, j)],
        out_specs=tile_spec,
        out_shape=jax.ShapeDtypeStruct(x.shape, F32),
        scratch_shapes=[pltpu.VMEM((PIPE_SLOTS, n_keys, 4 * GRID_W), F32),
                        pltpu.VMEM((PIPE_SLOTS, n_keys, 4 * GRID_W), BF16),
                        pltpu.VMEM((PIPE_SLOTS, SUBLANES, 4 * GRID_W), F32),
                        pltpu.VMEM((N_HEAD_PAIRS, tile, LANES), F32)],
        compiler_params=_params(2),
        name="na_attention",
    )(jnp.asarray(ids), q, k, v, kc, vc, g, x, mods, bias, w_out)


def kernel(x, c, ctx, c_ctx, norm_g, ada_w, ada_b, pool_w_in, pool_w_grp, pool_scale, pool_w_out,
           na_w_in, na_rpb, na_w_out, conv_w_in, conv_dw, conv_db, conv_w_out, final_g):
    depth = norm_g.shape[0]
    batch, _, d = x.shape
    assert batch < COND_ROWS and WIN_ROWS // 2 <= HALO

    cond = jnp.zeros((COND_ROWS, d), F32).at[:batch].set(c).at[batch].set(c_ctx)
    mods = _modulation(cond, ada_w, ada_b).reshape(depth, COND_ROWS, 3, d)
    norm_g = norm_g.reshape(depth, 1, d)

    f32_weights = {1: (na_w_in, na_w_out), 2: (conv_w_in, conv_w_out)}
    bf16_weights = {0: _fold_pool_weights(pool_w_in, pool_w_grp, pool_scale, pool_w_out)}

    def weights_of(kind):
        if kind not in bf16_weights:
            bf16_weights[kind] = tuple(w.astype(BF16) for w in f32_weights[kind])
        return bf16_weights[kind]

    def casts_for_next(i):
        kind = (i + 1) % N_MIXERS
        return f32_weights[kind] if i + 1 < depth and kind not in bf16_weights else ()

    last_ctx_reader = max([i for i in range(depth) if i % N_MIXERS == 1], default=-1)
    for i in range(depth):
        kind, j = i % N_MIXERS, i // N_MIXERS
        update_ctx = i < last_ctx_reader
        fg = final_g if i == depth - 1 else None
        w_bf16 = weights_of(kind)
        casts = casts_for_next(i)
        if kind != 1:
            if kind == 0:
                kernel_fn, weights = _pool_kernel, w_bf16
            else:
                kernel_fn = _conv_kernel
                weights = (w_bf16[0], conv_dw, conv_db.reshape(-1, 1, WIDTH), w_bf16[1])
            run = functools.partial(_mixer_layer, kernel_fn, mods=mods, layer=i, norm_g=norm_g,
                                    weights=weights, j=j)
            if update_ctx:
                ctx = run(x=ctx, mod_row=batch, final_g=None, tile=CTX_TILE, name=f"ctx_layer{i}")
            x = run(x=x, mod_row=None, final_g=fg, tile=X_TILE, name=f"layer{i}", casts=casts)
            if casts:
                x, *cast_out = x
        else:
            if update_ctx:
                raise NotImplementedError("context output of a neighbourhood-attention layer")
            if fg is not None:
                raise NotImplementedError("final norm after a neighbourhood-attention layer")
            w_in, w_out = w_bf16
            q, k, v, g, *cast_out = _project(
                x, mods, i, None, norm_g, w_in, j, (0, 1, 2, 3),
                ("pairs_t", "pairs", "pairs_t", "flat"),
                (HEAD_DIM ** -0.5 * LOG2E, 1.0, 1.0, 1.0), X_TILE, f"na_project{i}", casts=casts)
            kc, vc = _project(ctx.reshape(1, -1, d), mods, i, batch, norm_g, w_in, j, (1, 2),
                              ("pairs", "pairs_t"), (1.0, 1.0), X_TILE, f"na_ctx_project{i}")
            x = _attention_layer(x, q, k, v, kc, vc, g, mods, i, na_rpb[j], w_out, j)
        if casts:
            bf16_weights[(i + 1) % N_MIXERS] = tuple(cast_out)
    return x
```

```python
import functools
import math

import numpy as np

import jax
import jax.numpy as jnp
from jax import lax
from jax.experimental import pallas as pl
from jax.experimental.pallas import tpu as pltpu

D_MODEL = 1024
WIDTH = D_MODEL
GRID_W = 64
N_MIXERS = 3
POOL_WINDOWS = (2, 4, 8, 16)
POOL_GROUP = WIDTH // len(POOL_WINDOWS)
HEAD_DIM = 64
N_HEADS = WIDTH // HEAD_DIM
WIN_ROWS = 8
WIN_COLS = 16
EPS = 1e-6

LANES = 128
SUBLANES = 8
BF16_SUBLANES = 16
MXU_TILE = 256
VMEM_LIMIT = 56 * 1024 * 1024

HALO = SUBLANES
N_HEAD_PAIRS = WIDTH // LANES
MASK_VALUE = -1e30
LOG2E = math.log2(math.e)

X_TILE = 1024
CTX_TILE = 256
ATT_ROWS = 8
SLAB_ROWS = WIN_ROWS + 2
PIPE_SLOTS = 2
CONV_BLOCK = MXU_TILE
COND_ROWS = BF16_SUBLANES

F32 = jnp.float32
BF16 = jnp.bfloat16


def _silu(x):
    return x / (1.0 + jnp.exp(-x))


def _mod_norm(x, norm_g, shift, scale):
    ms = jnp.mean(x * x, axis=-1, keepdims=True)
    return (x * lax.rsqrt(ms + EPS)) * (norm_g * (1.0 + scale)) + shift


def _params(n_axes):
    return pltpu.CompilerParams(
        dimension_semantics=("arbitrary",) * n_axes, vmem_limit_bytes=VMEM_LIMIT)


def _layer_block(arr, j, col=None):
    shape = (1,) + arr.shape[1:]
    index = (j,) + (0,) * (arr.ndim - 1)
    if col is not None:
        shape = shape[:-1] + (WIDTH,)
        index = index[:-1] + (col,)
    return pl.BlockSpec(shape, lambda *_: index, pipeline_mode=pl.Buffered(1))


def _mod_block(mods, layer, row):
    d = mods.shape[-1]
    if row is None:
        return pl.BlockSpec((1, 1, 3, d), lambda bi, i: (layer, bi, 0, 0))
    return pl.BlockSpec((1, 1, 3, d), lambda bi, i: (layer, row, 0, 0))


def _modulation_kernel(cond_ref, w_ref, b_ref, o_ref):
    s = _silu(cond_ref[...]).astype(BF16)
    o_ref[0] = jnp.dot(s, w_ref[0].astype(BF16), preferred_element_type=F32) + b_ref[0]


def _modulation(cond, ada_w, ada_b):
    depth, d, n = ada_w.shape
    rows = cond.shape[0]
    tn = n
    return pl.pallas_call(
        _modulation_kernel,
        grid=(depth, n // tn),
        in_specs=[
            pl.BlockSpec((rows, d), lambda i, j: (0, 0)),
            pl.BlockSpec((1, d, tn), lambda i, j: (i, 0, j)),
            pl.BlockSpec((1, 1, tn), lambda i, j: (i, 0, j)),
        ],
        out_specs=pl.BlockSpec((1, rows, tn), lambda i, j: (i, 0, j)),
        out_shape=jax.ShapeDtypeStruct((depth, rows, n), F32),
        compiler_params=_params(2),
        name="modulation",
    )(cond, ada_w, ada_b.reshape(depth, 1, n))


def _normed_rows(x_ref, xp_ref, xn_ref, mod_ref, ng_ref):
    shift, scale = mod_ref[0, 0, 0:1, :], mod_ref[0, 0, 1:2, :]
    xe = jnp.concatenate([xp_ref[0], x_ref[0], xn_ref[0]], axis=0)
    he = _mod_norm(xe, ng_ref[0], shift, scale)
    return he.astype(BF16), he[HALO:he.shape[0] - HALO].astype(BF16)


def _zero_outside_sequence(e, tile):
    i = pl.program_id(1)
    keep_prev = (i > 0).astype(F32)
    keep_next = (i < pl.num_programs(1) - 1).astype(F32)
    return jnp.concatenate(
        [e[:HALO] * keep_prev, e[HALO:HALO + tile], e[HALO + tile:] * keep_next], axis=0)


def _window_sum(e, width, tile):
    n = e.shape[0]
    half = width // 2
    f = e
    k = 1
    while k < half:
        f = f + pltpu.roll(f, n - k, axis=0)
        k *= 2
    if half == HALO:
        return f[:tile] + f[HALO:HALO + tile]
    return (pltpu.roll(f, half, axis=0) + f)[HALO:HALO + tile]


def _window_mean(ws, width, tile, seq_len):
    half = width // 2
    row = lax.broadcasted_iota(jnp.int32, (HALO, 1), 0)

    def inv_count(first_row):
        t = row + (pl.program_id(1) * tile + first_row)
        cnt = jnp.minimum(t + half, seq_len) - jnp.maximum(t - half, 0)
        return 1.0 / cnt.astype(F32)

    return jnp.concatenate([ws[:HALO] * inv_count(0),
                            ws[HALO:tile - HALO] * (1.0 / width),
                            ws[tile - HALO:] * inv_count(tile - HALO)], axis=0)


def _finish(x, y, gate, o_ref, fg_ref):
    out = x + gate * y
    if fg_ref is not None:
        ms = jnp.mean(out * out, axis=-1, keepdims=True)
        out = (out * lax.rsqrt(ms + EPS)) * fg_ref[...]
    o_ref[0] = out


def _cast_blocks(arr, n_inner, n_steps):
    layers, rows, cols = arr.shape
    if cols % (n_steps * LANES) == 0:
        shape = (layers, rows, cols // n_steps)
        index = lambda bi, i: (0, 0, bi * n_inner + i)
    else:
        assert rows % (n_steps * BF16_SUBLANES) == 0
        shape = (layers, rows // n_steps, cols)
        index = lambda bi, i: (0, bi * n_inner + i, 0)
    return (pl.BlockSpec(shape, index), pl.BlockSpec(shape, index),
            jax.ShapeDtypeStruct(arr.shape, BF16))


def _run_casts(cast_refs):
    n = len(cast_refs) // 2
    for src, dst in zip(cast_refs[:n], cast_refs[n:]):
        dst[...] = src[...].astype(BF16)


def _fold_pool_kernel(win_ref, wgrp_ref, ps_ref, wout_in_ref, wu_ref, wg_ref, wout_ref):
    for gi in range(len(POOL_WINDOWS)):
        cols = slice(gi * POOL_GROUP, (gi + 1) * POOL_GROUP)
        folded = jnp.dot(win_ref[0, :, cols], wgrp_ref[0, gi],
                         preferred_element_type=F32, precision=lax.Precision.HIGHEST)
        wu_ref[0, :, cols] = (folded * ps_ref[0, :, cols]).astype(BF16)
    wg_ref[0] = win_ref[0, :, WIDTH:].astype(BF16)
    wout_ref[0] = wout_in_ref[0].astype(BF16)


def _fold_pool_weights(w_in, w_grp, scale, w_out):
    n, d, _ = w_in.shape
    out = jax.ShapeDtypeStruct((n, d, WIDTH), BF16)
    return pl.pallas_call(
        _fold_pool_kernel,
        grid=(n,),
        in_specs=[pl.BlockSpec((1,) + w_in.shape[1:], lambda i: (i, 0, 0)),
                  pl.BlockSpec((1,) + w_grp.shape[1:], lambda i: (i, 0, 0, 0)),
                  pl.BlockSpec((1, 1, WIDTH), lambda i: (i, 0, 0)),
                  pl.BlockSpec((1,) + w_out.shape[1:], lambda i: (i, 0, 0))],
        out_specs=[pl.BlockSpec((1, d, WIDTH), lambda i: (i, 0, 0))] * 2
        + [pl.BlockSpec((1,) + w_out.shape[1:], lambda i: (i, 0, 0))],
        out_shape=[out, out, jax.ShapeDtypeStruct(w_out.shape, BF16)],
        compiler_params=_params(1),
        name="fold_pool_weights",
    )(w_in, w_grp, scale.reshape(n, 1, WIDTH), w_out)


def _split_rest(rest, n_casts):
    has_gain = len(rest) - 1 - 2 * n_casts
    fg_ref = rest[0] if has_gain else None
    cast_refs = rest[has_gain:has_gain + n_casts] + rest[has_gain + n_casts + 1:]
    return fg_ref, rest[has_gain + n_casts], cast_refs


def _pool_kernel(x_ref, xp_ref, xn_ref, mod_ref, ng_ref, wu_ref, wg_ref, wout_ref,
                 *rest, tile, seq_len, n_casts):
    fg_ref, o_ref, cast_refs = _split_rest(rest, n_casts)
    _run_casts(cast_refs)
    he, hm = _normed_rows(x_ref, xp_ref, xn_ref, mod_ref, ng_ref)
    t = jnp.dot(he, wu_ref[0], preferred_element_type=F32)
    t = _zero_outside_sequence(t, tile)
    g = jnp.dot(hm, wg_ref[0], preferred_element_type=F32)
    mixed = []
    for gi, width in enumerate(POOL_WINDOWS):
        te = t[:, gi * POOL_GROUP:(gi + 1) * POOL_GROUP]
        pooled = _window_mean(_window_sum(te, width, tile), width, tile, seq_len)
        mixed.append(pooled - te[HALO:HALO + tile])
    mixed = jnp.concatenate(mixed, axis=1)
    z = (mixed * _silu(g)).astype(BF16)
    y = jnp.dot(z, wout_ref[0], preferred_element_type=F32)
    _finish(x_ref[0], y, mod_ref[0, 0, 2:3, :], o_ref, fg_ref)


def _conv_kernel(x_ref, xp_ref, xn_ref, mod_ref, ng_ref, win_ref, dw_ref, db_ref, wout_ref,
                 *rest, tile, seq_len, n_casts):
    fg_ref, o_ref, cast_refs = _split_rest(rest, n_casts)
    _run_casts(cast_refs)
    he, hm = _normed_rows(x_ref, xp_ref, xn_ref, mod_ref, ng_ref)
    n = tile + 2 * HALO
    rows = slice(HALO, HALO + tile)
    y = None
    for ci in range(WIDTH // CONV_BLOCK):
        cols = slice(ci * CONV_BLOCK, (ci + 1) * CONV_BLOCK)

        def proj(h, part):
            w = win_ref[0, :, part * WIDTH + ci * CONV_BLOCK:part * WIDTH + (ci + 1) * CONV_BLOCK]
            return jnp.dot(h, w, preferred_element_type=F32)

        z = _zero_outside_sequence(proj(he, 1) * proj(he, 2), tile)
        conv = (dw_ref[0, 0:1, cols] * pltpu.roll(z, 1, axis=0)[rows]
                + dw_ref[0, 1:2, cols] * z[rows]
                + dw_ref[0, 2:3, cols] * pltpu.roll(z, n - 1, axis=0)[rows]
                + db_ref[0, :, cols])
        yc = (proj(hm, 0) * conv * _silu(proj(hm, 3))).astype(BF16)
        part = jnp.dot(yc, wout_ref[0, cols, :], preferred_element_type=F32)
        y = part if y is None else y + part
    _finish(x_ref[0], y, mod_ref[0, 0, 2:3, :], o_ref, fg_ref)


def _mixer_layer(kernel_fn, x, mods, layer, mod_row, norm_g, weights, j, final_g, tile, name,
                 casts=()):
    b, seq_len, d = x.shape
    tile = min(tile, seq_len)
    halo_blocks = seq_len // HALO
    per_tile = tile // HALO
    n_inner = seq_len // tile
    cast_blocks = [_cast_blocks(a, n_inner, b * n_inner) for a in casts]
    in_specs = [
        pl.BlockSpec((1, tile, d), lambda bi, i: (bi, i, 0)),
        pl.BlockSpec((1, HALO, d), lambda bi, i: (bi, jnp.maximum(i * per_tile - 1, 0), 0)),
        pl.BlockSpec((1, HALO, d),
                     lambda bi, i: (bi, jnp.minimum((i + 1) * per_tile, halo_blocks - 1), 0)),
        _mod_block(mods, layer, mod_row),
        _layer_block(norm_g, layer),
    ] + [_layer_block(w, j) for w in weights]
    args = [x, x, x, mods, norm_g] + list(weights)
    if final_g is not None:
        in_specs.append(pl.BlockSpec((1, d), lambda bi, i: (0, 0)))
        args.append(final_g.reshape(1, d))
    outs = pl.pallas_call(
        functools.partial(kernel_fn, tile=tile, seq_len=seq_len, n_casts=len(casts)),
        grid=(b, n_inner),
        in_specs=in_specs + [blk[0] for blk in cast_blocks],
        out_specs=[pl.BlockSpec((1, tile, d), lambda bi, i: (bi, i, 0))]
        + [blk[1] for blk in cast_blocks],
        out_shape=[jax.ShapeDtypeStruct(x.shape, F32)] + [blk[2] for blk in cast_blocks],
        compiler_params=_params(2),
        name=name,
    )(*args, *casts)
    return outs if casts else outs[0]


def _project_kernel(x_ref, mod_ref, ng_ref, *refs, scales, layouts):
    n = len(scales)
    n_casts = (len(refs) - 2 * n) // 2
    w_refs, o_refs = refs[:n], refs[n + n_casts:2 * n + n_casts]
    _run_casts(refs[n:n + n_casts] + refs[2 * n + n_casts:])
    shift, scale = mod_ref[0, 0, 0:1, :], mod_ref[0, 0, 1:2, :]
    h = _mod_norm(x_ref[0], ng_ref[0], shift, scale).astype(BF16)
    for w_ref, o_ref, s, layout in zip(w_refs, o_refs, scales, layouts):
        p = jnp.dot(h, w_ref[0], preferred_element_type=F32)
        if s != 1.0:
            p = p * s
        if layout == "flat":
            o_ref[0] = p.astype(o_ref.dtype)
            continue
        for hp in range(N_HEAD_PAIRS):
            block = p[:, hp * LANES:(hp + 1) * LANES]
            o_ref[0, hp] = (block.T if layout == "pairs_t" else block).astype(o_ref.dtype)


def _project(x, mods, layer, mod_row, norm_g, w, j, cols, layouts, scales, tile, name,
             casts=()):
    b, seq_len, d = x.shape
    tile = min(tile, seq_len)
    assert tile % LANES == 0
    specs = {
        "flat": (pl.BlockSpec((1, tile, WIDTH), lambda bi, i: (bi, i, 0)),
                 jax.ShapeDtypeStruct((b, seq_len, WIDTH), F32)),
        "pairs": (pl.BlockSpec((1, N_HEAD_PAIRS, tile, LANES), lambda bi, i: (bi, 0, i, 0)),
                  jax.ShapeDtypeStruct((b, N_HEAD_PAIRS, seq_len, LANES), BF16)),
        "pairs_t": (pl.BlockSpec((1, N_HEAD_PAIRS, LANES, tile), lambda bi, i: (bi, 0, 0, i)),
                    jax.ShapeDtypeStruct((b, N_HEAD_PAIRS, LANES, seq_len), BF16)),
    }
    n_inner = seq_len // tile
    cast_blocks = [_cast_blocks(a, n_inner, b * n_inner) for a in casts]
    out_specs = [specs[name_][0] for name_ in layouts] + [blk[1] for blk in cast_blocks]
    out_shape = [specs[name_][1] for name_ in layouts] + [blk[2] for blk in cast_blocks]
    return pl.pallas_call(
        functools.partial(_project_kernel, scales=scales, layouts=layouts),
        grid=(b, n_inner),
        in_specs=[
            pl.BlockSpec((1, tile, d), lambda bi, i: (bi, i, 0)),
            _mod_block(mods, layer, mod_row),
            _layer_block(norm_g, layer),
        ] + [_layer_block(w, j, col) for col in cols] + [blk[0] for blk in cast_blocks],
        out_specs=out_specs,
        out_shape=out_shape,
        compiler_params=_params(2),
        name=name,
    )(x, mods, norm_g, *([w] * len(cols)), *casts)


SLAB_TILES = SLAB_ROWS // 2


def _slab_base(pair_row0, n_rows):
    return jnp.clip(pair_row0 - WIN_ROWS // 2, 0, n_rows - SLAB_ROWS)


def _bias_tile_plan(n_rows):
    tiles, ids = [], []
    for r0 in range(0, n_rows, 2):
        base = int(np.clip(r0 - WIN_ROWS // 2, 0, n_rows - SLAB_ROWS))
        for j in range(SLAB_TILES):
            quad = []
            for key_row in (base + 2 * j, base + 2 * j + 1):
                for r in (r0, r0 + 1):
                    start = int(np.clip(r - WIN_ROWS // 2, 0, n_rows - WIN_ROWS))
                    assert base <= start and start + WIN_ROWS <= base + SLAB_ROWS
                    inside = start <= key_row < start + WIN_ROWS
                    quad.append(key_row - r + WIN_ROWS - 1 if inside else None)
            quad = tuple(quad)
            if quad not in tiles:
                tiles.append(quad)
            ids.append(tiles.index(quad))
    return tiles, np.asarray(ids, np.int32)


def _build_bias_tiles(w_ref, o_ref, tiles):
    k_col = lax.broadcasted_iota(jnp.int32, (GRID_W, LANES), 0)
    lane = lax.broadcasted_iota(jnp.int32, (GRID_W, LANES), 1)
    q_col = lane % GRID_W
    c_start = jnp.clip(q_col - WIN_COLS // 2, 0, GRID_W - WIN_COLS)
    inside = (k_col >= c_start) & (k_col < c_start + WIN_COLS)
    first_query_row = lane < GRID_W
    masked = jnp.full((GRID_W, LANES), MASK_VALUE, F32)

    def toeplitz(head, d, shift):
        if d is None:
            return masked
        row = jnp.broadcast_to(w_ref[head, d:d + 1, :], (GRID_W, LANES))
        return pltpu.roll(row, shift, axis=1, stride=1, stride_axis=0) * LOG2E

    def one_head(head, carry):
        for tile_id, quad in enumerate(tiles):
            halves = [jnp.where(inside,
                                jnp.where(first_query_row, toeplitz(head, quad[2 * kr], 0),
                                          toeplitz(head, quad[2 * kr + 1], GRID_W)),
                                masked) for kr in range(2)]
            o_ref[head, tile_id] = jnp.concatenate(halves, axis=0)
        return carry

    lax.fori_loop(0, w_ref.shape[0], one_head, 0)


def _bias_rows(rpb):
    h, n_dr, n_dc = rpb.shape
    assert n_dr == 2 * WIN_ROWS - 1 and n_dc == 2 * WIN_COLS - 1
    rev = rpb[..., ::-1]
    w = jnp.concatenate([rev[..., WIN_COLS - 1:], jnp.zeros((h, n_dr, LANES - n_dc), F32),
                         rev[..., :WIN_COLS - 1]], axis=-1)
    return jnp.pad(w, ((0, 0), (0, 2 * WIN_ROWS - n_dr), (0, 0)))


def _attention_kernel(ids_ref, q_ref, k_ref, v_ref, kc_ref, vc_ref, g_ref, x_ref, mod_ref,
                      rpb_ref, wout_ref, o_ref, bias_ref, s_scr, p_scr, r_scr, o_scr,
                      *, n_rows, tiles):
    @pl.when((pl.program_id(0) == 0) & (pl.program_id(1) == 0))
    def _():
        _build_bias_tiles(rpb_ref, bias_ref, tiles)

    pair_tokens = 2 * GRID_W
    n_ctx = kc_ref.shape[2]
    slab_keys = SLAB_ROWS * GRID_W
    n_items = (ATT_ROWS // 2) * N_HEAD_PAIRS
    first_head = lax.broadcasted_iota(jnp.int32, (LANES, pair_tokens), 0) < HEAD_DIM

    def locate(item):
        pb, hp = item // N_HEAD_PAIRS, item % N_HEAD_PAIRS
        pair = pl.program_id(1) * (ATT_ROWS // 2) + pb
        k_off = pl.multiple_of(_slab_base(2 * pair, n_rows) * GRID_W, pair_tokens)
        return hp, pair, pb * pair_tokens, k_off

    def scores(item, slot):
        hp, _, q_off, k_off = locate(item)
        qt = q_ref[0, hp, :, pl.ds(q_off, pair_tokens)]
        zero = jnp.zeros_like(qt)
        q2 = jnp.concatenate([jnp.where(first_head, qt, zero),
                              jnp.where(first_head, zero, qt)], axis=1)
        s_scr[slot, :n_ctx, :] = jnp.dot(kc_ref[0, hp], q2, preferred_element_type=F32)
        s_scr[slot, n_ctx:, :] = jnp.dot(k_ref[0, hp, pl.ds(k_off, slab_keys), :], q2,
                                         preferred_element_type=F32)

    def softmax(item, slot):
        hp, pair, _, _ = locate(item)
        bias = jnp.concatenate(
            [jnp.concatenate([bias_ref[2 * hp + a, ids_ref[pair * SLAB_TILES + j]]
                              for a in range(2)], axis=1)
             for j in range(SLAB_TILES)], axis=0)
        s_ctx = s_scr[slot, :n_ctx, :]
        s_loc = s_scr[slot, n_ctx:, :] + bias
        m = jnp.maximum(jnp.max(s_loc, axis=0, keepdims=True),
                        jnp.max(s_ctx, axis=0, keepdims=True))
        p_ctx = jnp.exp2(s_ctx - m)
        p_loc = jnp.exp2(s_loc - m)
        denom = jnp.sum(p_loc, axis=0, keepdims=True) + jnp.sum(p_ctx, axis=0, keepdims=True)
        p_scr[slot, :n_ctx, :] = p_ctx.astype(BF16)
        p_scr[slot, n_ctx:, :] = p_loc.astype(BF16)
        r_scr[slot] = jnp.broadcast_to(1.0 / denom, r_scr.shape[1:])

    def values(item, slot):
        hp, _, q_off, k_off = locate(item)
        pv = (jnp.dot(vc_ref[0, hp], p_scr[slot, :n_ctx, :], preferred_element_type=F32)
              + jnp.dot(v_ref[0, hp, :, pl.ds(k_off, slab_keys)], p_scr[slot, n_ctx:, :],
                        preferred_element_type=F32))
        pv = pv * r_scr[slot, 0:1, :]
        out = jnp.where(first_head, pv[:, :pair_tokens], pv[:, pair_tokens:])
        o_scr[hp, pl.ds(q_off, pair_tokens), :] = out.T

    for t in range(-2, n_items):
        if t + 2 < n_items:
            scores(t + 2, (t + 2) % PIPE_SLOTS)
        if 0 <= t + 1 < n_items:
            softmax(t + 1, (t + 1) % PIPE_SLOTS)
        if t >= 0:
            values(t, t % PIPE_SLOTS)

    gate = mod_ref[0, 0, 2:3, :]
    o = jnp.concatenate([o_scr[hp] for hp in range(N_HEAD_PAIRS)], axis=1)
    y = (o * _silu(g_ref[0])).astype(BF16)
    y = jnp.dot(y, wout_ref[0], preferred_element_type=F32)
    o_ref[0] = x_ref[0] + gate * y


def _attention_layer(x, q, k, v, kc, vc, g, mods, layer, rpb, w_out, j):
    b, seq_len, d = x.shape
    n_ctx = kc.shape[2] // b
    n_rows = seq_len // GRID_W
    assert n_rows % ATT_ROWS == 0 and n_rows >= SLAB_ROWS and ATT_ROWS % 2 == 0
    tile = ATT_ROWS * GRID_W
    n_keys = n_ctx + SLAB_ROWS * GRID_W
    tiles, ids = _bias_tile_plan(n_rows)
    rpb_rows = _bias_rows(rpb)
    tile_spec = pl.BlockSpec((1, tile, d), lambda bi, i: (bi, i, 0))
    q_spec = pl.BlockSpec((1, N_HEAD_PAIRS, LANES, tile), lambda bi, i: (bi, 0, 0, i))
    seq_spec = pl.BlockSpec((1, N_HEAD_PAIRS, seq_len, LANES), lambda bi, i: (bi, 0, 0, 0))
    ctx_spec = pl.BlockSpec((1, N_HEAD_PAIRS, n_ctx, LANES), lambda bi, i: (0, 0, bi, 0))
    seq_t_spec = pl.BlockSpec((1, N_HEAD_PAIRS, LANES, seq_len), lambda bi, i: (bi, 0, 0, 0))
    ctx_t_spec = pl.BlockSpec((1, N_HEAD_PAIRS, LANES, n_ctx), lambda bi, i: (0, 0, 0, bi))
    return pl.pallas_call(
        functools.partial(_attention_kernel, n_rows=n_rows, tiles=tiles),
        grid=(b, n_rows // ATT_ROWS),
        in_specs=[pl.BlockSpec(memory_space=pltpu.SMEM),
                  q_spec, seq_spec, seq_t_spec, ctx_spec, ctx_t_spec, tile_spec, tile_spec,
                  _mod_block(mods, layer, None),
                  pl.BlockSpec(rpb_rows.shape, lambda bi, i: (0, 0, 0),
                               pipeline_mode=pl.Buffered(1)),
                  _layer_block(w_out, j)],
        out_specs=tile_spec,
        out_shape=jax.ShapeDtypeStruct(x.shape, F32),
        scratch_shapes=[pltpu.VMEM((N_HEADS, len(tiles), LANES, LANES), F32),
                        pltpu.VMEM((PIPE_SLOTS, n_keys, 4 * GRID_W), F32),
                        pltpu.VMEM((PIPE_SLOTS, n_keys, 4 * GRID_W), BF16),
                        pltpu.VMEM((PIPE_SLOTS, SUBLANES, 4 * GRID_W), F32),
                        pltpu.VMEM((N_HEAD_PAIRS, tile, LANES), F32)],
        compiler_params=_params(2),
        name="na_attention",
    )(jnp.asarray(ids), q, k, v, kc, vc, g, x, mods, rpb_rows, w_out)


def kernel(x, c, ctx, c_ctx, norm_g, ada_w, ada_b, pool_w_in, pool_w_grp, pool_scale, pool_w_out,
           na_w_in, na_rpb, na_w_out, conv_w_in, conv_dw, conv_db, conv_w_out, final_g):
    depth = norm_g.shape[0]
    batch, _, d = x.shape
    assert batch < COND_ROWS and WIN_ROWS // 2 <= HALO

    cond = jnp.zeros((COND_ROWS, d), F32).at[:batch].set(c).at[batch].set(c_ctx)
    mods = _modulation(cond, ada_w, ada_b).reshape(depth, COND_ROWS, 3, d)
    norm_g = norm_g.reshape(depth, 1, d)

    f32_weights = {1: (na_w_in, na_w_out), 2: (conv_w_in, conv_w_out)}
    bf16_weights = {0: _fold_pool_weights(pool_w_in, pool_w_grp, pool_scale, pool_w_out)}

    def weights_of(kind):
        if kind not in bf16_weights:
            bf16_weights[kind] = tuple(w.astype(BF16) for w in f32_weights[kind])
        return bf16_weights[kind]

    def casts_for_next(i):
        kind = (i + 1) % N_MIXERS
        return f32_weights[kind] if i + 1 < depth and kind not in bf16_weights else ()

    last_ctx_reader = max([i for i in range(depth) if i % N_MIXERS == 1], default=-1)
    for i in range(depth):
        kind, j = i % N_MIXERS, i // N_MIXERS
        update_ctx = i < last_ctx_reader
        fg = final_g if i == depth - 1 else None
        w_bf16 = weights_of(kind)
        casts = casts_for_next(i)
        if kind != 1:
            if kind == 0:
                kernel_fn, weights = _pool_kernel, w_bf16
            else:
                kernel_fn = _conv_kernel
                weights = (w_bf16[0], conv_dw, conv_db.reshape(-1, 1, WIDTH), w_bf16[1])
            run = functools.partial(_mixer_layer, kernel_fn, mods=mods, layer=i, norm_g=norm_g,
                                    weights=weights, j=j)
            if update_ctx:
                ctx = run(x=ctx, mod_row=batch, final_g=None, tile=CTX_TILE, name=f"ctx_layer{i}")
            x = run(x=x, mod_row=None, final_g=fg, tile=X_TILE, name=f"layer{i}", casts=casts)
            if casts:
                x, *cast_out = x
        else:
            if update_ctx:
                raise NotImplementedError("context output of a neighbourhood-attention layer")
            if fg is not None:
                raise NotImplementedError("final norm after a neighbourhood-attention layer")
            w_in, w_out = w_bf16
            q, k, v, g, *cast_out = _project(
                x, mods, i, None, norm_g, w_in, j, (0, 1, 2, 3),
                ("pairs_t", "pairs", "pairs_t", "flat"),
                (HEAD_DIM ** -0.5 * LOG2E, 1.0, 1.0, 1.0), X_TILE, f"na_project{i}", casts=casts)
            kc, vc = _project(ctx.reshape(1, -1, d), mods, i, batch, norm_g, w_in, j, (1, 2),
                              ("pairs", "pairs_t"), (1.0, 1.0), X_TILE, f"na_ctx_project{i}")
            x = _attention_layer(x, q, k, v, kc, vc, g, mods, i, na_rpb[j], w_out, j)
        if casts:
            bf16_weights[(i + 1) % N_MIXERS] = tuple(cast_out)
    return x
```

```python
import functools
import math

import numpy as np

import jax
import jax.numpy as jnp
from jax import lax
from jax.experimental import pallas as pl
from jax.experimental.pallas import tpu as pltpu

D_MODEL = 1024
WIDTH = D_MODEL
GRID_W = 64
N_MIXERS = 3
POOL_WINDOWS = (2, 4, 8, 16)
POOL_GROUP = WIDTH // len(POOL_WINDOWS)
HEAD_DIM = 64
N_HEADS = WIDTH // HEAD_DIM
WIN_ROWS = 8
WIN_COLS = 16
EPS = 1e-6

LANES = 128
SUBLANES = 8
BF16_SUBLANES = 16
MXU_TILE = 256
VMEM_LIMIT = 56 * 1024 * 1024

HALO = SUBLANES
N_HEAD_PAIRS = WIDTH // LANES
MASK_VALUE = -1e30
LOG2E = math.log2(math.e)

X_TILE = 1024
CTX_TILE = 256
ATT_ROWS = 8
SLAB_ROWS = WIN_ROWS + 2
PIPE_SLOTS = 2
BIAS_HEADS_PER_STEP = 8
CONV_BLOCK = MXU_TILE
COND_ROWS = BF16_SUBLANES

F32 = jnp.float32
BF16 = jnp.bfloat16


def _silu(x):
    return x / (1.0 + jnp.exp(-x))


def _mod_norm(x, norm_g, shift, scale):
    ms = jnp.mean(x * x, axis=-1, keepdims=True)
    return (x * lax.rsqrt(ms + EPS)) * (norm_g * (1.0 + scale)) + shift


def _params(n_axes):
    return pltpu.CompilerParams(
        dimension_semantics=("arbitrary",) * n_axes, vmem_limit_bytes=VMEM_LIMIT)


def _layer_block(arr, j, col=None):
    shape = (1,) + arr.shape[1:]
    index = (j,) + (0,) * (arr.ndim - 1)
    if col is not None:
        shape = shape[:-1] + (WIDTH,)
        index = index[:-1] + (col,)
    return pl.BlockSpec(shape, lambda *_: index, pipeline_mode=pl.Buffered(1))


def _mod_block(mods, layer, row):
    d = mods.shape[-1]
    if row is None:
        return pl.BlockSpec((1, 1, 3, d), lambda bi, i: (layer, bi, 0, 0))
    return pl.BlockSpec((1, 1, 3, d), lambda bi, i: (layer, row, 0, 0))


def _modulation_kernel(cond_ref, w_ref, b_ref, o_ref):
    s = _silu(cond_ref[...]).astype(BF16)
    o_ref[0] = jnp.dot(s, w_ref[0].astype(BF16), preferred_element_type=F32) + b_ref[0]


def _modulation(cond, ada_w, ada_b):
    depth, d, n = ada_w.shape
    rows = cond.shape[0]
    tn = n
    return pl.pallas_call(
        _modulation_kernel,
        grid=(depth, n // tn),
        in_specs=[
            pl.BlockSpec((rows, d), lambda i, j: (0, 0)),
            pl.BlockSpec((1, d, tn), lambda i, j: (i, 0, j)),
            pl.BlockSpec((1, 1, tn), lambda i, j: (i, 0, j)),
        ],
        out_specs=pl.BlockSpec((1, rows, tn), lambda i, j: (i, 0, j)),
        out_shape=jax.ShapeDtypeStruct((depth, rows, n), F32),
        compiler_params=_params(2),
        name="modulation",
    )(cond, ada_w, ada_b.reshape(depth, 1, n))


def _normed_rows(x_ref, xp_ref, xn_ref, mod_ref, ng_ref):
    shift, scale = mod_ref[0, 0, 0:1, :], mod_ref[0, 0, 1:2, :]
    xe = jnp.concatenate([xp_ref[0], x_ref[0], xn_ref[0]], axis=0)
    he = _mod_norm(xe, ng_ref[0], shift, scale)
    return he.astype(BF16), he[HALO:he.shape[0] - HALO].astype(BF16)


def _zero_outside_sequence(e, tile):
    i = pl.program_id(1)
    keep_prev = (i > 0).astype(F32)
    keep_next = (i < pl.num_programs(1) - 1).astype(F32)
    return jnp.concatenate(
        [e[:HALO] * keep_prev, e[HALO:HALO + tile], e[HALO + tile:] * keep_next], axis=0)


def _window_sum(e, width, tile):
    n = e.shape[0]
    half = width // 2
    f = e
    k = 1
    while k < half:
        f = f + pltpu.roll(f, n - k, axis=0)
        k *= 2
    if half == HALO:
        return f[:tile] + f[HALO:HALO + tile]
    return (pltpu.roll(f, half, axis=0) + f)[HALO:HALO + tile]


def _window_mean(ws, width, tile, seq_len):
    half = width // 2
    row = lax.broadcasted_iota(jnp.int32, (HALO, 1), 0)

    def inv_count(first_row):
        t = row + (pl.program_id(1) * tile + first_row)
        cnt = jnp.minimum(t + half, seq_len) - jnp.maximum(t - half, 0)
        return 1.0 / cnt.astype(F32)

    return jnp.concatenate([ws[:HALO] * inv_count(0),
                            ws[HALO:tile - HALO] * (1.0 / width),
                            ws[tile - HALO:] * inv_count(tile - HALO)], axis=0)


def _finish(x, y, gate, o_ref, fg_ref):
    out = x + gate * y
    if fg_ref is not None:
        ms = jnp.mean(out * out, axis=-1, keepdims=True)
        out = (out * lax.rsqrt(ms + EPS)) * fg_ref[...]
    o_ref[0] = out


def _cast_blocks(arr, n_inner, n_steps):
    layers, rows, cols = arr.shape
    if cols % (n_steps * LANES) == 0:
        shape = (layers, rows, cols // n_steps)
        index = lambda bi, i: (0, 0, bi * n_inner + i)
    else:
        assert rows % (n_steps * BF16_SUBLANES) == 0
        shape = (layers, rows // n_steps, cols)
        index = lambda bi, i: (0, bi * n_inner + i, 0)
    return (pl.BlockSpec(shape, index), pl.BlockSpec(shape, index),
            jax.ShapeDtypeStruct(arr.shape, BF16))


def _run_casts(cast_refs):
    n = len(cast_refs) // 2
    for src, dst in zip(cast_refs[:n], cast_refs[n:]):
        dst[...] = src[...].astype(BF16)


def _fold_pool_kernel(win_ref, wgrp_ref, ps_ref, wout_in_ref, wu_ref, wg_ref, wout_ref):
    for gi in range(len(POOL_WINDOWS)):
        cols = slice(gi * POOL_GROUP, (gi + 1) * POOL_GROUP)
        folded = jnp.dot(win_ref[0, :, cols].astype(BF16), wgrp_ref[0, gi].astype(BF16),
                         preferred_element_type=F32)
        wu_ref[0, :, cols] = (folded * ps_ref[0, :, cols]).astype(BF16)
    wg_ref[0] = win_ref[0, :, WIDTH:].astype(BF16)
    wout_ref[0] = wout_in_ref[0].astype(BF16)


def _fold_pool_weights(w_in, w_grp, scale, w_out):
    n, d, _ = w_in.shape
    out = jax.ShapeDtypeStruct((n, d, WIDTH), BF16)
    return pl.pallas_call(
        _fold_pool_kernel,
        grid=(n,),
        in_specs=[pl.BlockSpec((1,) + w_in.shape[1:], lambda i: (i, 0, 0)),
                  pl.BlockSpec((1,) + w_grp.shape[1:], lambda i: (i, 0, 0, 0)),
                  pl.BlockSpec((1, 1, WIDTH), lambda i: (i, 0, 0)),
                  pl.BlockSpec((1,) + w_out.shape[1:], lambda i: (i, 0, 0))],
        out_specs=[pl.BlockSpec((1, d, WIDTH), lambda i: (i, 0, 0))] * 2
        + [pl.BlockSpec((1,) + w_out.shape[1:], lambda i: (i, 0, 0))],
        out_shape=[out, out, jax.ShapeDtypeStruct(w_out.shape, BF16)],
        compiler_params=_params(1),
        name="fold_pool_weights",
    )(w_in, w_grp, scale.reshape(n, 1, WIDTH), w_out)


def _split_rest(rest, n_casts):
    has_gain = len(rest) - 1 - 2 * n_casts
    fg_ref = rest[0] if has_gain else None
    cast_refs = rest[has_gain:has_gain + n_casts] + rest[has_gain + n_casts + 1:]
    return fg_ref, rest[has_gain + n_casts], cast_refs


def _pool_kernel(x_ref, xp_ref, xn_ref, mod_ref, ng_ref, wu_ref, wg_ref, wout_ref,
                 *rest, tile, seq_len, n_casts):
    fg_ref, o_ref, cast_refs = _split_rest(rest, n_casts)
    _run_casts(cast_refs)
    he, hm = _normed_rows(x_ref, xp_ref, xn_ref, mod_ref, ng_ref)
    t = jnp.dot(he, wu_ref[0], preferred_element_type=F32)
    t = _zero_outside_sequence(t, tile)
    g = jnp.dot(hm, wg_ref[0], preferred_element_type=F32)
    mixed = []
    for gi, width in enumerate(POOL_WINDOWS):
        te = t[:, gi * POOL_GROUP:(gi + 1) * POOL_GROUP]
        pooled = _window_mean(_window_sum(te, width, tile), width, tile, seq_len)
        mixed.append(pooled - te[HALO:HALO + tile])
    mixed = jnp.concatenate(mixed, axis=1)
    z = (mixed * _silu(g)).astype(BF16)
    y = jnp.dot(z, wout_ref[0], preferred_element_type=F32)
    _finish(x_ref[0], y, mod_ref[0, 0, 2:3, :], o_ref, fg_ref)


def _conv_kernel(x_ref, xp_ref, xn_ref, mod_ref, ng_ref, win_ref, dw_ref, db_ref, wout_ref,
                 *rest, tile, seq_len, n_casts):
    fg_ref, o_ref, cast_refs = _split_rest(rest, n_casts)
    _run_casts(cast_refs)
    he, hm = _normed_rows(x_ref, xp_ref, xn_ref, mod_ref, ng_ref)
    n = tile + 2 * HALO
    rows = slice(HALO, HALO + tile)
    y = None
    for ci in range(WIDTH // CONV_BLOCK):
        cols = slice(ci * CONV_BLOCK, (ci + 1) * CONV_BLOCK)

        def proj(h, part):
            w = win_ref[0, :, part * WIDTH + ci * CONV_BLOCK:part * WIDTH + (ci + 1) * CONV_BLOCK]
            return jnp.dot(h, w, preferred_element_type=F32)

        z = _zero_outside_sequence(proj(he, 1) * proj(he, 2), tile)
        conv = (dw_ref[0, 0:1, cols] * pltpu.roll(z, 1, axis=0)[rows]
                + dw_ref[0, 1:2, cols] * z[rows]
                + dw_ref[0, 2:3, cols] * pltpu.roll(z, n - 1, axis=0)[rows]
                + db_ref[0, :, cols])
        yc = (proj(hm, 0) * conv * _silu(proj(hm, 3))).astype(BF16)
        part = jnp.dot(yc, wout_ref[0, cols, :], preferred_element_type=F32)
        y = part if y is None else y + part
    _finish(x_ref[0], y, mod_ref[0, 0, 2:3, :], o_ref, fg_ref)


def _mixer_layer(kernel_fn, x, mods, layer, mod_row, norm_g, weights, j, final_g, tile, name,
                 casts=()):
    b, seq_len, d = x.shape
    tile = min(tile, seq_len)
    halo_blocks = seq_len // HALO
    per_tile = tile // HALO
    n_inner = seq_len // tile
    cast_blocks = [_cast_blocks(a, n_inner, b * n_inner) for a in casts]
    in_specs = [
        pl.BlockSpec((1, tile, d), lambda bi, i: (bi, i, 0)),
        pl.BlockSpec((1, HALO, d), lambda bi, i: (bi, jnp.maximum(i * per_tile - 1, 0), 0)),
        pl.BlockSpec((1, HALO, d),
                     lambda bi, i: (bi, jnp.minimum((i + 1) * per_tile, halo_blocks - 1), 0)),
        _mod_block(mods, layer, mod_row),
        _layer_block(norm_g, layer),
    ] + [_layer_block(w, j) for w in weights]
    args = [x, x, x, mods, norm_g] + list(weights)
    if final_g is not None:
        in_specs.append(pl.BlockSpec((1, d), lambda bi, i: (0, 0)))
        args.append(final_g.reshape(1, d))
    outs = pl.pallas_call(
        functools.partial(kernel_fn, tile=tile, seq_len=seq_len, n_casts=len(casts)),
        grid=(b, n_inner),
        in_specs=in_specs + [blk[0] for blk in cast_blocks],
        out_specs=[pl.BlockSpec((1, tile, d), lambda bi, i: (bi, i, 0))]
        + [blk[1] for blk in cast_blocks],
        out_shape=[jax.ShapeDtypeStruct(x.shape, F32)] + [blk[2] for blk in cast_blocks],
        compiler_params=_params(2),
        name=name,
    )(*args, *casts)
    return outs if casts else outs[0]


def _project_kernel(x_ref, mod_ref, ng_ref, *refs, scales, layouts):
    n = len(scales)
    n_casts = (len(refs) - 2 * n) // 2
    w_refs, o_refs = refs[:n], refs[n + n_casts:2 * n + n_casts]
    _run_casts(refs[n:n + n_casts] + refs[2 * n + n_casts:])
    shift, scale = mod_ref[0, 0, 0:1, :], mod_ref[0, 0, 1:2, :]
    h = _mod_norm(x_ref[0], ng_ref[0], shift, scale).astype(BF16)
    for w_ref, o_ref, s, layout in zip(w_refs, o_refs, scales, layouts):
        p = jnp.dot(h, w_ref[0], preferred_element_type=F32)
        if s != 1.0:
            p = p * s
        if layout == "flat":
            o_ref[0] = p.astype(o_ref.dtype)
            continue
        for hp in range(N_HEAD_PAIRS):
            block = p[:, hp * LANES:(hp + 1) * LANES]
            o_ref[0, hp] = (block.T if layout == "pairs_t" else block).astype(o_ref.dtype)


def _project(x, mods, layer, mod_row, norm_g, w, j, cols, layouts, scales, tile, name,
             casts=()):
    b, seq_len, d = x.shape
    tile = min(tile, seq_len)
    assert tile % LANES == 0
    specs = {
        "flat": (pl.BlockSpec((1, tile, WIDTH), lambda bi, i: (bi, i, 0)),
                 jax.ShapeDtypeStruct((b, seq_len, WIDTH), F32)),
        "pairs": (pl.BlockSpec((1, N_HEAD_PAIRS, tile, LANES), lambda bi, i: (bi, 0, i, 0)),
                  jax.ShapeDtypeStruct((b, N_HEAD_PAIRS, seq_len, LANES), BF16)),
        "pairs_t": (pl.BlockSpec((1, N_HEAD_PAIRS, LANES, tile), lambda bi, i: (bi, 0, 0, i)),
                    jax.ShapeDtypeStruct((b, N_HEAD_PAIRS, LANES, seq_len), BF16)),
    }
    n_inner = seq_len // tile
    cast_blocks = [_cast_blocks(a, n_inner, b * n_inner) for a in casts]
    out_specs = [specs[name_][0] for name_ in layouts] + [blk[1] for blk in cast_blocks]
    out_shape = [specs[name_][1] for name_ in layouts] + [blk[2] for blk in cast_blocks]
    return pl.pallas_call(
        functools.partial(_project_kernel, scales=scales, layouts=layouts),
        grid=(b, n_inner),
        in_specs=[
            pl.BlockSpec((1, tile, d), lambda bi, i: (bi, i, 0)),
            _mod_block(mods, layer, mod_row),
            _layer_block(norm_g, layer),
        ] + [_layer_block(w, j, col) for col in cols] + [blk[0] for blk in cast_blocks],
        out_specs=out_specs,
        out_shape=out_shape,
        compiler_params=_params(2),
        name=name,
    )(x, mods, norm_g, *([w] * len(cols)), *casts)


SLAB_TILES = SLAB_ROWS // 2


def _slab_base(pair_row0, n_rows):
    return jnp.clip(pair_row0 - WIN_ROWS // 2, 0, n_rows - SLAB_ROWS)


def _bias_tile_plan(n_rows):
    tiles, ids = [], []
    for r0 in range(0, n_rows, 2):
        base = int(np.clip(r0 - WIN_ROWS // 2, 0, n_rows - SLAB_ROWS))
        for j in range(SLAB_TILES):
            quad = []
            for key_row in (base + 2 * j, base + 2 * j + 1):
                for r in (r0, r0 + 1):
                    start = int(np.clip(r - WIN_ROWS // 2, 0, n_rows - WIN_ROWS))
                    assert base <= start and start + WIN_ROWS <= base + SLAB_ROWS
                    inside = start <= key_row < start + WIN_ROWS
                    quad.append(key_row - r + WIN_ROWS - 1 if inside else None)
            quad = tuple(quad)
            if quad not in tiles:
                tiles.append(quad)
            ids.append(tiles.index(quad))
    return tiles, np.asarray(ids, np.int32)


def _bias_tiles_kernel(w_ref, o_ref, *, tiles):
    k_col = lax.broadcasted_iota(jnp.int32, (GRID_W, LANES), 0)
    lane = lax.broadcasted_iota(jnp.int32, (GRID_W, LANES), 1)
    q_col = lane % GRID_W
    c_start = jnp.clip(q_col - WIN_COLS // 2, 0, GRID_W - WIN_COLS)
    inside = (k_col >= c_start) & (k_col < c_start + WIN_COLS)
    first_query_row = lane < GRID_W
    masked = jnp.full((GRID_W, LANES), MASK_VALUE, F32)

    def toeplitz(head, d, shift):
        if d is None:
            return masked
        row = jnp.broadcast_to(w_ref[head, d:d + 1, :], (GRID_W, LANES))
        return pltpu.roll(row, shift, axis=1, stride=1, stride_axis=0) * LOG2E

    for head in range(w_ref.shape[0]):
        for tile_id, quad in enumerate(tiles):
            halves = [jnp.where(inside,
                                jnp.where(first_query_row, toeplitz(head, quad[2 * kr], 0),
                                          toeplitz(head, quad[2 * kr + 1], GRID_W)),
                                masked) for kr in range(2)]
            o_ref[head, tile_id] = jnp.concatenate(halves, axis=0)


def _attention_bias(rpb, tiles):
    h, n_dr, n_dc = rpb.shape
    assert n_dr == 2 * WIN_ROWS - 1 and n_dc == 2 * WIN_COLS - 1
    rev = rpb[..., ::-1]
    w = jnp.concatenate([rev[..., WIN_COLS - 1:], jnp.zeros((h, n_dr, LANES - n_dc), F32),
                         rev[..., :WIN_COLS - 1]], axis=-1)
    w = jnp.pad(w, ((0, 0), (0, 2 * WIN_ROWS - n_dr), (0, 0)))
    return pl.pallas_call(
        functools.partial(_bias_tiles_kernel, tiles=tiles),
        grid=(h // BIAS_HEADS_PER_STEP,),
        in_specs=[pl.BlockSpec((BIAS_HEADS_PER_STEP, 2 * WIN_ROWS, LANES), lambda i: (i, 0, 0))],
        out_specs=pl.BlockSpec((BIAS_HEADS_PER_STEP, len(tiles), LANES, LANES),
                               lambda i: (i, 0, 0, 0)),
        out_shape=jax.ShapeDtypeStruct((h, len(tiles), LANES, LANES), F32),
        compiler_params=_params(1),
        name="bias_tiles",
    )(w)


def _attention_kernel(ids_ref, q_ref, k_ref, v_ref, kc_ref, vc_ref, g_ref, x_ref, mod_ref,
                      bias_ref, wout_ref, o_ref, s_scr, p_scr, r_scr, o_scr, *, n_rows):
    pair_tokens = 2 * GRID_W
    n_ctx = kc_ref.shape[2]
    slab_keys = SLAB_ROWS * GRID_W
    n_items = (ATT_ROWS // 2) * N_HEAD_PAIRS
    first_head = lax.broadcasted_iota(jnp.int32, (LANES, pair_tokens), 0) < HEAD_DIM

    def locate(item):
        pb, hp = item // N_HEAD_PAIRS, item % N_HEAD_PAIRS
        pair = pl.program_id(1) * (ATT_ROWS // 2) + pb
        k_off = pl.multiple_of(_slab_base(2 * pair, n_rows) * GRID_W, pair_tokens)
        return hp, pair, pb * pair_tokens, k_off

    def scores(item, slot):
        hp, _, q_off, k_off = locate(item)
        qt = q_ref[0, hp, :, pl.ds(q_off, pair_tokens)]
        zero = jnp.zeros_like(qt)
        q2 = jnp.concatenate([jnp.where(first_head, qt, zero),
                              jnp.where(first_head, zero, qt)], axis=1)
        s_scr[slot, :n_ctx, :] = jnp.dot(kc_ref[0, hp], q2, preferred_element_type=F32)
        s_scr[slot, n_ctx:, :] = jnp.dot(k_ref[0, hp, pl.ds(k_off, slab_keys), :], q2,
                                         preferred_element_type=F32)

    def softmax(item, slot):
        hp, pair, _, _ = locate(item)
        bias = jnp.concatenate(
            [jnp.concatenate([bias_ref[2 * hp + a, ids_ref[pair * SLAB_TILES + j]]
                              for a in range(2)], axis=1)
             for j in range(SLAB_TILES)], axis=0)
        s_ctx = s_scr[slot, :n_ctx, :]
        s_loc = s_scr[slot, n_ctx:, :] + bias
        m = jnp.maximum(jnp.max(s_loc, axis=0, keepdims=True),
                        jnp.max(s_ctx, axis=0, keepdims=True))
        p_ctx = jnp.exp2(s_ctx - m)
        p_loc = jnp.exp2(s_loc - m)
        denom = jnp.sum(p_loc, axis=0, keepdims=True) + jnp.sum(p_ctx, axis=0, keepdims=True)
        p_scr[slot, :n_ctx, :] = p_ctx.astype(BF16)
        p_scr[slot, n_ctx:, :] = p_loc.astype(BF16)
        r_scr[slot] = jnp.broadcast_to(1.0 / denom, r_scr.shape[1:])

    def values(item, slot):
        hp, _, q_off, k_off = locate(item)
        pv = (jnp.dot(vc_ref[0, hp], p_scr[slot, :n_ctx, :], preferred_element_type=F32)
              + jnp.dot(v_ref[0, hp, :, pl.ds(k_off, slab_keys)], p_scr[slot, n_ctx:, :],
                        preferred_element_type=F32))
        pv = pv * r_scr[slot, 0:1, :]
        out = jnp.where(first_head, pv[:, :pair_tokens], pv[:, pair_tokens:])
        o_scr[hp, pl.ds(q_off, pair_tokens), :] = out.T

    for t in range(-2, n_items):
        if t + 2 < n_items:
            scores(t + 2, (t + 2) % PIPE_SLOTS)
        if 0 <= t + 1 < n_items:
            softmax(t + 1, (t + 1) % PIPE_SLOTS)
        if t >= 0:
            values(t, t % PIPE_SLOTS)

    gate = mod_ref[0, 0, 2:3, :]
    o = jnp.concatenate([o_scr[hp] for hp in range(N_HEAD_PAIRS)], axis=1)
    y = (o * _silu(g_ref[0])).astype(BF16)
    y = jnp.dot(y, wout_ref[0], preferred_element_type=F32)
    o_ref[0] = x_ref[0] + gate * y


def _attention_layer(x, q, k, v, kc, vc, g, mods, layer, rpb, w_out, j):
    b, seq_len, d = x.shape
    n_ctx = kc.shape[2] // b
    n_rows = seq_len // GRID_W
    assert n_rows % ATT_ROWS == 0 and n_rows >= SLAB_ROWS and ATT_ROWS % 2 == 0
    tile = ATT_ROWS * GRID_W
    n_keys = n_ctx + SLAB_ROWS * GRID_W
    tiles, ids = _bias_tile_plan(n_rows)
    bias = _attention_bias(rpb, tiles)
    tile_spec = pl.BlockSpec((1, tile, d), lambda bi, i: (bi, i, 0))
    q_spec = pl.BlockSpec((1, N_HEAD_PAIRS, LANES, tile), lambda bi, i: (bi, 0, 0, i))
    seq_spec = pl.BlockSpec((1, N_HEAD_PAIRS, seq_len, LANES), lambda bi, i: (bi, 0, 0, 0))
    ctx_spec = pl.BlockSpec((1, N_HEAD_PAIRS, n_ctx, LANES), lambda bi, i: (0, 0, bi, 0))
    seq_t_spec = pl.BlockSpec((1, N_HEAD_PAIRS, LANES, seq_len), lambda bi, i: (bi, 0, 0, 0))
    ctx_t_spec = pl.BlockSpec((1, N_HEAD_PAIRS, LANES, n_ctx), lambda bi, i: (0, 0, 0, bi))
    return pl.pallas_call(
        functools.partial(_attention_kernel, n_rows=n_rows),
        grid=(b, n_rows // ATT_ROWS),
        in_specs=[pl.BlockSpec(memory_space=pltpu.SMEM),
                  q_spec, seq_spec, seq_t_spec, ctx_spec, ctx_t_spec, tile_spec, tile_spec,
                  _mod_block(mods, layer, None),
                  pl.BlockSpec(bias.shape, lambda bi, i: (0, 0, 0, 0),
                               pipeline_mode=pl.Buffered(1)),
                  _layer_block(w_out, j)],
        out_specs=tile_spec,
        out_shape=jax.ShapeDtypeStruct(x.shape, F32),
        scratch_shapes=[pltpu.VMEM((PIPE_SLOTS, n_keys, 4 * GRID_W), F32),
                        pltpu.VMEM((PIPE_SLOTS, n_keys, 4 * GRID_W), BF16),
                        pltpu.VMEM((PIPE_SLOTS, SUBLANES, 4 * GRID_W), F32),
                        pltpu.VMEM((N_HEAD_PAIRS, tile, LANES), F32)],
        compiler_params=_params(2),
        name="na_attention",
    )(jnp.asarray(ids), q, k, v, kc, vc, g, x, mods, bias, w_out)


def kernel(x, c, ctx, c_ctx, norm_g, ada_w, ada_b, pool_w_in, pool_w_grp, pool_scale, pool_w_out,
           na_w_in, na_rpb, na_w_out, conv_w_in, conv_dw, conv_db, conv_w_out, final_g):
    depth = norm_g.shape[0]
    batch, _, d = x.shape
    assert batch < COND_ROWS and WIN_ROWS // 2 <= HALO

    cond = jnp.zeros((COND_ROWS, d), F32).at[:batch].set(c).at[batch].set(c_ctx)
    mods = _modulation(cond, ada_w, ada_b).reshape(depth, COND_ROWS, 3, d)
    norm_g = norm_g.reshape(depth, 1, d)

    f32_weights = {1: (na_w_in, na_w_out), 2: (conv_w_in, conv_w_out)}
    bf16_weights = {0: _fold_pool_weights(pool_w_in, pool_w_grp, pool_scale, pool_w_out)}

    def weights_of(kind):
        if kind not in bf16_weights:
            bf16_weights[kind] = tuple(w.astype(BF16) for w in f32_weights[kind])
        return bf16_weights[kind]

    def casts_for_next(i):
        kind = (i + 1) % N_MIXERS
        return f32_weights[kind] if i + 1 < depth and kind not in bf16_weights else ()

    last_ctx_reader = max([i for i in range(depth) if i % N_MIXERS == 1], default=-1)
    for i in range(depth):
        kind, j = i % N_MIXERS, i // N_MIXERS
        update_ctx = i < last_ctx_reader
        fg = final_g if i == depth - 1 else None
        w_bf16 = weights_of(kind)
        casts = casts_for_next(i)
        if kind != 1:
            if kind == 0:
                kernel_fn, weights = _pool_kernel, w_bf16
            else:
                kernel_fn = _conv_kernel
                weights = (w_bf16[0], conv_dw, conv_db.reshape(-1, 1, WIDTH), w_bf16[1])
            run = functools.partial(_mixer_layer, kernel_fn, mods=mods, layer=i, norm_g=norm_g,
                                    weights=weights, j=j)
            if update_ctx:
                ctx = run(x=ctx, mod_row=batch, final_g=None, tile=CTX_TILE, name=f"ctx_layer{i}")
            x = run(x=x, mod_row=None, final_g=fg, tile=X_TILE, name=f"layer{i}", casts=casts)
            if casts:
                x, *cast_out = x
        else:
            if update_ctx:
                raise NotImplementedError("context output of a neighbourhood-attention layer")
            if fg is not None:
                raise NotImplementedError("final norm after a neighbourhood-attention layer")
            w_in, w_out = w_bf16
            q, k, v, g, *cast_out = _project(
                x, mods, i, None, norm_g, w_in, j, (0, 1, 2, 3),
                ("pairs_t", "pairs", "pairs_t", "flat"),
                (HEAD_DIM ** -0.5 * LOG2E, 1.0, 1.0, 1.0), X_TILE, f"na_project{i}", casts=casts)
            kc, vc = _project(ctx.reshape(1, -1, d), mods, i, batch, norm_g, w_in, j, (1, 2),
                              ("pairs", "pairs_t"), (1.0, 1.0), X_TILE, f"na_ctx_project{i}")
            x = _attention_layer(x, q, k, v, kc, vc, g, mods, i, na_rpb[j], w_out, j)
        if casts:
            bf16_weights[(i + 1) % N_MIXERS] = tuple(cast_out)
    return x
```

```python
import functools
import math

import numpy as np

import jax
import jax.numpy as jnp
from jax import lax
from jax.experimental import pallas as pl
from jax.experimental.pallas import tpu as pltpu

D_MODEL = 1024
WIDTH = D_MODEL
GRID_W = 64
N_MIXERS = 3
POOL_WINDOWS = (2, 4, 8, 16)
POOL_GROUP = WIDTH // len(POOL_WINDOWS)
HEAD_DIM = 64
N_HEADS = WIDTH // HEAD_DIM
WIN_ROWS = 8
WIN_COLS = 16
EPS = 1e-6

LANES = 128
SUBLANES = 8
BF16_SUBLANES = 16
MXU_TILE = 256
VMEM_LIMIT = 56 * 1024 * 1024

HALO = SUBLANES
N_HEAD_PAIRS = WIDTH // LANES
MASK_VALUE = -1e30
LOG2E = math.log2(math.e)

X_TILE = 1024
CTX_TILE = 256
ATT_ROWS = 8
SLAB_ROWS = WIN_ROWS + 2
PIPE_SLOTS = 2
BIAS_HEADS_PER_STEP = 8
CONV_BLOCK = MXU_TILE
COND_ROWS = BF16_SUBLANES

F32 = jnp.float32
BF16 = jnp.bfloat16


def _silu(x):
    return x / (1.0 + jnp.exp(-x))


def _mod_norm(x, norm_g, shift, scale):
    ms = jnp.mean(x * x, axis=-1, keepdims=True)
    return (x * lax.rsqrt(ms + EPS)) * (norm_g * (1.0 + scale)) + shift


def _params(n_axes):
    return pltpu.CompilerParams(
        dimension_semantics=("arbitrary",) * n_axes, vmem_limit_bytes=VMEM_LIMIT)


def _layer_block(arr, j, col=None):
    shape = (1,) + arr.shape[1:]
    index = (j,) + (0,) * (arr.ndim - 1)
    if col is not None:
        shape = shape[:-1] + (WIDTH,)
        index = index[:-1] + (col,)
    return pl.BlockSpec(shape, lambda *_: index, pipeline_mode=pl.Buffered(1))


def _mod_block(mods, layer, row):
    d = mods.shape[-1]
    if row is None:
        return pl.BlockSpec((1, 1, 3, d), lambda bi, i: (layer, bi, 0, 0))
    return pl.BlockSpec((1, 1, 3, d), lambda bi, i: (layer, row, 0, 0))


def _modulation_kernel(cond_ref, w_ref, b_ref, o_ref):
    s = _silu(cond_ref[...]).astype(BF16)
    o_ref[0] = jnp.dot(s, w_ref[0].astype(BF16), preferred_element_type=F32) + b_ref[0]


def _modulation(cond, ada_w, ada_b):
    depth, d, n = ada_w.shape
    rows = cond.shape[0]
    tn = n
    return pl.pallas_call(
        _modulation_kernel,
        grid=(depth, n // tn),
        in_specs=[
            pl.BlockSpec((rows, d), lambda i, j: (0, 0)),
            pl.BlockSpec((1, d, tn), lambda i, j: (i, 0, j)),
            pl.BlockSpec((1, 1, tn), lambda i, j: (i, 0, j)),
        ],
        out_specs=pl.BlockSpec((1, rows, tn), lambda i, j: (i, 0, j)),
        out_shape=jax.ShapeDtypeStruct((depth, rows, n), F32),
        compiler_params=_params(2),
        name="modulation",
    )(cond, ada_w, ada_b.reshape(depth, 1, n))


def _normed_rows(x_ref, xp_ref, xn_ref, mod_ref, ng_ref):
    shift, scale = mod_ref[0, 0, 0:1, :], mod_ref[0, 0, 1:2, :]
    xe = jnp.concatenate([xp_ref[0], x_ref[0], xn_ref[0]], axis=0)
    he = _mod_norm(xe, ng_ref[0], shift, scale)
    return he.astype(BF16), he[HALO:he.shape[0] - HALO].astype(BF16)


def _zero_outside_sequence(e, tile):
    i = pl.program_id(1)
    keep_prev = (i > 0).astype(F32)
    keep_next = (i < pl.num_programs(1) - 1).astype(F32)
    return jnp.concatenate(
        [e[:HALO] * keep_prev, e[HALO:HALO + tile], e[HALO + tile:] * keep_next], axis=0)


def _window_sum(e, width, tile):
    n = e.shape[0]
    half = width // 2
    f = e
    k = 1
    while k < half:
        f = f + pltpu.roll(f, n - k, axis=0)
        k *= 2
    if half == HALO:
        return f[:tile] + f[HALO:HALO + tile]
    return (pltpu.roll(f, half, axis=0) + f)[HALO:HALO + tile]


def _window_mean(ws, width, tile, seq_len):
    half = width // 2
    row = lax.broadcasted_iota(jnp.int32, (HALO, 1), 0)

    def inv_count(first_row):
        t = row + (pl.program_id(1) * tile + first_row)
        cnt = jnp.minimum(t + half, seq_len) - jnp.maximum(t - half, 0)
        return 1.0 / cnt.astype(F32)

    return jnp.concatenate([ws[:HALO] * inv_count(0),
                            ws[HALO:tile - HALO] * (1.0 / width),
                            ws[tile - HALO:] * inv_count(tile - HALO)], axis=0)


def _finish(x, y, gate, o_ref, fg_ref):
    out = x + gate * y
    if fg_ref is not None:
        ms = jnp.mean(out * out, axis=-1, keepdims=True)
        out = (out * lax.rsqrt(ms + EPS)) * fg_ref[...]
    o_ref[0] = out


def _cast_blocks(arr, n_inner, n_steps):
    layers, rows, cols = arr.shape
    if cols % (n_steps * LANES) == 0:
        shape = (layers, rows, cols // n_steps)
        index = lambda bi, i: (0, 0, bi * n_inner + i)
    else:
        assert rows % (n_steps * BF16_SUBLANES) == 0
        shape = (layers, rows // n_steps, cols)
        index = lambda bi, i: (0, bi * n_inner + i, 0)
    return (pl.BlockSpec(shape, index), pl.BlockSpec(shape, index),
            jax.ShapeDtypeStruct(arr.shape, BF16))


def _run_casts(cast_refs):
    n = len(cast_refs) // 2
    for src, dst in zip(cast_refs[:n], cast_refs[n:]):
        dst[...] = src[...].astype(BF16)


def _fold_pool_kernel(win_ref, wgrp_ref, ps_ref, wout_in_ref, wu_ref, wg_ref, wout_ref):
    for gi in range(len(POOL_WINDOWS)):
        cols = slice(gi * POOL_GROUP, (gi + 1) * POOL_GROUP)
        folded = jnp.dot(win_ref[0, :, cols], wgrp_ref[0, gi],
                         preferred_element_type=F32, precision=lax.Precision.HIGHEST)
        wu_ref[0, :, cols] = (folded * ps_ref[0, :, cols]).astype(BF16)
    wg_ref[0] = win_ref[0, :, WIDTH:].astype(BF16)
    wout_ref[0] = wout_in_ref[0].astype(BF16)


def _fold_pool_weights(w_in, w_grp, scale, w_out):
    n, d, _ = w_in.shape
    out = jax.ShapeDtypeStruct((n, d, WIDTH), BF16)
    return pl.pallas_call(
        _fold_pool_kernel,
        grid=(n,),
        in_specs=[pl.BlockSpec((1,) + w_in.shape[1:], lambda i: (i, 0, 0)),
                  pl.BlockSpec((1,) + w_grp.shape[1:], lambda i: (i, 0, 0, 0)),
                  pl.BlockSpec((1, 1, WIDTH), lambda i: (i, 0, 0)),
                  pl.BlockSpec((1,) + w_out.shape[1:], lambda i: (i, 0, 0))],
        out_specs=[pl.BlockSpec((1, d, WIDTH), lambda i: (i, 0, 0))] * 2
        + [pl.BlockSpec((1,) + w_out.shape[1:], lambda i: (i, 0, 0))],
        out_shape=[out, out, jax.ShapeDtypeStruct(w_out.shape, BF16)],
        compiler_params=_params(1),
        name="fold_pool_weights",
    )(w_in, w_grp, scale.reshape(n, 1, WIDTH), w_out)


def _split_rest(rest, n_casts):
    has_gain = len(rest) - 1 - 2 * n_casts
    fg_ref = rest[0] if has_gain else None
    cast_refs = rest[has_gain:has_gain + n_casts] + rest[has_gain + n_casts + 1:]
    return fg_ref, rest[has_gain + n_casts], cast_refs


def _pool_kernel(x_ref, xp_ref, xn_ref, mod_ref, ng_ref, wu_ref, wg_ref, wout_ref,
                 *rest, tile, seq_len, n_casts):
    fg_ref, o_ref, cast_refs = _split_rest(rest, n_casts)
    _run_casts(cast_refs)
    he, hm = _normed_rows(x_ref, xp_ref, xn_ref, mod_ref, ng_ref)
    t = jnp.dot(he, wu_ref[0], preferred_element_type=F32)
    t = _zero_outside_sequence(t, tile)
    g = jnp.dot(hm, wg_ref[0], preferred_element_type=F32)
    mixed = []
    for gi, width in enumerate(POOL_WINDOWS):
        te = t[:, gi * POOL_GROUP:(gi + 1) * POOL_GROUP]
        pooled = _window_mean(_window_sum(te, width, tile), width, tile, seq_len)
        mixed.append(pooled - te[HALO:HALO + tile])
    mixed = jnp.concatenate(mixed, axis=1)
    z = (mixed * _silu(g)).astype(BF16)
    y = jnp.dot(z, wout_ref[0], preferred_element_type=F32)
    _finish(x_ref[0], y, mod_ref[0, 0, 2:3, :], o_ref, fg_ref)


def _conv_kernel(x_ref, xp_ref, xn_ref, mod_ref, ng_ref, win_ref, dw_ref, db_ref, wout_ref,
                 *rest, tile, seq_len, n_casts):
    fg_ref, o_ref, cast_refs = _split_rest(rest, n_casts)
    _run_casts(cast_refs)
    he, hm = _normed_rows(x_ref, xp_ref, xn_ref, mod_ref, ng_ref)
    n = tile + 2 * HALO
    rows = slice(HALO, HALO + tile)
    y = None
    for ci in range(WIDTH // CONV_BLOCK):
        cols = slice(ci * CONV_BLOCK, (ci + 1) * CONV_BLOCK)

        def proj(h, part):
            w = win_ref[0, :, part * WIDTH + ci * CONV_BLOCK:part * WIDTH + (ci + 1) * CONV_BLOCK]
            return jnp.dot(h, w, preferred_element_type=F32)

        z = _zero_outside_sequence(proj(he, 1) * proj(he, 2), tile)
        conv = (dw_ref[0, 0:1, cols] * pltpu.roll(z, 1, axis=0)[rows]
                + dw_ref[0, 1:2, cols] * z[rows]
                + dw_ref[0, 2:3, cols] * pltpu.roll(z, n - 1, axis=0)[rows]
                + db_ref[0, :, cols])
        yc = (proj(hm, 0) * conv * _silu(proj(hm, 3))).astype(BF16)
        part = jnp.dot(yc, wout_ref[0, cols, :], preferred_element_type=F32)
        y = part if y is None else y + part
    _finish(x_ref[0], y, mod_ref[0, 0, 2:3, :], o_ref, fg_ref)


def _mixer_layer(kernel_fn, x, mods, layer, mod_row, norm_g, weights, j, final_g, tile, name,
                 casts=()):
    b, seq_len, d = x.shape
    tile = min(tile, seq_len)
    halo_blocks = seq_len // HALO
    per_tile = tile // HALO
    n_inner = seq_len // tile
    cast_blocks = [_cast_blocks(a, n_inner, b * n_inner) for a in casts]
    in_specs = [
        pl.BlockSpec((1, tile, d), lambda bi, i: (bi, i, 0)),
        pl.BlockSpec((1, HALO, d), lambda bi, i: (bi, jnp.maximum(i * per_tile - 1, 0), 0)),
        pl.BlockSpec((1, HALO, d),
                     lambda bi, i: (bi, jnp.minimum((i + 1) * per_tile, halo_blocks - 1), 0)),
        _mod_block(mods, layer, mod_row),
        _layer_block(norm_g, layer),
    ] + [_layer_block(w, j) for w in weights]
    args = [x, x, x, mods, norm_g] + list(weights)
    if final_g is not None:
        in_specs.append(pl.BlockSpec((1, d), lambda bi, i: (0, 0)))
        args.append(final_g.reshape(1, d))
    outs = pl.pallas_call(
        functools.partial(kernel_fn, tile=tile, seq_len=seq_len, n_casts=len(casts)),
        grid=(b, n_inner),
        in_specs=in_specs + [blk[0] for blk in cast_blocks],
        out_specs=[pl.BlockSpec((1, tile, d), lambda bi, i: (bi, i, 0))]
        + [blk[1] for blk in cast_blocks],
        out_shape=[jax.ShapeDtypeStruct(x.shape, F32)] + [blk[2] for blk in cast_blocks],
        compiler_params=_params(2),
        name=name,
    )(*args, *casts)
    return outs if casts else outs[0]


def _project_kernel(x_ref, mod_ref, ng_ref, *refs, scales, layouts):
    n = len(scales)
    n_casts = (len(refs) - 2 * n) // 2
    w_refs, o_refs = refs[:n], refs[n + n_casts:2 * n + n_casts]
    _run_casts(refs[n:n + n_casts] + refs[2 * n + n_casts:])
    shift, scale = mod_ref[0, 0, 0:1, :], mod_ref[0, 0, 1:2, :]
    h = _mod_norm(x_ref[0], ng_ref[0], shift, scale).astype(BF16)
    for w_ref, o_ref, s, layout in zip(w_refs, o_refs, scales, layouts):
        p = jnp.dot(h, w_ref[0], preferred_element_type=F32)
        if s != 1.0:
            p = p * s
        if layout == "flat":
            o_ref[0] = p.astype(o_ref.dtype)
            continue
        for hp in range(N_HEAD_PAIRS):
            block = p[:, hp * LANES:(hp + 1) * LANES]
            o_ref[0, hp] = (block.T if layout == "pairs_t" else block).astype(o_ref.dtype)


def _project(x, mods, layer, mod_row, norm_g, w, j, cols, layouts, scales, tile, name,
             casts=()):
    b, seq_len, d = x.shape
    tile = min(tile, seq_len)
    assert tile % LANES == 0
    specs = {
        "flat": (pl.BlockSpec((1, tile, WIDTH), lambda bi, i: (bi, i, 0)),
                 jax.ShapeDtypeStruct((b, seq_len, WIDTH), F32)),
        "pairs": (pl.BlockSpec((1, N_HEAD_PAIRS, tile, LANES), lambda bi, i: (bi, 0, i, 0)),
                  jax.ShapeDtypeStruct((b, N_HEAD_PAIRS, seq_len, LANES), BF16)),
        "pairs_t": (pl.BlockSpec((1, N_HEAD_PAIRS, LANES, tile), lambda bi, i: (bi, 0, 0, i)),
                    jax.ShapeDtypeStruct((b, N_HEAD_PAIRS, LANES, seq_len), BF16)),
    }
    n_inner = seq_len // tile
    cast_blocks = [_cast_blocks(a, n_inner, b * n_inner) for a in casts]
    out_specs = [specs[name_][0] for name_ in layouts] + [blk[1] for blk in cast_blocks]
    out_shape = [specs[name_][1] for name_ in layouts] + [blk[2] for blk in cast_blocks]
    return pl.pallas_call(
        functools.partial(_project_kernel, scales=scales, layouts=layouts),
        grid=(b, n_inner),
        in_specs=[
            pl.BlockSpec((1, tile, d), lambda bi, i: (bi, i, 0)),
            _mod_block(mods, layer, mod_row),
            _layer_block(norm_g, layer),
        ] + [_layer_block(w, j, col) for col in cols] + [blk[0] for blk in cast_blocks],
        out_specs=out_specs,
        out_shape=out_shape,
        compiler_params=_params(2),
        name=name,
    )(x, mods, norm_g, *([w] * len(cols)), *casts)


SLAB_TILES = SLAB_ROWS // 2


def _slab_base(pair_row0, n_rows):
    return jnp.clip(pair_row0 - WIN_ROWS // 2, 0, n_rows - SLAB_ROWS)


def _bias_tile_plan(n_rows):
    tiles, ids = [], []
    for r0 in range(0, n_rows, 2):
        base = int(np.clip(r0 - WIN_ROWS // 2, 0, n_rows - SLAB_ROWS))
        for j in range(SLAB_TILES):
            quad = []
            for key_row in (base + 2 * j, base + 2 * j + 1):
                for r in (r0, r0 + 1):
                    start = int(np.clip(r - WIN_ROWS // 2, 0, n_rows - WIN_ROWS))
                    assert base <= start and start + WIN_ROWS <= base + SLAB_ROWS
                    inside = start <= key_row < start + WIN_ROWS
                    quad.append(key_row - r + WIN_ROWS - 1 if inside else None)
            quad = tuple(quad)
            if quad not in tiles:
                tiles.append(quad)
            ids.append(tiles.index(quad))
    return tiles, np.asarray(ids, np.int32)


def _bias_tiles_kernel(w_ref, o_ref, *, tiles):
    k_col = lax.broadcasted_iota(jnp.int32, (GRID_W, LANES), 0)
    lane = lax.broadcasted_iota(jnp.int32, (GRID_W, LANES), 1)
    q_col = lane % GRID_W
    c_start = jnp.clip(q_col - WIN_COLS // 2, 0, GRID_W - WIN_COLS)
    inside = (k_col >= c_start) & (k_col < c_start + WIN_COLS)
    first_query_row = lane < GRID_W
    masked = jnp.full((GRID_W, LANES), MASK_VALUE, F32)

    def toeplitz(head, d, shift):
        if d is None:
            return masked
        row = jnp.broadcast_to(w_ref[head, d:d + 1, :], (GRID_W, LANES))
        return pltpu.roll(row, shift, axis=1, stride=1, stride_axis=0) * LOG2E

    for head in range(w_ref.shape[0]):
        for tile_id, quad in enumerate(tiles):
            halves = [jnp.where(inside,
                                jnp.where(first_query_row, toeplitz(head, quad[2 * kr], 0),
                                          toeplitz(head, quad[2 * kr + 1], GRID_W)),
                                masked) for kr in range(2)]
            o_ref[head, tile_id] = jnp.concatenate(halves, axis=0)


def _attention_bias(rpb, tiles):
    h, n_dr, n_dc = rpb.shape
    assert n_dr == 2 * WIN_ROWS - 1 and n_dc == 2 * WIN_COLS - 1
    rev = rpb[..., ::-1]
    w = jnp.concatenate([rev[..., WIN_COLS - 1:], jnp.zeros((h, n_dr, LANES - n_dc), F32),
                         rev[..., :WIN_COLS - 1]], axis=-1)
    w = jnp.pad(w, ((0, 0), (0, 2 * WIN_ROWS - n_dr), (0, 0)))
    return pl.pallas_call(
        functools.partial(_bias_tiles_kernel, tiles=tiles),
        grid=(h // BIAS_HEADS_PER_STEP,),
        in_specs=[pl.BlockSpec((BIAS_HEADS_PER_STEP, 2 * WIN_ROWS, LANES), lambda i: (i, 0, 0))],
        out_specs=pl.BlockSpec((BIAS_HEADS_PER_STEP, len(tiles), LANES, LANES),
                               lambda i: (i, 0, 0, 0)),
        out_shape=jax.ShapeDtypeStruct((h, len(tiles), LANES, LANES), F32),
        compiler_params=_params(1),
        name="bias_tiles",
    )(w)


def _attention_kernel(ids_ref, q_ref, k_ref, v_ref, kc_ref, vc_ref, g_ref, x_ref, mod_ref,
                      bias_ref, wout_ref, o_ref, s_scr, p_scr, r_scr, o_scr, *, n_rows):
    pair_tokens = 2 * GRID_W
    n_ctx = kc_ref.shape[2]
    slab_keys = SLAB_ROWS * GRID_W
    n_items = (ATT_ROWS // 2) * N_HEAD_PAIRS
    first_head = lax.broadcasted_iota(jnp.int32, (LANES, pair_tokens), 0) < HEAD_DIM

    def locate(item):
        pb, hp = item // N_HEAD_PAIRS, item % N_HEAD_PAIRS
        pair = pl.program_id(1) * (ATT_ROWS // 2) + pb
        k_off = pl.multiple_of(_slab_base(2 * pair, n_rows) * GRID_W, pair_tokens)
        return hp, pair, pb * pair_tokens, k_off

    def scores(item, slot):
        hp, _, q_off, k_off = locate(item)
        qt = q_ref[0, hp, :, pl.ds(q_off, pair_tokens)]
        zero = jnp.zeros_like(qt)
        q2 = jnp.concatenate([jnp.where(first_head, qt, zero),
                              jnp.where(first_head, zero, qt)], axis=1)
        s_scr[slot, :n_ctx, :] = jnp.dot(kc_ref[0, hp], q2, preferred_element_type=F32)
        s_scr[slot, n_ctx:, :] = jnp.dot(k_ref[0, hp, pl.ds(k_off, slab_keys), :], q2,
                                         preferred_element_type=F32)

    def softmax(item, slot, a):
        hp, pair, _, _ = locate(item)
        cols = slice(a * pair_tokens, (a + 1) * pair_tokens)
        bias = jnp.concatenate([bias_ref[2 * hp + a, ids_ref[pair * SLAB_TILES + j]]
                                for j in range(SLAB_TILES)], axis=0)
        s_ctx = s_scr[slot, :n_ctx, cols]
        s_loc = s_scr[slot, n_ctx:, cols] + bias
        m = jnp.maximum(jnp.max(s_loc, axis=0, keepdims=True),
                        jnp.max(s_ctx, axis=0, keepdims=True))
        p_ctx = jnp.exp2(s_ctx - m)
        p_loc = jnp.exp2(s_loc - m)
        denom = jnp.sum(p_loc, axis=0, keepdims=True) + jnp.sum(p_ctx, axis=0, keepdims=True)
        p_scr[slot, :n_ctx, cols] = p_ctx.astype(BF16)
        p_scr[slot, n_ctx:, cols] = p_loc.astype(BF16)
        r_scr[slot, :, cols] = jnp.broadcast_to(1.0 / denom, (SUBLANES, pair_tokens))

    def values(item, slot):
        hp, _, q_off, k_off = locate(item)
        pv = (jnp.dot(vc_ref[0, hp], p_scr[slot, :n_ctx, :], preferred_element_type=F32)
              + jnp.dot(v_ref[0, hp, :, pl.ds(k_off, slab_keys)], p_scr[slot, n_ctx:, :],
                        preferred_element_type=F32))
        pv = pv * r_scr[slot, 0:1, :]
        out = jnp.where(first_head, pv[:, :pair_tokens], pv[:, pair_tokens:])
        o_scr[hp, pl.ds(q_off, pair_tokens), :] = out.T

    for t in range(-2, n_items):
        if t + 2 < n_items:
            scores(t + 2, (t + 2) % PIPE_SLOTS)
        if 0 <= t + 1 < n_items:
            softmax(t + 1, (t + 1) % PIPE_SLOTS, 0)
        if t >= 0:
            values(t, t % PIPE_SLOTS)
        if 0 <= t + 1 < n_items:
            softmax(t + 1, (t + 1) % PIPE_SLOTS, 1)

    gate = mod_ref[0, 0, 2:3, :]
    o = jnp.concatenate([o_scr[hp] for hp in range(N_HEAD_PAIRS)], axis=1)
    y = (o * _silu(g_ref[0])).astype(BF16)
    y = jnp.dot(y, wout_ref[0], preferred_element_type=F32)
    o_ref[0] = x_ref[0] + gate * y


def _attention_layer(x, q, k, v, kc, vc, g, mods, layer, rpb, w_out, j):
    b, seq_len, d = x.shape
    n_ctx = kc.shape[2] // b
    n_rows = seq_len // GRID_W
    assert n_rows % ATT_ROWS == 0 and n_rows >= SLAB_ROWS and ATT_ROWS % 2 == 0
    tile = ATT_ROWS * GRID_W
    n_keys = n_ctx + SLAB_ROWS * GRID_W
    tiles, ids = _bias_tile_plan(n_rows)
    bias = _attention_bias(rpb, tiles)
    tile_spec = pl.BlockSpec((1, tile, d), lambda bi, i: (bi, i, 0))
    q_spec = pl.BlockSpec((1, N_HEAD_PAIRS, LANES, tile), lambda bi, i: (bi, 0, 0, i))
    seq_spec = pl.BlockSpec((1, N_HEAD_PAIRS, seq_len, LANES), lambda bi, i: (bi, 0, 0, 0))
    ctx_spec = pl.BlockSpec((1, N_HEAD_PAIRS, n_ctx, LANES), lambda bi, i: (0, 0, bi, 0))
    seq_t_spec = pl.BlockSpec((1, N_HEAD_PAIRS, LANES, seq_len), lambda bi, i: (bi, 0, 0, 0))
    ctx_t_spec = pl.BlockSpec((1, N_HEAD_PAIRS, LANES, n_ctx), lambda bi, i: (0, 0, 0, bi))
    return pl.pallas_call(
        functools.partial(_attention_kernel, n_rows=n_rows),
        grid=(b, n_rows // ATT_ROWS),
        in_specs=[pl.BlockSpec(memory_space=pltpu.SMEM),
                  q_spec, seq_spec, seq_t_spec, ctx_spec, ctx_t_spec, tile_spec, tile_spec,
                  _mod_block(mods, layer, None),
                  pl.BlockSpec(bias.shape, lambda bi, i: (0, 0, 0, 0),
                               pipeline_mode=pl.Buffered(1)),
                  _layer_block(w_out, j)],
        out_specs=tile_spec,
        out_shape=jax.ShapeDtypeStruct(x.shape, F32),
        scratch_shapes=[pltpu.VMEM((PIPE_SLOTS, n_keys, 4 * GRID_W), F32),
                        pltpu.VMEM((PIPE_SLOTS, n_keys, 4 * GRID_W), BF16),
                        pltpu.VMEM((PIPE_SLOTS, SUBLANES, 4 * GRID_W), F32),
                        pltpu.VMEM((N_HEAD_PAIRS, tile, LANES), F32)],
        compiler_params=_params(2),
        name="na_attention",
    )(jnp.asarray(ids), q, k, v, kc, vc, g, x, mods, bias, w_out)


def kernel(x, c, ctx, c_ctx, norm_g, ada_w, ada_b, pool_w_in, pool_w_grp, pool_scale, pool_w_out,
           na_w_in, na_rpb, na_w_out, conv_w_in, conv_dw, conv_db, conv_w_out, final_g):
    depth = norm_g.shape[0]
    batch, _, d = x.shape
    assert batch < COND_ROWS and WIN_ROWS // 2 <= HALO

    cond = jnp.zeros((COND_ROWS, d), F32).at[:batch].set(c).at[batch].set(c_ctx)
    mods = _modulation(cond, ada_w, ada_b).reshape(depth, COND_ROWS, 3, d)
    norm_g = norm_g.reshape(depth, 1, d)

    f32_weights = {1: (na_w_in, na_w_out), 2: (conv_w_in, conv_w_out)}
    bf16_weights = {0: _fold_pool_weights(pool_w_in, pool_w_grp, pool_scale, pool_w_out)}

    def weights_of(kind):
        if kind not in bf16_weights:
            bf16_weights[kind] = tuple(w.astype(BF16) for w in f32_weights[kind])
        return bf16_weights[kind]

    def casts_for_next(i):
        kind = (i + 1) % N_MIXERS
        return f32_weights[kind] if i + 1 < depth and kind not in bf16_weights else ()

    last_ctx_reader = max([i for i in range(depth) if i % N_MIXERS == 1], default=-1)
    for i in range(depth):
        kind, j = i % N_MIXERS, i // N_MIXERS
        update_ctx = i < last_ctx_reader
        fg = final_g if i == depth - 1 else None
        w_bf16 = weights_of(kind)
        casts = casts_for_next(i)
        if kind != 1:
            if kind == 0:
                kernel_fn, weights = _pool_kernel, w_bf16
            else:
                kernel_fn = _conv_kernel
                weights = (w_bf16[0], conv_dw, conv_db.reshape(-1, 1, WIDTH), w_bf16[1])
            run = functools.partial(_mixer_layer, kernel_fn, mods=mods, layer=i, norm_g=norm_g,
                                    weights=weights, j=j)
            if update_ctx:
                ctx = run(x=ctx, mod_row=batch, final_g=None, tile=CTX_TILE, name=f"ctx_layer{i}")
            x = run(x=x, mod_row=None, final_g=fg, tile=X_TILE, name=f"layer{i}", casts=casts)
            if casts:
                x, *cast_out = x
        else:
            if update_ctx:
                raise NotImplementedError("context output of a neighbourhood-attention layer")
            if fg is not None:
                raise NotImplementedError("final norm after a neighbourhood-attention layer")
            w_in, w_out = w_bf16
            q, k, v, g, *cast_out = _project(
                x, mods, i, None, norm_g, w_in, j, (0, 1, 2, 3),
                ("pairs_t", "pairs", "pairs_t", "flat"),
                (HEAD_DIM ** -0.5 * LOG2E, 1.0, 1.0, 1.0), X_TILE, f"na_project{i}", casts=casts)
            kc, vc = _project(ctx.reshape(1, -1, d), mods, i, batch, norm_g, w_in, j, (1, 2),
                              ("pairs", "pairs_t"), (1.0, 1.0), X_TILE, f"na_ctx_project{i}")
            x = _attention_layer(x, q, k, v, kc, vc, g, mods, i, na_rpb[j], w_out, j)
        if casts:
            bf16_weights[(i + 1) % N_MIXERS] = tuple(cast_out)
    return x
```

```python
import functools
import math

import numpy as np

import jax
import jax.numpy as jnp
from jax import lax
from jax.experimental import pallas as pl
from jax.experimental.pallas import tpu as pltpu

D_MODEL = 1024
WIDTH = D_MODEL
GRID_W = 64
N_MIXERS = 3
POOL_WINDOWS = (2, 4, 8, 16)
POOL_GROUP = WIDTH // len(POOL_WINDOWS)
HEAD_DIM = 64
N_HEADS = WIDTH // HEAD_DIM
WIN_ROWS = 8
WIN_COLS = 16
EPS = 1e-6

LANES = 128
SUBLANES = 8
BF16_SUBLANES = 16
MXU_TILE = 256
VMEM_LIMIT = 56 * 1024 * 1024

HALO = SUBLANES
N_HEAD_PAIRS = WIDTH // LANES
MASK_VALUE = -1e30
LOG2E = math.log2(math.e)

X_TILE = 1024
CTX_TILE = 256
ATT_ROWS = 8
SLAB_ROWS = WIN_ROWS + 2
PIPE_SLOTS = 2
BIAS_HEADS_PER_STEP = 8
CONV_BLOCK = MXU_TILE
COND_ROWS = BF16_SUBLANES

F32 = jnp.float32
BF16 = jnp.bfloat16


def _silu(x):
    return x / (1.0 + jnp.exp(-x))


def _mod_norm(x, norm_g, shift, scale):
    ms = jnp.mean(x * x, axis=-1, keepdims=True)
    return (x * lax.rsqrt(ms + EPS)) * (norm_g * (1.0 + scale)) + shift


def _params(n_axes):
    return pltpu.CompilerParams(
        dimension_semantics=("arbitrary",) * n_axes, vmem_limit_bytes=VMEM_LIMIT)


def _layer_block(arr, j, col=None, width=WIDTH):
    shape = (1,) + arr.shape[1:]
    index = (j,) + (0,) * (arr.ndim - 1)
    if col is not None:
        shape = shape[:-1] + (width,)
        index = index[:-1] + (col,)
    return pl.BlockSpec(shape, lambda *_: index, pipeline_mode=pl.Buffered(1))


def _mod_block(mods, layer, row):
    d = mods.shape[-1]
    if row is None:
        return pl.BlockSpec((1, 1, 3, d), lambda bi, i: (layer, bi, 0, 0))
    return pl.BlockSpec((1, 1, 3, d), lambda bi, i: (layer, row, 0, 0))


def _modulation_kernel(cond_ref, w_ref, b_ref, o_ref):
    s = _silu(cond_ref[...]).astype(BF16)
    o_ref[0] = jnp.dot(s, w_ref[0].astype(BF16), preferred_element_type=F32) + b_ref[0]


def _modulation(cond, ada_w, ada_b):
    depth, d, n = ada_w.shape
    rows = cond.shape[0]
    tn = n
    return pl.pallas_call(
        _modulation_kernel,
        grid=(depth, n // tn),
        in_specs=[
            pl.BlockSpec((rows, d), lambda i, j: (0, 0)),
            pl.BlockSpec((1, d, tn), lambda i, j: (i, 0, j)),
            pl.BlockSpec((1, 1, tn), lambda i, j: (i, 0, j)),
        ],
        out_specs=pl.BlockSpec((1, rows, tn), lambda i, j: (i, 0, j)),
        out_shape=jax.ShapeDtypeStruct((depth, rows, n), F32),
        compiler_params=_params(2),
        name="modulation",
    )(cond, ada_w, ada_b.reshape(depth, 1, n))


def _normed_rows(x_ref, xp_ref, xn_ref, mod_ref, ng_ref):
    shift, scale = mod_ref[0, 0, 0:1, :], mod_ref[0, 0, 1:2, :]
    xe = jnp.concatenate([xp_ref[0], x_ref[0], xn_ref[0]], axis=0)
    he = _mod_norm(xe, ng_ref[0], shift, scale)
    return he.astype(BF16), he[HALO:he.shape[0] - HALO].astype(BF16)


def _zero_outside_sequence(e, tile):
    i = pl.program_id(1)
    keep_prev = (i > 0).astype(F32)
    keep_next = (i < pl.num_programs(1) - 1).astype(F32)
    return jnp.concatenate(
        [e[:HALO] * keep_prev, e[HALO:HALO + tile], e[HALO + tile:] * keep_next], axis=0)


def _window_sum(e, width, tile):
    n = e.shape[0]
    half = width // 2
    f = e
    k = 1
    while k < half:
        f = f + pltpu.roll(f, n - k, axis=0)
        k *= 2
    if half == HALO:
        return f[:tile] + f[HALO:HALO + tile]
    return (pltpu.roll(f, half, axis=0) + f)[HALO:HALO + tile]


def _window_mean(ws, width, tile, seq_len):
    half = width // 2
    row = lax.broadcasted_iota(jnp.int32, (HALO, 1), 0)

    def inv_count(first_row):
        t = row + (pl.program_id(1) * tile + first_row)
        cnt = jnp.minimum(t + half, seq_len) - jnp.maximum(t - half, 0)
        return 1.0 / cnt.astype(F32)

    return jnp.concatenate([ws[:HALO] * inv_count(0),
                            ws[HALO:tile - HALO] * (1.0 / width),
                            ws[tile - HALO:] * inv_count(tile - HALO)], axis=0)


def _finish(x, y, gate, o_ref, fg_ref):
    out = x + gate * y
    if fg_ref is not None:
        ms = jnp.mean(out * out, axis=-1, keepdims=True)
        out = (out * lax.rsqrt(ms + EPS)) * fg_ref[...]
    o_ref[0] = out


def _cast_blocks(arr, n_inner, n_steps):
    layers, rows, cols = arr.shape
    if cols % (n_steps * LANES) == 0:
        shape = (layers, rows, cols // n_steps)
        index = lambda bi, i: (0, 0, bi * n_inner + i)
    else:
        assert rows % (n_steps * BF16_SUBLANES) == 0
        shape = (layers, rows // n_steps, cols)
        index = lambda bi, i: (0, bi * n_inner + i, 0)
    return (pl.BlockSpec(shape, index), pl.BlockSpec(shape, index),
            jax.ShapeDtypeStruct(arr.shape, BF16))


def _run_casts(cast_refs):
    n = len(cast_refs) // 2
    for src, dst in zip(cast_refs[:n], cast_refs[n:]):
        dst[...] = src[...].astype(BF16)


def _fold_pool_kernel(win_ref, wgrp_ref, ps_ref, wout_in_ref, wu_ref, wg_ref, wout_ref):
    for gi in range(len(POOL_WINDOWS)):
        cols = slice(gi * POOL_GROUP, (gi + 1) * POOL_GROUP)
        folded = jnp.dot(win_ref[0, :, cols], wgrp_ref[0, gi],
                         preferred_element_type=F32, precision=lax.Precision.HIGHEST)
        wu_ref[0, :, cols] = (folded * ps_ref[0, :, cols]).astype(BF16)
    wg_ref[0] = win_ref[0, :, WIDTH:].astype(BF16)
    wout_ref[0] = wout_in_ref[0].astype(BF16)


def _fold_pool_weights(w_in, w_grp, scale, w_out):
    n, d, _ = w_in.shape
    out = jax.ShapeDtypeStruct((n, d, WIDTH), BF16)
    return pl.pallas_call(
        _fold_pool_kernel,
        grid=(n,),
        in_specs=[pl.BlockSpec((1,) + w_in.shape[1:], lambda i: (i, 0, 0)),
                  pl.BlockSpec((1,) + w_grp.shape[1:], lambda i: (i, 0, 0, 0)),
                  pl.BlockSpec((1, 1, WIDTH), lambda i: (i, 0, 0)),
                  pl.BlockSpec((1,) + w_out.shape[1:], lambda i: (i, 0, 0))],
        out_specs=[pl.BlockSpec((1, d, WIDTH), lambda i: (i, 0, 0))] * 2
        + [pl.BlockSpec((1,) + w_out.shape[1:], lambda i: (i, 0, 0))],
        out_shape=[out, out, jax.ShapeDtypeStruct(w_out.shape, BF16)],
        compiler_params=_params(1),
        name="fold_pool_weights",
    )(w_in, w_grp, scale.reshape(n, 1, WIDTH), w_out)


def _split_rest(rest, n_casts):
    has_gain = len(rest) - 1 - 2 * n_casts
    fg_ref = rest[0] if has_gain else None
    cast_refs = rest[has_gain:has_gain + n_casts] + rest[has_gain + n_casts + 1:]
    return fg_ref, rest[has_gain + n_casts], cast_refs


def _pool_kernel(x_ref, xp_ref, xn_ref, mod_ref, ng_ref, wu_ref, wg_ref, wout_ref,
                 *rest, tile, seq_len, n_casts):
    fg_ref, o_ref, cast_refs = _split_rest(rest, n_casts)
    _run_casts(cast_refs)
    he, hm = _normed_rows(x_ref, xp_ref, xn_ref, mod_ref, ng_ref)
    t = jnp.dot(he, wu_ref[0], preferred_element_type=F32)
    t = _zero_outside_sequence(t, tile)
    g = jnp.dot(hm, wg_ref[0], preferred_element_type=F32)
    mixed = []
    for gi, width in enumerate(POOL_WINDOWS):
        te = t[:, gi * POOL_GROUP:(gi + 1) * POOL_GROUP]
        pooled = _window_mean(_window_sum(te, width, tile), width, tile, seq_len)
        mixed.append(pooled - te[HALO:HALO + tile])
    mixed = jnp.concatenate(mixed, axis=1)
    z = (mixed * _silu(g)).astype(BF16)
    y = jnp.dot(z, wout_ref[0], preferred_element_type=F32)
    _finish(x_ref[0], y, mod_ref[0, 0, 2:3, :], o_ref, fg_ref)


def _conv_kernel(x_ref, xp_ref, xn_ref, mod_ref, ng_ref, *rest, tile, seq_len, n_casts):
    n_blocks = WIDTH // CONV_BLOCK
    win_refs, (dw_ref, db_ref, wout_ref) = rest[:4 * n_blocks], rest[4 * n_blocks:4 * n_blocks + 3]
    rest = rest[4 * n_blocks + 3:]
    fg_ref, o_ref, cast_refs = _split_rest(rest, n_casts)
    _run_casts(cast_refs)
    he, hm = _normed_rows(x_ref, xp_ref, xn_ref, mod_ref, ng_ref)
    n = tile + 2 * HALO
    rows = slice(HALO, HALO + tile)
    y = None
    for ci in range(n_blocks):
        cols = slice(ci * CONV_BLOCK, (ci + 1) * CONV_BLOCK)

        def proj(h, part):
            return jnp.dot(h, win_refs[part * n_blocks + ci][0], preferred_element_type=F32)

        z = _zero_outside_sequence(proj(he, 1) * proj(he, 2), tile)
        conv = (dw_ref[0, 0:1, cols] * pltpu.roll(z, 1, axis=0)[rows]
                + dw_ref[0, 1:2, cols] * z[rows]
                + dw_ref[0, 2:3, cols] * pltpu.roll(z, n - 1, axis=0)[rows]
                + db_ref[0, :, cols])
        yc = (proj(hm, 0) * conv * _silu(proj(hm, 3))).astype(BF16)
        part = jnp.dot(yc, wout_ref[0, cols, :], preferred_element_type=F32)
        y = part if y is None else y + part
    _finish(x_ref[0], y, mod_ref[0, 0, 2:3, :], o_ref, fg_ref)


def _mixer_layer(kernel_fn, x, mods, layer, mod_row, norm_g, weights, j, final_g, tile, name,
                 casts=()):
    b, seq_len, d = x.shape
    tile = min(tile, seq_len)
    halo_blocks = seq_len // HALO
    per_tile = tile // HALO
    n_inner = seq_len // tile
    cast_blocks = [_cast_blocks(a, n_inner, b * n_inner) for a in casts]
    in_specs = [
        pl.BlockSpec((1, tile, d), lambda bi, i: (bi, i, 0)),
        pl.BlockSpec((1, HALO, d), lambda bi, i: (bi, jnp.maximum(i * per_tile - 1, 0), 0)),
        pl.BlockSpec((1, HALO, d),
                     lambda bi, i: (bi, jnp.minimum((i + 1) * per_tile, halo_blocks - 1), 0)),
        _mod_block(mods, layer, mod_row),
        _layer_block(norm_g, layer),
    ] + [_layer_block(w[0], j, w[1], w[2]) if isinstance(w, tuple) else _layer_block(w, j)
         for w in weights]
    args = [x, x, x, mods, norm_g] + [w[0] if isinstance(w, tuple) else w for w in weights]
    if final_g is not None:
        in_specs.append(pl.BlockSpec((1, d), lambda bi, i: (0, 0)))
        args.append(final_g.reshape(1, d))
    outs = pl.pallas_call(
        functools.partial(kernel_fn, tile=tile, seq_len=seq_len, n_casts=len(casts)),
        grid=(b, n_inner),
        in_specs=in_specs + [blk[0] for blk in cast_blocks],
        out_specs=[pl.BlockSpec((1, tile, d), lambda bi, i: (bi, i, 0))]
        + [blk[1] for blk in cast_blocks],
        out_shape=[jax.ShapeDtypeStruct(x.shape, F32)] + [blk[2] for blk in cast_blocks],
        compiler_params=_params(2),
        name=name,
    )(*args, *casts)
    return outs if casts else outs[0]


def _project_kernel(x_ref, mod_ref, ng_ref, *refs, scales, layouts):
    n = len(scales)
    n_casts = (len(refs) - 2 * n) // 2
    w_refs, o_refs = refs[:n], refs[n + n_casts:2 * n + n_casts]
    _run_casts(refs[n:n + n_casts] + refs[2 * n + n_casts:])
    shift, scale = mod_ref[0, 0, 0:1, :], mod_ref[0, 0, 1:2, :]
    h = _mod_norm(x_ref[0], ng_ref[0], shift, scale).astype(BF16)
    for w_ref, o_ref, s, layout in zip(w_refs, o_refs, scales, layouts):
        p = jnp.dot(h, w_ref[0], preferred_element_type=F32)
        if s != 1.0:
            p = p * s
        if layout == "flat":
            o_ref[0] = p.astype(o_ref.dtype)
            continue
        for hp in range(N_HEAD_PAIRS):
            block = p[:, hp * LANES:(hp + 1) * LANES]
            if layout == "pairs":
                o_ref[0, hp] = block.astype(o_ref.dtype)
                continue
            for tb in range(block.shape[0] // LANES):
                o_ref[0, hp, tb] = block[tb * LANES:(tb + 1) * LANES].T.astype(o_ref.dtype)


def _project(x, mods, layer, mod_row, norm_g, w, j, cols, layouts, scales, tile, name,
             casts=()):
    b, seq_len, d = x.shape
    tile = min(tile, seq_len)
    assert tile % LANES == 0
    specs = {
        "flat": (pl.BlockSpec((1, tile, WIDTH), lambda bi, i: (bi, i, 0)),
                 jax.ShapeDtypeStruct((b, seq_len, WIDTH), F32)),
        "pairs": (pl.BlockSpec((1, N_HEAD_PAIRS, tile, LANES), lambda bi, i: (bi, 0, i, 0)),
                  jax.ShapeDtypeStruct((b, N_HEAD_PAIRS, seq_len, LANES), BF16)),
        "pairs_t": (pl.BlockSpec((1, N_HEAD_PAIRS, tile // LANES, LANES, LANES),
                                 lambda bi, i: (bi, 0, i, 0, 0)),
                    jax.ShapeDtypeStruct((b, N_HEAD_PAIRS, seq_len // LANES, LANES, LANES), BF16)),
    }
    n_inner = seq_len // tile
    cast_blocks = [_cast_blocks(a, n_inner, b * n_inner) for a in casts]
    out_specs = [specs[name_][0] for name_ in layouts] + [blk[1] for blk in cast_blocks]
    out_shape = [specs[name_][1] for name_ in layouts] + [blk[2] for blk in cast_blocks]
    return pl.pallas_call(
        functools.partial(_project_kernel, scales=scales, layouts=layouts),
        grid=(b, n_inner),
        in_specs=[
            pl.BlockSpec((1, tile, d), lambda bi, i: (bi, i, 0)),
            _mod_block(mods, layer, mod_row),
            _layer_block(norm_g, layer),
        ] + [_layer_block(w, j, col) for col in cols] + [blk[0] for blk in cast_blocks],
        out_specs=out_specs,
        out_shape=out_shape,
        compiler_params=_params(2),
        name=name,
    )(x, mods, norm_g, *([w] * len(cols)), *casts)


SLAB_TILES = SLAB_ROWS // 2


def _slab_base(pair_row0, n_rows):
    return jnp.clip(pair_row0 - WIN_ROWS // 2, 0, n_rows - SLAB_ROWS)


def _bias_tile_plan(n_rows):
    tiles, ids = [], []
    for r0 in range(0, n_rows, 2):
        base = int(np.clip(r0 - WIN_ROWS // 2, 0, n_rows - SLAB_ROWS))
        for j in range(SLAB_TILES):
            quad = []
            for key_row in (base + 2 * j, base + 2 * j + 1):
                for r in (r0, r0 + 1):
                    start = int(np.clip(r - WIN_ROWS // 2, 0, n_rows - WIN_ROWS))
                    assert base <= start and start + WIN_ROWS <= base + SLAB_ROWS
                    inside = start <= key_row < start + WIN_ROWS
                    quad.append(key_row - r + WIN_ROWS - 1 if inside else None)
            quad = tuple(quad)
            if quad not in tiles:
                tiles.append(quad)
            ids.append(tiles.index(quad))
    return tiles, np.asarray(ids, np.int32)


def _bias_tiles_kernel(w_ref, o_ref, *, tiles):
    k_col = lax.broadcasted_iota(jnp.int32, (GRID_W, LANES), 0)
    lane = lax.broadcasted_iota(jnp.int32, (GRID_W, LANES), 1)
    q_col = lane % GRID_W
    c_start = jnp.clip(q_col - WIN_COLS // 2, 0, GRID_W - WIN_COLS)
    inside = (k_col >= c_start) & (k_col < c_start + WIN_COLS)
    first_query_row = lane < GRID_W
    masked = jnp.full((GRID_W, LANES), MASK_VALUE, F32)

    def toeplitz(head, d, shift):
        if d is None:
            return masked
        row = jnp.broadcast_to(w_ref[head, d:d + 1, :], (GRID_W, LANES))
        return pltpu.roll(row, shift, axis=1, stride=1, stride_axis=0) * LOG2E

    for head in range(w_ref.shape[0]):
        for tile_id, quad in enumerate(tiles):
            halves = [jnp.where(inside,
                                jnp.where(first_query_row, toeplitz(head, quad[2 * kr], 0),
                                          toeplitz(head, quad[2 * kr + 1], GRID_W)),
                                masked) for kr in range(2)]
            o_ref[head, tile_id] = jnp.concatenate(halves, axis=0)


def _attention_bias(rpb, tiles):
    h, n_dr, n_dc = rpb.shape
    assert n_dr == 2 * WIN_ROWS - 1 and n_dc == 2 * WIN_COLS - 1
    rev = rpb[..., ::-1]
    w = jnp.concatenate([rev[..., WIN_COLS - 1:], jnp.zeros((h, n_dr, LANES - n_dc), F32),
                         rev[..., :WIN_COLS - 1]], axis=-1)
    w = jnp.pad(w, ((0, 0), (0, 2 * WIN_ROWS - n_dr), (0, 0)))
    return pl.pallas_call(
        functools.partial(_bias_tiles_kernel, tiles=tiles),
        grid=(h // BIAS_HEADS_PER_STEP,),
        in_specs=[pl.BlockSpec((BIAS_HEADS_PER_STEP, 2 * WIN_ROWS, LANES), lambda i: (i, 0, 0))],
        out_specs=pl.BlockSpec((BIAS_HEADS_PER_STEP, len(tiles), LANES, LANES),
                               lambda i: (i, 0, 0, 0)),
        out_shape=jax.ShapeDtypeStruct((h, len(tiles), LANES, LANES), F32),
        compiler_params=_params(1),
        name="bias_tiles",
    )(w)


def _attention_kernel(ids_ref, q_ref, k_ref, v_ref, kc_ref, vc_ref, g_ref, x_ref, mod_ref,
                      bias_ref, wout_ref, o_ref, s_scr, p_scr, r_scr, o_scr, *, n_rows):
    pair_tokens = 2 * GRID_W
    n_ctx = kc_ref.shape[2]
    slab_keys = SLAB_ROWS * GRID_W
    n_items = (ATT_ROWS // 2) * N_HEAD_PAIRS
    first_head = lax.broadcasted_iota(jnp.int32, (LANES, pair_tokens), 0) < HEAD_DIM

    def locate(item):
        pb, hp = item // N_HEAD_PAIRS, item % N_HEAD_PAIRS
        pair = pl.program_id(1) * (ATT_ROWS // 2) + pb
        k_off = pl.multiple_of(_slab_base(2 * pair, n_rows) * GRID_W, pair_tokens)
        return hp, pair, pb * pair_tokens, k_off

    def scores(item, slot):
        hp, _, q_off, k_off = locate(item)
        qt = q_ref[0, hp, q_off // pair_tokens]
        zero = jnp.zeros_like(qt)
        q2 = jnp.concatenate([jnp.where(first_head, qt, zero),
                              jnp.where(first_head, zero, qt)], axis=1)
        s_scr[slot, :n_ctx, :] = jnp.dot(kc_ref[0, hp], q2, preferred_element_type=F32)
        s_scr[slot, n_ctx:, :] = jnp.dot(k_ref[0, hp, pl.ds(k_off, slab_keys), :], q2,
                                         preferred_element_type=F32)

    def softmax(item, slot):
        hp, pair, _, _ = locate(item)
        bias = jnp.concatenate(
            [jnp.concatenate([bias_ref[2 * hp + a, ids_ref[pair * SLAB_TILES + j]]
                              for a in range(2)], axis=1)
             for j in range(SLAB_TILES)], axis=0)
        s_ctx = s_scr[slot, :n_ctx, :]
        s_loc = s_scr[slot, n_ctx:, :] + bias
        m = jnp.maximum(jnp.max(s_loc, axis=0, keepdims=True),
                        jnp.max(s_ctx, axis=0, keepdims=True))
        p_ctx = jnp.exp2(s_ctx - m)
        p_loc = jnp.exp2(s_loc - m)
        denom = jnp.sum(p_loc, axis=0, keepdims=True) + jnp.sum(p_ctx, axis=0, keepdims=True)
        p_scr[slot, :n_ctx, :] = p_ctx.astype(BF16)
        p_scr[slot, n_ctx:, :] = p_loc.astype(BF16)
        r_scr[slot] = jnp.broadcast_to(1.0 / denom, r_scr.shape[1:])

    def values(item, slot):
        hp, _, q_off, k_off = locate(item)
        vc = jnp.concatenate([vc_ref[0, hp, j] for j in range(n_ctx // LANES)], axis=1)
        v_block = lax.shift_right_logical(k_off, LANES.bit_length() - 1)
        vs = jnp.concatenate([v_ref[0, hp, v_block + j] for j in range(slab_keys // LANES)],
                             axis=1)
        pv = (jnp.dot(vc, p_scr[slot, :n_ctx, :], preferred_element_type=F32)
              + jnp.dot(vs, p_scr[slot, n_ctx:, :], preferred_element_type=F32))
        pv = pv * r_scr[slot, 0:1, :]
        out = jnp.where(first_head, pv[:, :pair_tokens], pv[:, pair_tokens:])
        o_scr[hp, pl.ds(q_off, pair_tokens), :] = out.T

    for t in range(-2, n_items):
        if t + 2 < n_items:
            scores(t + 2, (t + 2) % PIPE_SLOTS)
        if 0 <= t + 1 < n_items:
            softmax(t + 1, (t + 1) % PIPE_SLOTS)
        if t >= 0:
            values(t, t % PIPE_SLOTS)

    gate = mod_ref[0, 0, 2:3, :]
    o = jnp.concatenate([o_scr[hp] for hp in range(N_HEAD_PAIRS)], axis=1)
    y = (o * _silu(g_ref[0])).astype(BF16)
    y = jnp.dot(y, wout_ref[0], preferred_element_type=F32)
    o_ref[0] = x_ref[0] + gate * y


def _attention_layer(x, q, k, v, kc, vc, g, mods, layer, rpb, w_out, j):
    b, seq_len, d = x.shape
    n_ctx = kc.shape[2] // b
    n_rows = seq_len // GRID_W
    assert n_rows % ATT_ROWS == 0 and n_rows >= SLAB_ROWS and ATT_ROWS % 2 == 0
    tile = ATT_ROWS * GRID_W
    n_keys = n_ctx + SLAB_ROWS * GRID_W
    tiles, ids = _bias_tile_plan(n_rows)
    bias = _attention_bias(rpb, tiles)
    tile_spec = pl.BlockSpec((1, tile, d), lambda bi, i: (bi, i, 0))
    q_spec = pl.BlockSpec((1, N_HEAD_PAIRS, tile // LANES, LANES, LANES),
                          lambda bi, i: (bi, 0, i, 0, 0))
    seq_spec = pl.BlockSpec((1, N_HEAD_PAIRS, seq_len, LANES), lambda bi, i: (bi, 0, 0, 0))
    ctx_spec = pl.BlockSpec((1, N_HEAD_PAIRS, n_ctx, LANES), lambda bi, i: (0, 0, bi, 0))
    seq_t_spec = pl.BlockSpec((1, N_HEAD_PAIRS, seq_len // LANES, LANES, LANES),
                              lambda bi, i: (bi, 0, 0, 0, 0))
    ctx_t_spec = pl.BlockSpec((1, N_HEAD_PAIRS, n_ctx // LANES, LANES, LANES),
                              lambda bi, i: (0, 0, bi, 0, 0))
    return pl.pallas_call(
        functools.partial(_attention_kernel, n_rows=n_rows),
        grid=(b, n_rows // ATT_ROWS),
        in_specs=[pl.BlockSpec(memory_space=pltpu.SMEM),
                  q_spec, seq_spec, seq_t_spec, ctx_spec, ctx_t_spec, tile_spec, tile_spec,
                  _mod_block(mods, layer, None),
                  pl.BlockSpec(bias.shape, lambda bi, i: (0, 0, 0, 0),
                               pipeline_mode=pl.Buffered(1)),
                  _layer_block(w_out, j)],
        out_specs=tile_spec,
        out_shape=jax.ShapeDtypeStruct(x.shape, F32),
        scratch_shapes=[pltpu.VMEM((PIPE_SLOTS, n_keys, 4 * GRID_W), F32),
                        pltpu.VMEM((PIPE_SLOTS, n_keys, 4 * GRID_W), BF16),
                        pltpu.VMEM((PIPE_SLOTS, SUBLANES, 4 * GRID_W), F32),
                        pltpu.VMEM((N_HEAD_PAIRS, tile, LANES), F32)],
        compiler_params=_params(2),
        name="na_attention",
    )(jnp.asarray(ids), q, k, v, kc, vc, g, x, mods, bias, w_out)


def kernel(x, c, ctx, c_ctx, norm_g, ada_w, ada_b, pool_w_in, pool_w_grp, pool_scale, pool_w_out,
           na_w_in, na_rpb, na_w_out, conv_w_in, conv_dw, conv_db, conv_w_out, final_g):
    depth = norm_g.shape[0]
    batch, _, d = x.shape
    assert batch < COND_ROWS and WIN_ROWS // 2 <= HALO

    cond = jnp.zeros((COND_ROWS, d), F32).at[:batch].set(c).at[batch].set(c_ctx)
    mods = _modulation(cond, ada_w, ada_b).reshape(depth, COND_ROWS, 3, d)
    norm_g = norm_g.reshape(depth, 1, d)

    f32_weights = {1: (na_w_in, na_w_out), 2: (conv_w_in, conv_w_out)}
    bf16_weights = {0: _fold_pool_weights(pool_w_in, pool_w_grp, pool_scale, pool_w_out)}

    def weights_of(kind):
        if kind not in bf16_weights:
            bf16_weights[kind] = tuple(w.astype(BF16) for w in f32_weights[kind])
        return bf16_weights[kind]

    def casts_for_next(i):
        kind = (i + 1) % N_MIXERS
        return f32_weights[kind] if i + 1 < depth and kind not in bf16_weights else ()

    last_ctx_reader = max([i for i in range(depth) if i % N_MIXERS == 1], default=-1)
    for i in range(depth):
        kind, j = i % N_MIXERS, i // N_MIXERS
        update_ctx = i < last_ctx_reader
        fg = final_g if i == depth - 1 else None
        w_bf16 = weights_of(kind)
        casts = casts_for_next(i)
        if kind != 1:
            if kind == 0:
                kernel_fn, weights = _pool_kernel, w_bf16
            else:
                kernel_fn = _conv_kernel
                n_cols = w_bf16[0].shape[-1] // CONV_BLOCK
                weights = (*[(w_bf16[0], col, CONV_BLOCK) for col in range(n_cols)],
                           conv_dw, conv_db.reshape(-1, 1, WIDTH), w_bf16[1])
            run = functools.partial(_mixer_layer, kernel_fn, mods=mods, layer=i, norm_g=norm_g,
                                    weights=weights, j=j)
            if update_ctx:
                ctx = run(x=ctx, mod_row=batch, final_g=None, tile=CTX_TILE, name=f"ctx_layer{i}")
            x = run(x=x, mod_row=None, final_g=fg, tile=X_TILE, name=f"layer{i}", casts=casts)
            if casts:
                x, *cast_out = x
        else:
            if update_ctx:
                raise NotImplementedError("context output of a neighbourhood-attention layer")
            if fg is not None:
                raise NotImplementedError("final norm after a neighbourhood-attention layer")
            w_in, w_out = w_bf16
            q, k, v, g, *cast_out = _project(
                x, mods, i, None, norm_g, w_in, j, (0, 1, 2, 3),
                ("pairs_t", "pairs", "pairs_t", "flat"),
                (HEAD_DIM ** -0.5 * LOG2E, 1.0, 1.0, 1.0), X_TILE, f"na_project{i}", casts=casts)
            kc, vc = _project(ctx.reshape(1, -1, d), mods, i, batch, norm_g, w_in, j, (1, 2),
                              ("pairs", "pairs_t"), (1.0, 1.0), X_TILE, f"na_ctx_project{i}")
            x = _attention_layer(x, q, k, v, kc, vc, g, mods, i, na_rpb[j], w_out, j)
        if casts:
            bf16_weights[(i + 1) % N_MIXERS] = tuple(cast_out)
    return x
```

```python
import functools
import math

import numpy as np

import jax
import jax.numpy as jnp
from jax import lax
from jax.experimental import pallas as pl
from jax.experimental.pallas import tpu as pltpu

D_MODEL = 1024
WIDTH = D_MODEL
GRID_W = 64
N_MIXERS = 3
POOL_WINDOWS = (2, 4, 8, 16)
POOL_GROUP = WIDTH // len(POOL_WINDOWS)
HEAD_DIM = 64
N_HEADS = WIDTH // HEAD_DIM
WIN_ROWS = 8
WIN_COLS = 16
EPS = 1e-6

LANES = 128
SUBLANES = 8
BF16_SUBLANES = 16
MXU_TILE = 256
VMEM_LIMIT = 56 * 1024 * 1024

HALO = SUBLANES
N_HEAD_PAIRS = WIDTH // LANES
MASK_VALUE = -1e30
LOG2E = math.log2(math.e)

X_TILE = 1024
CTX_TILE = 256
ATT_ROWS = 8
SLAB_ROWS = WIN_ROWS + 2
PIPE_SLOTS = 2
BIAS_HEADS_PER_STEP = 8
CONV_BLOCK = MXU_TILE
COND_ROWS = BF16_SUBLANES

F32 = jnp.float32
BF16 = jnp.bfloat16


def _silu(x):
    return x / (1.0 + jnp.exp(-x))


def _mod_norm(x, norm_g, shift, scale):
    ms = jnp.mean(x * x, axis=-1, keepdims=True)
    return (x * lax.rsqrt(ms + EPS)) * (norm_g * (1.0 + scale)) + shift


def _params(n_axes):
    return pltpu.CompilerParams(
        dimension_semantics=("arbitrary",) * n_axes, vmem_limit_bytes=VMEM_LIMIT)


def _layer_block(arr, j, col=None):
    shape = (1,) + arr.shape[1:]
    index = (j,) + (0,) * (arr.ndim - 1)
    if col is not None:
        shape = shape[:-1] + (WIDTH,)
        index = index[:-1] + (col,)
    return pl.BlockSpec(shape, lambda *_: index, pipeline_mode=pl.Buffered(1))


def _mod_block(mods, layer, row):
    d = mods.shape[-1]
    if row is None:
        return pl.BlockSpec((1, 1, 3, d), lambda bi, i: (layer, bi, 0, 0))
    return pl.BlockSpec((1, 1, 3, d), lambda bi, i: (layer, row, 0, 0))


def _modulation_kernel(cond_ref, w_ref, b_ref, o_ref):
    s = _silu(cond_ref[...]).astype(BF16)
    o_ref[0] = jnp.dot(s, w_ref[0].astype(BF16), preferred_element_type=F32) + b_ref[0]


def _modulation(cond, ada_w, ada_b):
    depth, d, n = ada_w.shape
    rows = cond.shape[0]
    tn = n
    return pl.pallas_call(
        _modulation_kernel,
        grid=(depth, n // tn),
        in_specs=[
            pl.BlockSpec((rows, d), lambda i, j: (0, 0)),
            pl.BlockSpec((1, d, tn), lambda i, j: (i, 0, j)),
            pl.BlockSpec((1, 1, tn), lambda i, j: (i, 0, j)),
        ],
        out_specs=pl.BlockSpec((1, rows, tn), lambda i, j: (i, 0, j)),
        out_shape=jax.ShapeDtypeStruct((depth, rows, n), F32),
        compiler_params=_params(2),
        name="modulation",
    )(cond, ada_w, ada_b.reshape(depth, 1, n))


def _normed_rows(x_ref, xp_ref, xn_ref, mod_ref, ng_ref):
    shift, scale = mod_ref[0, 0, 0:1, :], mod_ref[0, 0, 1:2, :]
    xe = jnp.concatenate([xp_ref[0], x_ref[0], xn_ref[0]], axis=0)
    he = _mod_norm(xe, ng_ref[0], shift, scale)
    return he.astype(BF16), he[HALO:he.shape[0] - HALO].astype(BF16)


def _zero_outside_sequence(e, tile):
    i = pl.program_id(1)
    keep_prev = (i > 0).astype(F32)
    keep_next = (i < pl.num_programs(1) - 1).astype(F32)
    return jnp.concatenate(
        [e[:HALO] * keep_prev, e[HALO:HALO + tile], e[HALO + tile:] * keep_next], axis=0)


def _window_sum(e, width, tile):
    n = e.shape[0]
    half = width // 2
    f = e
    k = 1
    while k < half:
        f = f + pltpu.roll(f, n - k, axis=0)
        k *= 2
    if half == HALO:
        return f[:tile] + f[HALO:HALO + tile]
    return (pltpu.roll(f, half, axis=0) + f)[HALO:HALO + tile]


def _window_mean(ws, width, tile, seq_len):
    half = width // 2
    row = lax.broadcasted_iota(jnp.int32, (HALO, 1), 0)

    def inv_count(first_row):
        t = row + (pl.program_id(1) * tile + first_row)
        cnt = jnp.minimum(t + half, seq_len) - jnp.maximum(t - half, 0)
        return 1.0 / cnt.astype(F32)

    return jnp.concatenate([ws[:HALO] * inv_count(0),
                            ws[HALO:tile - HALO] * (1.0 / width),
                            ws[tile - HALO:] * inv_count(tile - HALO)], axis=0)


def _finish(x, y, gate, o_ref, fg_ref):
    out = x + gate * y
    if fg_ref is not None:
        ms = jnp.mean(out * out, axis=-1, keepdims=True)
        out = (out * lax.rsqrt(ms + EPS)) * fg_ref[...]
    o_ref[0] = out


def _cast_blocks(arr, n_inner, n_steps):
    layers, rows, cols = arr.shape
    if cols % (n_steps * LANES) == 0:
        shape = (layers, rows, cols // n_steps)
        index = lambda bi, i: (0, 0, bi * n_inner + i)
    else:
        assert rows % (n_steps * BF16_SUBLANES) == 0
        shape = (layers, rows // n_steps, cols)
        index = lambda bi, i: (0, bi * n_inner + i, 0)
    return (pl.BlockSpec(shape, index), pl.BlockSpec(shape, index),
            jax.ShapeDtypeStruct(arr.shape, BF16))


def _run_casts(cast_refs):
    n = len(cast_refs) // 2
    for src, dst in zip(cast_refs[:n], cast_refs[n:]):
        dst[...] = src[...].astype(BF16)


def _fold_pool_kernel(win_ref, wgrp_ref, ps_ref, wout_in_ref, wu_ref, wg_ref, wout_ref):
    for gi in range(len(POOL_WINDOWS)):
        cols = slice(gi * POOL_GROUP, (gi + 1) * POOL_GROUP)
        folded = jnp.dot(win_ref[0, :, cols], wgrp_ref[0, gi],
                         preferred_element_type=F32, precision=lax.Precision.HIGHEST)
        wu_ref[0, :, cols] = (folded * ps_ref[0, :, cols]).astype(BF16)
    wg_ref[0] = win_ref[0, :, WIDTH:].astype(BF16)
    wout_ref[0] = wout_in_ref[0].astype(BF16)


def _fold_pool_weights(w_in, w_grp, scale, w_out):
    n, d, _ = w_in.shape
    out = jax.ShapeDtypeStruct((n, d, WIDTH), BF16)
    return pl.pallas_call(
        _fold_pool_kernel,
        grid=(n,),
        in_specs=[pl.BlockSpec((1,) + w_in.shape[1:], lambda i: (i, 0, 0)),
                  pl.BlockSpec((1,) + w_grp.shape[1:], lambda i: (i, 0, 0, 0)),
                  pl.BlockSpec((1, 1, WIDTH), lambda i: (i, 0, 0)),
                  pl.BlockSpec((1,) + w_out.shape[1:], lambda i: (i, 0, 0))],
        out_specs=[pl.BlockSpec((1, d, WIDTH), lambda i: (i, 0, 0))] * 2
        + [pl.BlockSpec((1,) + w_out.shape[1:], lambda i: (i, 0, 0))],
        out_shape=[out, out, jax.ShapeDtypeStruct(w_out.shape, BF16)],
        compiler_params=_params(1),
        name="fold_pool_weights",
    )(w_in, w_grp, scale.reshape(n, 1, WIDTH), w_out)


def _split_rest(rest, n_casts):
    has_gain = len(rest) - 1 - 2 * n_casts
    fg_ref = rest[0] if has_gain else None
    cast_refs = rest[has_gain:has_gain + n_casts] + rest[has_gain + n_casts + 1:]
    return fg_ref, rest[has_gain + n_casts], cast_refs


def _pool_kernel(x_ref, xp_ref, xn_ref, mod_ref, ng_ref, wu_ref, wg_ref, wout_ref,
                 *rest, tile, seq_len, n_casts):
    fg_ref, o_ref, cast_refs = _split_rest(rest, n_casts)
    _run_casts(cast_refs)
    he, hm = _normed_rows(x_ref, xp_ref, xn_ref, mod_ref, ng_ref)
    t = jnp.dot(he, wu_ref[0], preferred_element_type=F32)
    t = _zero_outside_sequence(t, tile)
    g = jnp.dot(hm, wg_ref[0], preferred_element_type=F32)
    mixed = []
    for gi, width in enumerate(POOL_WINDOWS):
        te = t[:, gi * POOL_GROUP:(gi + 1) * POOL_GROUP]
        pooled = _window_mean(_window_sum(te, width, tile), width, tile, seq_len)
        mixed.append(pooled - te[HALO:HALO + tile])
    mixed = jnp.concatenate(mixed, axis=1)
    z = (mixed * _silu(g)).astype(BF16)
    y = jnp.dot(z, wout_ref[0], preferred_element_type=F32)
    _finish(x_ref[0], y, mod_ref[0, 0, 2:3, :], o_ref, fg_ref)


def _conv_kernel(x_ref, xp_ref, xn_ref, mod_ref, ng_ref, win_ref, dw_ref, db_ref, wout_ref,
                 *rest, tile, seq_len, n_casts):
    fg_ref, o_ref, cast_refs = _split_rest(rest, n_casts)
    _run_casts(cast_refs)
    he, hm = _normed_rows(x_ref, xp_ref, xn_ref, mod_ref, ng_ref)
    n = tile + 2 * HALO
    rows = slice(HALO, HALO + tile)
    y = None
    for ci in range(WIDTH // CONV_BLOCK):
        cols = slice(ci * CONV_BLOCK, (ci + 1) * CONV_BLOCK)

        def proj(h, part):
            w = win_ref[0, :, part * WIDTH + ci * CONV_BLOCK:part * WIDTH + (ci + 1) * CONV_BLOCK]
            return jnp.dot(h, w, preferred_element_type=F32)

        z = _zero_outside_sequence(proj(he, 1) * proj(he, 2), tile)
        conv = (dw_ref[0, 0:1, cols] * pltpu.roll(z, 1, axis=0)[rows]
                + dw_ref[0, 1:2, cols] * z[rows]
                + dw_ref[0, 2:3, cols] * pltpu.roll(z, n - 1, axis=0)[rows]
                + db_ref[0, :, cols])
        yc = (proj(hm, 0) * conv * _silu(proj(hm, 3))).astype(BF16)
        part = jnp.dot(yc, wout_ref[0, cols, :], preferred_element_type=F32)
        y = part if y is None else y + part
    _finish(x_ref[0], y, mod_ref[0, 0, 2:3, :], o_ref, fg_ref)


def _mixer_layer(kernel_fn, x, mods, layer, mod_row, norm_g, weights, j, final_g, tile, name,
                 casts=()):
    b, seq_len, d = x.shape
    tile = min(tile, seq_len)
    halo_blocks = seq_len // HALO
    per_tile = tile // HALO
    n_inner = seq_len // tile
    cast_blocks = [_cast_blocks(a, n_inner, b * n_inner) for a in casts]
    in_specs = [
        pl.BlockSpec((1, tile, d), lambda bi, i: (bi, i, 0)),
        pl.BlockSpec((1, HALO, d), lambda bi, i: (bi, jnp.maximum(i * per_tile - 1, 0), 0)),
        pl.BlockSpec((1, HALO, d),
                     lambda bi, i: (bi, jnp.minimum((i + 1) * per_tile, halo_blocks - 1), 0)),
        _mod_block(mods, layer, mod_row),
        _layer_block(norm_g, layer),
    ] + [_layer_block(w, j) for w in weights]
    args = [x, x, x, mods, norm_g] + list(weights)
    if final_g is not None:
        in_specs.append(pl.BlockSpec((1, d), lambda bi, i: (0, 0)))
        args.append(final_g.reshape(1, d))
    outs = pl.pallas_call(
        functools.partial(kernel_fn, tile=tile, seq_len=seq_len, n_casts=len(casts)),
        grid=(b, n_inner),
        in_specs=in_specs + [blk[0] for blk in cast_blocks],
        out_specs=[pl.BlockSpec((1, tile, d), lambda bi, i: (bi, i, 0))]
        + [blk[1] for blk in cast_blocks],
        out_shape=[jax.ShapeDtypeStruct(x.shape, F32)] + [blk[2] for blk in cast_blocks],
        compiler_params=_params(2),
        name=name,
    )(*args, *casts)
    return outs if casts else outs[0]


def _project_kernel(x_ref, mod_ref, ng_ref, *refs, scales, layouts):
    n = len(scales)
    n_casts = (len(refs) - 2 * n) // 2
    w_refs, o_refs = refs[:n], refs[n + n_casts:2 * n + n_casts]
    _run_casts(refs[n:n + n_casts] + refs[2 * n + n_casts:])
    shift, scale = mod_ref[0, 0, 0:1, :], mod_ref[0, 0, 1:2, :]
    h = _mod_norm(x_ref[0], ng_ref[0], shift, scale).astype(BF16)
    for w_ref, o_ref, s, layout in zip(w_refs, o_refs, scales, layouts):
        p = jnp.dot(h, w_ref[0], preferred_element_type=F32)
        if s != 1.0:
            p = p * s
        if layout == "flat":
            o_ref[0] = p.astype(o_ref.dtype)
            continue
        for hp in range(N_HEAD_PAIRS):
            block = p[:, hp * LANES:(hp + 1) * LANES]
            if layout == "pairs":
                o_ref[0, hp] = block.astype(o_ref.dtype)
                continue
            for tb in range(block.shape[0] // LANES):
                o_ref[0, hp, tb] = block[tb * LANES:(tb + 1) * LANES].T.astype(o_ref.dtype)


def _project(x, mods, layer, mod_row, norm_g, w, j, cols, layouts, scales, tile, name,
             casts=()):
    b, seq_len, d = x.shape
    tile = min(tile, seq_len)
    assert tile % LANES == 0
    specs = {
        "flat": (pl.BlockSpec((1, tile, WIDTH), lambda bi, i: (bi, i, 0)),
                 jax.ShapeDtypeStruct((b, seq_len, WIDTH), F32)),
        "pairs": (pl.BlockSpec((1, N_HEAD_PAIRS, tile, LANES), lambda bi, i: (bi, 0, i, 0)),
                  jax.ShapeDtypeStruct((b, N_HEAD_PAIRS, seq_len, LANES), BF16)),
        "pairs_t": (pl.BlockSpec((1, N_HEAD_PAIRS, tile // LANES, LANES, LANES),
                                 lambda bi, i: (bi, 0, i, 0, 0)),
                    jax.ShapeDtypeStruct((b, N_HEAD_PAIRS, seq_len // LANES, LANES, LANES), BF16)),
    }
    n_inner = seq_len // tile
    cast_blocks = [_cast_blocks(a, n_inner, b * n_inner) for a in casts]
    out_specs = [specs[name_][0] for name_ in layouts] + [blk[1] for blk in cast_blocks]
    out_shape = [specs[name_][1] for name_ in layouts] + [blk[2] for blk in cast_blocks]
    return pl.pallas_call(
        functools.partial(_project_kernel, scales=scales, layouts=layouts),
        grid=(b, n_inner),
        in_specs=[
            pl.BlockSpec((1, tile, d), lambda bi, i: (bi, i, 0)),
            _mod_block(mods, layer, mod_row),
            _layer_block(norm_g, layer),
        ] + [_layer_block(w, j, col) for col in cols] + [blk[0] for blk in cast_blocks],
        out_specs=out_specs,
        out_shape=out_shape,
        compiler_params=_params(2),
        name=name,
    )(x, mods, norm_g, *([w] * len(cols)), *casts)


SLAB_TILES = SLAB_ROWS // 2


def _slab_base(pair_row0, n_rows):
    return jnp.clip(pair_row0 - WIN_ROWS // 2, 0, n_rows - SLAB_ROWS)


def _bias_tile_plan(n_rows):
    tiles, ids = [], []
    for r0 in range(0, n_rows, 2):
        base = int(np.clip(r0 - WIN_ROWS // 2, 0, n_rows - SLAB_ROWS))
        for j in range(SLAB_TILES):
            quad = []
            for key_row in (base + 2 * j, base + 2 * j + 1):
                for r in (r0, r0 + 1):
                    start = int(np.clip(r - WIN_ROWS // 2, 0, n_rows - WIN_ROWS))
                    assert base <= start and start + WIN_ROWS <= base + SLAB_ROWS
                    inside = start <= key_row < start + WIN_ROWS
                    quad.append(key_row - r + WIN_ROWS - 1 if inside else None)
            quad = tuple(quad)
            if quad not in tiles:
                tiles.append(quad)
            ids.append(tiles.index(quad))
    return tiles, np.asarray(ids, np.int32)


def _bias_tiles_kernel(w_ref, o_ref, *, tiles):
    k_col = lax.broadcasted_iota(jnp.int32, (GRID_W, LANES), 0)
    lane = lax.broadcasted_iota(jnp.int32, (GRID_W, LANES), 1)
    q_col = lane % GRID_W
    c_start = jnp.clip(q_col - WIN_COLS // 2, 0, GRID_W - WIN_COLS)
    inside = (k_col >= c_start) & (k_col < c_start + WIN_COLS)
    first_query_row = lane < GRID_W
    masked = jnp.full((GRID_W, LANES), MASK_VALUE, F32)

    def toeplitz(head, d, shift):
        if d is None:
            return masked
        row = jnp.broadcast_to(w_ref[head, d:d + 1, :], (GRID_W, LANES))
        return pltpu.roll(row, shift, axis=1, stride=1, stride_axis=0) * LOG2E

    for head in range(w_ref.shape[0]):
        for tile_id, quad in enumerate(tiles):
            halves = [jnp.where(inside,
                                jnp.where(first_query_row, toeplitz(head, quad[2 * kr], 0),
                                          toeplitz(head, quad[2 * kr + 1], GRID_W)),
                                masked) for kr in range(2)]
            o_ref[head, tile_id] = jnp.concatenate(halves, axis=0)


def _attention_bias(rpb, tiles):
    h, n_dr, n_dc = rpb.shape
    assert n_dr == 2 * WIN_ROWS - 1 and n_dc == 2 * WIN_COLS - 1
    rev = rpb[..., ::-1]
    w = jnp.concatenate([rev[..., WIN_COLS - 1:], jnp.zeros((h, n_dr, LANES - n_dc), F32),
                         rev[..., :WIN_COLS - 1]], axis=-1)
    w = jnp.pad(w, ((0, 0), (0, 2 * WIN_ROWS - n_dr), (0, 0)))
    return pl.pallas_call(
        functools.partial(_bias_tiles_kernel, tiles=tiles),
        grid=(h // BIAS_HEADS_PER_STEP,),
        in_specs=[pl.BlockSpec((BIAS_HEADS_PER_STEP, 2 * WIN_ROWS, LANES), lambda i: (i, 0, 0))],
        out_specs=pl.BlockSpec((BIAS_HEADS_PER_STEP, len(tiles), LANES, LANES),
                               lambda i: (i, 0, 0, 0)),
        out_shape=jax.ShapeDtypeStruct((h, len(tiles), LANES, LANES), F32),
        compiler_params=_params(1),
        name="bias_tiles",
    )(w)


def _attention_kernel(ids_ref, q_ref, k_ref, v_ref, kc_ref, vc_ref, g_ref, x_ref, mod_ref,
                      bias_ref, wout_ref, o_ref, s_scr, p_scr, r_scr, o_scr, *, n_rows):
    pair_tokens = 2 * GRID_W
    n_ctx = kc_ref.shape[2]
    slab_keys = SLAB_ROWS * GRID_W
    n_items = (ATT_ROWS // 2) * N_HEAD_PAIRS
    first_head = lax.broadcasted_iota(jnp.int32, (LANES, pair_tokens), 0) < HEAD_DIM

    def locate(item):
        hp, pb = item // (ATT_ROWS // 2), item % (ATT_ROWS // 2)
        pair = pl.program_id(1) * (ATT_ROWS // 2) + pb
        k_off = pl.multiple_of(_slab_base(2 * pair, n_rows) * GRID_W, pair_tokens)
        return hp, pair, pb * pair_tokens, k_off

    def scores(item, slot):
        hp, _, q_off, k_off = locate(item)
        qt = q_ref[0, hp, q_off // pair_tokens]
        zero = jnp.zeros_like(qt)
        q2 = jnp.concatenate([jnp.where(first_head, qt, zero),
                              jnp.where(first_head, zero, qt)], axis=1)
        s_scr[slot, :n_ctx, :] = jnp.dot(kc_ref[0, hp], q2, preferred_element_type=F32)
        s_scr[slot, n_ctx:, :] = jnp.dot(k_ref[0, hp, pl.ds(k_off, slab_keys), :], q2,
                                         preferred_element_type=F32)

    def softmax(item, slot):
        hp, pair, _, _ = locate(item)
        bias = jnp.concatenate(
            [jnp.concatenate([bias_ref[2 * hp + a, ids_ref[pair * SLAB_TILES + j]]
                              for a in range(2)], axis=1)
             for j in range(SLAB_TILES)], axis=0)
        s_ctx = s_scr[slot, :n_ctx, :]
        s_loc = s_scr[slot, n_ctx:, :] + bias
        m = jnp.maximum(jnp.max(s_loc, axis=0, keepdims=True),
                        jnp.max(s_ctx, axis=0, keepdims=True))
        p_ctx = jnp.exp2(s_ctx - m)
        p_loc = jnp.exp2(s_loc - m)
        denom = jnp.sum(p_loc, axis=0, keepdims=True) + jnp.sum(p_ctx, axis=0, keepdims=True)
        p_scr[slot, :n_ctx, :] = p_ctx.astype(BF16)
        p_scr[slot, n_ctx:, :] = p_loc.astype(BF16)
        r_scr[slot] = jnp.broadcast_to(1.0 / denom, r_scr.shape[1:])

    def values(item, slot):
        hp, _, q_off, k_off = locate(item)
        vc = jnp.concatenate([vc_ref[0, hp, j] for j in range(n_ctx // LANES)], axis=1)
        v_block = lax.shift_right_logical(k_off, LANES.bit_length() - 1)
        vs = jnp.concatenate([v_ref[0, hp, v_block + j] for j in range(slab_keys // LANES)],
                             axis=1)
        pv = (jnp.dot(vc, p_scr[slot, :n_ctx, :], preferred_element_type=F32)
              + jnp.dot(vs, p_scr[slot, n_ctx:, :], preferred_element_type=F32))
        pv = pv * r_scr[slot, 0:1, :]
        out = jnp.where(first_head, pv[:, :pair_tokens], pv[:, pair_tokens:])
        o_scr[hp, pl.ds(q_off, pair_tokens), :] = out.T

    for t in range(-2, n_items):
        if t + 2 < n_items:
            scores(t + 2, (t + 2) % PIPE_SLOTS)
        if 0 <= t + 1 < n_items:
            softmax(t + 1, (t + 1) % PIPE_SLOTS)
        if t >= 0:
            values(t, t % PIPE_SLOTS)

    gate = mod_ref[0, 0, 2:3, :]
    o = jnp.concatenate([o_scr[hp] for hp in range(N_HEAD_PAIRS)], axis=1)
    y = (o * _silu(g_ref[0])).astype(BF16)
    y = jnp.dot(y, wout_ref[0], preferred_element_type=F32)
    o_ref[0] = x_ref[0] + gate * y


def _attention_layer(x, q, k, v, kc, vc, g, mods, layer, rpb, w_out, j):
    b, seq_len, d = x.shape
    n_ctx = kc.shape[2] // b
    n_rows = seq_len // GRID_W
    assert n_rows % ATT_ROWS == 0 and n_rows >= SLAB_ROWS and ATT_ROWS % 2 == 0
    tile = ATT_ROWS * GRID_W
    n_keys = n_ctx + SLAB_ROWS * GRID_W
    tiles, ids = _bias_tile_plan(n_rows)
    bias = _attention_bias(rpb, tiles)
    tile_spec = pl.BlockSpec((1, tile, d), lambda bi, i: (bi, i, 0))
    q_spec = pl.BlockSpec((1, N_HEAD_PAIRS, tile // LANES, LANES, LANES),
                          lambda bi, i: (bi, 0, i, 0, 0))
    seq_spec = pl.BlockSpec((1, N_HEAD_PAIRS, seq_len, LANES), lambda bi, i: (bi, 0, 0, 0))
    ctx_spec = pl.BlockSpec((1, N_HEAD_PAIRS, n_ctx, LANES), lambda bi, i: (0, 0, bi, 0))
    seq_t_spec = pl.BlockSpec((1, N_HEAD_PAIRS, seq_len // LANES, LANES, LANES),
                              lambda bi, i: (bi, 0, 0, 0, 0))
    ctx_t_spec = pl.BlockSpec((1, N_HEAD_PAIRS, n_ctx // LANES, LANES, LANES),
                              lambda bi, i: (0, 0, bi, 0, 0))
    return pl.pallas_call(
        functools.partial(_attention_kernel, n_rows=n_rows),
        grid=(b, n_rows // ATT_ROWS),
        in_specs=[pl.BlockSpec(memory_space=pltpu.SMEM),
                  q_spec, seq_spec, seq_t_spec, ctx_spec, ctx_t_spec, tile_spec, tile_spec,
                  _mod_block(mods, layer, None),
                  pl.BlockSpec(bias.shape, lambda bi, i: (0, 0, 0, 0),
                               pipeline_mode=pl.Buffered(1)),
                  _layer_block(w_out, j)],
        out_specs=tile_spec,
        out_shape=jax.ShapeDtypeStruct(x.shape, F32),
        scratch_shapes=[pltpu.VMEM((PIPE_SLOTS, n_keys, 4 * GRID_W), F32),
                        pltpu.VMEM((PIPE_SLOTS, n_keys, 4 * GRID_W), BF16),
                        pltpu.VMEM((PIPE_SLOTS, SUBLANES, 4 * GRID_W), F32),
                        pltpu.VMEM((N_HEAD_PAIRS, tile, LANES), F32)],
        compiler_params=_params(2),
        name="na_attention",
    )(jnp.asarray(ids), q, k, v, kc, vc, g, x, mods, bias, w_out)


def kernel(x, c, ctx, c_ctx, norm_g, ada_w, ada_b, pool_w_in, pool_w_grp, pool_scale, pool_w_out,
           na_w_in, na_rpb, na_w_out, conv_w_in, conv_dw, conv_db, conv_w_out, final_g):
    depth = norm_g.shape[0]
    batch, _, d = x.shape
    assert batch < COND_ROWS and WIN_ROWS // 2 <= HALO

    cond = jnp.zeros((COND_ROWS, d), F32).at[:batch].set(c).at[batch].set(c_ctx)
    mods = _modulation(cond, ada_w, ada_b).reshape(depth, COND_ROWS, 3, d)
    norm_g = norm_g.reshape(depth, 1, d)

    f32_weights = {1: (na_w_in, na_w_out), 2: (conv_w_in, conv_w_out)}
    bf16_weights = {0: _fold_pool_weights(pool_w_in, pool_w_grp, pool_scale, pool_w_out)}

    def weights_of(kind):
        if kind not in bf16_weights:
            bf16_weights[kind] = tuple(w.astype(BF16) for w in f32_weights[kind])
        return bf16_weights[kind]

    def casts_for_next(i):
        kind = (i + 1) % N_MIXERS
        return f32_weights[kind] if i + 1 < depth and kind not in bf16_weights else ()

    last_ctx_reader = max([i for i in range(depth) if i % N_MIXERS == 1], default=-1)
    for i in range(depth):
        kind, j = i % N_MIXERS, i // N_MIXERS
        update_ctx = i < last_ctx_reader
        fg = final_g if i == depth - 1 else None
        w_bf16 = weights_of(kind)
        casts = casts_for_next(i)
        if kind != 1:
            if kind == 0:
                kernel_fn, weights = _pool_kernel, w_bf16
            else:
                kernel_fn = _conv_kernel
                weights = (w_bf16[0], conv_dw, conv_db.reshape(-1, 1, WIDTH), w_bf16[1])
            run = functools.partial(_mixer_layer, kernel_fn, mods=mods, layer=i, norm_g=norm_g,
                                    weights=weights, j=j)
            if update_ctx:
                ctx = run(x=ctx, mod_row=batch, final_g=None, tile=CTX_TILE, name=f"ctx_layer{i}")
            x = run(x=x, mod_row=None, final_g=fg, tile=X_TILE, name=f"layer{i}", casts=casts)
            if casts:
                x, *cast_out = x
        else:
            if update_ctx:
                raise NotImplementedError("context output of a neighbourhood-attention layer")
            if fg is not None:
                raise NotImplementedError("final norm after a neighbourhood-attention layer")
            w_in, w_out = w_bf16
            q, k, v, g, *cast_out = _project(
                x, mods, i, None, norm_g, w_in, j, (0, 1, 2, 3),
                ("pairs_t", "pairs", "pairs_t", "flat"),
                (HEAD_DIM ** -0.5 * LOG2E, 1.0, 1.0, 1.0), X_TILE, f"na_project{i}", casts=casts)
            kc, vc = _project(ctx.reshape(1, -1, d), mods, i, batch, norm_g, w_in, j, (1, 2),
                              ("pairs", "pairs_t"), (1.0, 1.0), X_TILE, f"na_ctx_project{i}")
            x = _attention_layer(x, q, k, v, kc, vc, g, mods, i, na_rpb[j], w_out, j)
        if casts:
            bf16_weights[(i + 1) % N_MIXERS] = tuple(cast_out)
    return x
```

```python
import functools
import math

import numpy as np

import jax
import jax.numpy as jnp
from jax import lax
from jax.experimental import pallas as pl
from jax.experimental.pallas import tpu as pltpu

D_MODEL = 1024
WIDTH = D_MODEL
GRID_W = 64
N_MIXERS = 3
POOL_WINDOWS = (2, 4, 8, 16)
POOL_GROUP = WIDTH // len(POOL_WINDOWS)
HEAD_DIM = 64
N_HEADS = WIDTH // HEAD_DIM
WIN_ROWS = 8
WIN_COLS = 16
EPS = 1e-6

LANES = 128
SUBLANES = 8
BF16_SUBLANES = 16
MXU_TILE = 256
VMEM_LIMIT = 56 * 1024 * 1024

HALO = SUBLANES
N_HEAD_PAIRS = WIDTH // LANES
MASK_VALUE = -1e30
LOG2E = math.log2(math.e)

X_TILE = 1024
CTX_TILE = 256
ATT_ROWS = 8
SLAB_ROWS = WIN_ROWS + 2
PIPE_SLOTS = 2
BIAS_HEADS_PER_STEP = 8
CONV_BLOCK = MXU_TILE
COND_ROWS = BF16_SUBLANES

F32 = jnp.float32
BF16 = jnp.bfloat16


def _silu(x):
    return x / (1.0 + jnp.exp(-x))


def _mod_norm(x, norm_g, shift, scale):
    ms = jnp.mean(x * x, axis=-1, keepdims=True)
    return (x * lax.rsqrt(ms + EPS)) * (norm_g * (1.0 + scale)) + shift


def _params(n_axes):
    return pltpu.CompilerParams(
        dimension_semantics=("arbitrary",) * n_axes, vmem_limit_bytes=VMEM_LIMIT)


def _layer_block(arr, j, col=None):
    shape = (1,) + arr.shape[1:]
    index = (j,) + (0,) * (arr.ndim - 1)
    if col is not None:
        shape = shape[:-1] + (WIDTH,)
        index = index[:-1] + (col,)
    return pl.BlockSpec(shape, lambda *_: index, pipeline_mode=pl.Buffered(1))


def _mod_block(mods, layer, row):
    d = mods.shape[-1]
    if row is None:
        return pl.BlockSpec((1, 1, 3, d), lambda bi, i: (layer, bi, 0, 0))
    return pl.BlockSpec((1, 1, 3, d), lambda bi, i: (layer, row, 0, 0))


def _modulation_kernel(cond_ref, w_ref, b_ref, o_ref):
    s = _silu(cond_ref[...]).astype(BF16)
    o_ref[0] = jnp.dot(s, w_ref[0].astype(BF16), preferred_element_type=F32) + b_ref[0]


def _modulation(cond, ada_w, ada_b):
    depth, d, n = ada_w.shape
    rows = cond.shape[0]
    tn = n
    return pl.pallas_call(
        _modulation_kernel,
        grid=(depth, n // tn),
        in_specs=[
            pl.BlockSpec((rows, d), lambda i, j: (0, 0)),
            pl.BlockSpec((1, d, tn), lambda i, j: (i, 0, j)),
            pl.BlockSpec((1, 1, tn), lambda i, j: (i, 0, j)),
        ],
        out_specs=pl.BlockSpec((1, rows, tn), lambda i, j: (i, 0, j)),
        out_shape=jax.ShapeDtypeStruct((depth, rows, n), F32),
        compiler_params=_params(2),
        name="modulation",
    )(cond, ada_w, ada_b.reshape(depth, 1, n))


def _normed_rows(x_ref, xp_ref, xn_ref, mod_ref, ng_ref):
    shift, scale = mod_ref[0, 0, 0:1, :], mod_ref[0, 0, 1:2, :]
    xe = jnp.concatenate([xp_ref[0], x_ref[0], xn_ref[0]], axis=0)
    he = _mod_norm(xe, ng_ref[0], shift, scale)
    return he.astype(BF16), he[HALO:he.shape[0] - HALO].astype(BF16)


def _zero_outside_sequence(e, tile):
    i = pl.program_id(1)
    keep_prev = (i > 0).astype(F32)
    keep_next = (i < pl.num_programs(1) - 1).astype(F32)
    return jnp.concatenate(
        [e[:HALO] * keep_prev, e[HALO:HALO + tile], e[HALO + tile:] * keep_next], axis=0)


def _window_sum(e, width, tile):
    n = e.shape[0]
    half = width // 2
    f = e
    k = 1
    while k < half:
        f = f + pltpu.roll(f, n - k, axis=0)
        k *= 2
    if half == HALO:
        return f[:tile] + f[HALO:HALO + tile]
    return (pltpu.roll(f, half, axis=0) + f)[HALO:HALO + tile]


def _window_mean(ws, width, tile, seq_len):
    half = width // 2
    row = lax.broadcasted_iota(jnp.int32, (HALO, 1), 0)

    def inv_count(first_row):
        t = row + (pl.program_id(1) * tile + first_row)
        cnt = jnp.minimum(t + half, seq_len) - jnp.maximum(t - half, 0)
        return 1.0 / cnt.astype(F32)

    return jnp.concatenate([ws[:HALO] * inv_count(0),
                            ws[HALO:tile - HALO] * (1.0 / width),
                            ws[tile - HALO:] * inv_count(tile - HALO)], axis=0)


def _finish(x, y, gate, o_ref, fg_ref):
    out = x + gate * y
    if fg_ref is not None:
        ms = jnp.mean(out * out, axis=-1, keepdims=True)
        out = (out * lax.rsqrt(ms + EPS)) * fg_ref[...]
    o_ref[0] = out


def _cast_blocks(arr, n_inner, n_steps):
    layers, rows, cols = arr.shape
    if cols % (n_steps * LANES) == 0:
        shape = (layers, rows, cols // n_steps)
        index = lambda bi, i: (0, 0, bi * n_inner + i)
    else:
        assert rows % (n_steps * BF16_SUBLANES) == 0
        shape = (layers, rows // n_steps, cols)
        index = lambda bi, i: (0, bi * n_inner + i, 0)
    return (pl.BlockSpec(shape, index), pl.BlockSpec(shape, index),
            jax.ShapeDtypeStruct(arr.shape, BF16))


def _run_casts(cast_refs):
    n = len(cast_refs) // 2
    for src, dst in zip(cast_refs[:n], cast_refs[n:]):
        dst[...] = src[...].astype(BF16)


def _fold_pool_kernel(win_ref, wgrp_ref, ps_ref, wout_in_ref, wu_ref, wg_ref, wout_ref):
    for gi in range(len(POOL_WINDOWS)):
        cols = slice(gi * POOL_GROUP, (gi + 1) * POOL_GROUP)
        folded = jnp.dot(win_ref[0, :, cols], wgrp_ref[0, gi],
                         preferred_element_type=F32, precision=lax.Precision.HIGHEST)
        wu_ref[0, :, cols] = (folded * ps_ref[0, :, cols]).astype(BF16)
    wg_ref[0] = win_ref[0, :, WIDTH:].astype(BF16)
    wout_ref[0] = wout_in_ref[0].astype(BF16)


def _fold_pool_weights(w_in, w_grp, scale, w_out):
    n, d, _ = w_in.shape
    out = jax.ShapeDtypeStruct((n, d, WIDTH), BF16)
    return pl.pallas_call(
        _fold_pool_kernel,
        grid=(n,),
        in_specs=[pl.BlockSpec((1,) + w_in.shape[1:], lambda i: (i, 0, 0)),
                  pl.BlockSpec((1,) + w_grp.shape[1:], lambda i: (i, 0, 0, 0)),
                  pl.BlockSpec((1, 1, WIDTH), lambda i: (i, 0, 0)),
                  pl.BlockSpec((1,) + w_out.shape[1:], lambda i: (i, 0, 0))],
        out_specs=[pl.BlockSpec((1, d, WIDTH), lambda i: (i, 0, 0))] * 2
        + [pl.BlockSpec((1,) + w_out.shape[1:], lambda i: (i, 0, 0))],
        out_shape=[out, out, jax.ShapeDtypeStruct(w_out.shape, BF16)],
        compiler_params=_params(1),
        name="fold_pool_weights",
    )(w_in, w_grp, scale.reshape(n, 1, WIDTH), w_out)


def _split_rest(rest, n_casts):
    has_gain = len(rest) - 1 - 2 * n_casts
    fg_ref = rest[0] if has_gain else None
    cast_refs = rest[has_gain:has_gain + n_casts] + rest[has_gain + n_casts + 1:]
    return fg_ref, rest[has_gain + n_casts], cast_refs


def _pool_kernel(x_ref, xp_ref, xn_ref, mod_ref, ng_ref, wu_ref, wg_ref, wout_ref,
                 *rest, tile, seq_len, n_casts):
    fg_ref, o_ref, cast_refs = _split_rest(rest, n_casts)
    _run_casts(cast_refs)
    he, hm = _normed_rows(x_ref, xp_ref, xn_ref, mod_ref, ng_ref)
    t = jnp.dot(he, wu_ref[0], preferred_element_type=F32)
    t = _zero_outside_sequence(t, tile)
    g = jnp.dot(hm, wg_ref[0], preferred_element_type=F32)
    mixed = []
    for gi, width in enumerate(POOL_WINDOWS):
        te = t[:, gi * POOL_GROUP:(gi + 1) * POOL_GROUP]
        pooled = _window_mean(_window_sum(te, width, tile), width, tile, seq_len)
        mixed.append(pooled - te[HALO:HALO + tile])
    mixed = jnp.concatenate(mixed, axis=1)
    z = (mixed * _silu(g)).astype(BF16)
    y = jnp.dot(z, wout_ref[0], preferred_element_type=F32)
    _finish(x_ref[0], y, mod_ref[0, 0, 2:3, :], o_ref, fg_ref)


def _conv_kernel(x_ref, xp_ref, xn_ref, mod_ref, ng_ref, win_ref, dw_ref, db_ref, wout_ref,
                 *rest, tile, seq_len, n_casts):
    fg_ref, o_ref, cast_refs = _split_rest(rest, n_casts)
    _run_casts(cast_refs)
    he, hm = _normed_rows(x_ref, xp_ref, xn_ref, mod_ref, ng_ref)
    n = tile + 2 * HALO
    rows = slice(HALO, HALO + tile)
    y = None
    for ci in range(WIDTH // CONV_BLOCK):
        cols = slice(ci * CONV_BLOCK, (ci + 1) * CONV_BLOCK)

        def proj(h, part):
            w = win_ref[0, :, part * WIDTH + ci * CONV_BLOCK:part * WIDTH + (ci + 1) * CONV_BLOCK]
            return jnp.dot(h, w, preferred_element_type=F32)

        z = _zero_outside_sequence(proj(he, 1) * proj(he, 2), tile)
        conv = (dw_ref[0, 0:1, cols] * pltpu.roll(z, 1, axis=0)[rows]
                + dw_ref[0, 1:2, cols] * z[rows]
                + dw_ref[0, 2:3, cols] * pltpu.roll(z, n - 1, axis=0)[rows]
                + db_ref[0, :, cols])
        yc = (proj(hm, 0) * conv * _silu(proj(hm, 3))).astype(BF16)
        part = jnp.dot(yc, wout_ref[0, cols, :], preferred_element_type=F32)
        y = part if y is None else y + part
    _finish(x_ref[0], y, mod_ref[0, 0, 2:3, :], o_ref, fg_ref)


def _mixer_layer(kernel_fn, x, mods, layer, mod_row, norm_g, weights, j, final_g, tile, name,
                 casts=()):
    b, seq_len, d = x.shape
    tile = min(tile, seq_len)
    halo_blocks = seq_len // HALO
    per_tile = tile // HALO
    n_inner = seq_len // tile
    cast_blocks = [_cast_blocks(a, n_inner, b * n_inner) for a in casts]
    in_specs = [
        pl.BlockSpec((1, tile, d), lambda bi, i: (bi, i, 0)),
        pl.BlockSpec((1, HALO, d), lambda bi, i: (bi, jnp.maximum(i * per_tile - 1, 0), 0)),
        pl.BlockSpec((1, HALO, d),
                     lambda bi, i: (bi, jnp.minimum((i + 1) * per_tile, halo_blocks - 1), 0)),
        _mod_block(mods, layer, mod_row),
        _layer_block(norm_g, layer),
    ] + [_layer_block(w, j) for w in weights]
    args = [x, x, x, mods, norm_g] + list(weights)
    if final_g is not None:
        in_specs.append(pl.BlockSpec((1, d), lambda bi, i: (0, 0)))
        args.append(final_g.reshape(1, d))
    outs = pl.pallas_call(
        functools.partial(kernel_fn, tile=tile, seq_len=seq_len, n_casts=len(casts)),
        grid=(b, n_inner),
        in_specs=in_specs + [blk[0] for blk in cast_blocks],
        out_specs=[pl.BlockSpec((1, tile, d), lambda bi, i: (bi, i, 0))]
        + [blk[1] for blk in cast_blocks],
        out_shape=[jax.ShapeDtypeStruct(x.shape, F32)] + [blk[2] for blk in cast_blocks],
        compiler_params=_params(2),
        name=name,
    )(*args, *casts)
    return outs if casts else outs[0]


def _project_kernel(x_ref, mod_ref, ng_ref, *refs, scales, layouts):
    n = len(scales)
    n_casts = (len(refs) - 2 * n) // 2
    w_refs, o_refs = refs[:n], refs[n + n_casts:2 * n + n_casts]
    _run_casts(refs[n:n + n_casts] + refs[2 * n + n_casts:])
    shift, scale = mod_ref[0, 0, 0:1, :], mod_ref[0, 0, 1:2, :]
    h = _mod_norm(x_ref[0], ng_ref[0], shift, scale).astype(BF16)
    for w_ref, o_ref, s, layout in zip(w_refs, o_refs, scales, layouts):
        p = jnp.dot(h, w_ref[0], preferred_element_type=F32)
        if s != 1.0:
            p = p * s
        if layout == "flat":
            o_ref[0] = p.astype(o_ref.dtype)
            continue
        for hp in range(N_HEAD_PAIRS):
            block = p[:, hp * LANES:(hp + 1) * LANES]
            if layout == "pairs":
                o_ref[0, hp] = block.astype(o_ref.dtype)
                continue
            for tb in range(block.shape[0] // LANES):
                o_ref[0, hp, tb] = block[tb * LANES:(tb + 1) * LANES].T.astype(o_ref.dtype)


def _project(x, mods, layer, mod_row, norm_g, w, j, cols, layouts, scales, tile, name,
             casts=()):
    b, seq_len, d = x.shape
    tile = min(tile, seq_len)
    assert tile % LANES == 0
    specs = {
        "flat": (pl.BlockSpec((1, tile, WIDTH), lambda bi, i: (bi, i, 0)),
                 jax.ShapeDtypeStruct((b, seq_len, WIDTH), F32)),
        "pairs": (pl.BlockSpec((1, N_HEAD_PAIRS, tile, LANES), lambda bi, i: (bi, 0, i, 0)),
                  jax.ShapeDtypeStruct((b, N_HEAD_PAIRS, seq_len, LANES), BF16)),
        "pairs_t": (pl.BlockSpec((1, N_HEAD_PAIRS, tile // LANES, LANES, LANES),
                                 lambda bi, i: (bi, 0, i, 0, 0)),
                    jax.ShapeDtypeStruct((b, N_HEAD_PAIRS, seq_len // LANES, LANES, LANES), BF16)),
    }
    n_inner = seq_len // tile
    cast_blocks = [_cast_blocks(a, n_inner, b * n_inner) for a in casts]
    out_specs = [specs[name_][0] for name_ in layouts] + [blk[1] for blk in cast_blocks]
    out_shape = [specs[name_][1] for name_ in layouts] + [blk[2] for blk in cast_blocks]
    return pl.pallas_call(
        functools.partial(_project_kernel, scales=scales, layouts=layouts),
        grid=(b, n_inner),
        in_specs=[
            pl.BlockSpec((1, tile, d), lambda bi, i: (bi, i, 0)),
            _mod_block(mods, layer, mod_row),
            _layer_block(norm_g, layer),
        ] + [_layer_block(w, j, col) for col in cols] + [blk[0] for blk in cast_blocks],
        out_specs=out_specs,
        out_shape=out_shape,
        compiler_params=_params(2),
        name=name,
    )(x, mods, norm_g, *([w] * len(cols)), *casts)


SLAB_TILES = SLAB_ROWS // 2


def _slab_base(pair_row0, n_rows):
    return jnp.clip(pair_row0 - WIN_ROWS // 2, 0, n_rows - SLAB_ROWS)


def _bias_tile_plan(n_rows):
    tiles, ids = [], []
    for r0 in range(0, n_rows, 2):
        base = int(np.clip(r0 - WIN_ROWS // 2, 0, n_rows - SLAB_ROWS))
        for j in range(SLAB_TILES):
            quad = []
            for key_row in (base + 2 * j, base + 2 * j + 1):
                for r in (r0, r0 + 1):
                    start = int(np.clip(r - WIN_ROWS // 2, 0, n_rows - WIN_ROWS))
                    assert base <= start and start + WIN_ROWS <= base + SLAB_ROWS
                    inside = start <= key_row < start + WIN_ROWS
                    quad.append(key_row - r + WIN_ROWS - 1 if inside else None)
            quad = tuple(quad)
            if quad not in tiles:
                tiles.append(quad)
            ids.append(tiles.index(quad))
    return tiles, np.asarray(ids, np.int32)


def _bias_tiles_kernel(w_ref, o_ref, *, tiles):
    k_col = lax.broadcasted_iota(jnp.int32, (GRID_W, LANES), 0)
    lane = lax.broadcasted_iota(jnp.int32, (GRID_W, LANES), 1)
    q_col = lane % GRID_W
    c_start = jnp.clip(q_col - WIN_COLS // 2, 0, GRID_W - WIN_COLS)
    inside = (k_col >= c_start) & (k_col < c_start + WIN_COLS)
    first_query_row = lane < GRID_W
    masked = jnp.full((GRID_W, LANES), MASK_VALUE, F32)

    def toeplitz(head, d, shift):
        if d is None:
            return masked
        row = jnp.broadcast_to(w_ref[head, d:d + 1, :], (GRID_W, LANES))
        return pltpu.roll(row, shift, axis=1, stride=1, stride_axis=0) * LOG2E

    for head in range(w_ref.shape[0]):
        for tile_id, quad in enumerate(tiles):
            halves = [jnp.where(inside,
                                jnp.where(first_query_row, toeplitz(head, quad[2 * kr], 0),
                                          toeplitz(head, quad[2 * kr + 1], GRID_W)),
                                masked) for kr in range(2)]
            o_ref[head, tile_id] = jnp.concatenate(halves, axis=0)


def _attention_bias(rpb, tiles):
    h, n_dr, n_dc = rpb.shape
    assert n_dr == 2 * WIN_ROWS - 1 and n_dc == 2 * WIN_COLS - 1
    rev = rpb[..., ::-1]
    w = jnp.concatenate([rev[..., WIN_COLS - 1:], jnp.zeros((h, n_dr, LANES - n_dc), F32),
                         rev[..., :WIN_COLS - 1]], axis=-1)
    w = jnp.pad(w, ((0, 0), (0, 2 * WIN_ROWS - n_dr), (0, 0)))
    return pl.pallas_call(
        functools.partial(_bias_tiles_kernel, tiles=tiles),
        grid=(h // BIAS_HEADS_PER_STEP,),
        in_specs=[pl.BlockSpec((BIAS_HEADS_PER_STEP, 2 * WIN_ROWS, LANES), lambda i: (i, 0, 0))],
        out_specs=pl.BlockSpec((BIAS_HEADS_PER_STEP, len(tiles), LANES, LANES),
                               lambda i: (i, 0, 0, 0)),
        out_shape=jax.ShapeDtypeStruct((h, len(tiles), LANES, LANES), F32),
        compiler_params=_params(1),
        name="bias_tiles",
    )(w)


def _attention_kernel(ids_ref, q_ref, k_ref, v_ref, kc_ref, vc_ref, g_ref, x_ref, mod_ref,
                      bias_ref, wout_ref, o_ref, s_scr, p_scr, r_scr, o_scr, *, n_rows):
    pair_tokens = 2 * GRID_W
    n_ctx = kc_ref.shape[2]
    slab_keys = SLAB_ROWS * GRID_W
    n_items = (ATT_ROWS // 2) * N_HEAD_PAIRS
    first_head = lax.broadcasted_iota(jnp.int32, (LANES, pair_tokens), 0) < HEAD_DIM

    def locate(item):
        hp, pb = item // (ATT_ROWS // 2), item % (ATT_ROWS // 2)
        pair = pl.program_id(1) * (ATT_ROWS // 2) + pb
        k_off = pl.multiple_of(_slab_base(2 * pair, n_rows) * GRID_W, pair_tokens)
        return hp, pair, pb * pair_tokens, k_off

    def scores(item, slot):
        hp, _, q_off, k_off = locate(item)
        qt = q_ref[0, hp, q_off // pair_tokens]
        zero = jnp.zeros_like(qt)
        q2 = jnp.concatenate([jnp.where(first_head, qt, zero),
                              jnp.where(first_head, zero, qt)], axis=1)
        s_scr[slot, :n_ctx, :] = jnp.dot(kc_ref[0, hp], q2, preferred_element_type=F32)
        s_scr[slot, n_ctx:, :] = jnp.dot(k_ref[0, hp, pl.ds(k_off, slab_keys), :], q2,
                                         preferred_element_type=F32)

    def softmax(item, slot):
        hp, pair, _, _ = locate(item)
        bias = jnp.concatenate(
            [jnp.concatenate([bias_ref[2 * hp + a, ids_ref[pair * SLAB_TILES + j]]
                              for a in range(2)], axis=1)
             for j in range(SLAB_TILES)], axis=0)
        s_ctx = s_scr[slot, :n_ctx, :]
        s_loc = s_scr[slot, n_ctx:, :] + bias
        m = jnp.maximum(jnp.max(s_loc, axis=0, keepdims=True),
                        jnp.max(s_ctx, axis=0, keepdims=True))
        p_ctx = jnp.exp2((s_ctx - m).astype(BF16))
        p_loc = jnp.exp2((s_loc - m).astype(BF16))
        denom = (jnp.sum(p_loc.astype(F32), axis=0, keepdims=True)
                 + jnp.sum(p_ctx.astype(F32), axis=0, keepdims=True))
        p_scr[slot, :n_ctx, :] = p_ctx
        p_scr[slot, n_ctx:, :] = p_loc
        r_scr[slot] = jnp.broadcast_to(1.0 / denom, r_scr.shape[1:])

    def values(item, slot):
        hp, _, q_off, k_off = locate(item)
        vc = jnp.concatenate([vc_ref[0, hp, j] for j in range(n_ctx // LANES)], axis=1)
        v_block = lax.shift_right_logical(k_off, LANES.bit_length() - 1)
        vs = jnp.concatenate([v_ref[0, hp, v_block + j] for j in range(slab_keys // LANES)],
                             axis=1)
        pv = (jnp.dot(vc, p_scr[slot, :n_ctx, :], preferred_element_type=F32)
              + jnp.dot(vs, p_scr[slot, n_ctx:, :], preferred_element_type=F32))
        pv = pv * r_scr[slot, 0:1, :]
        out = jnp.where(first_head, pv[:, :pair_tokens], pv[:, pair_tokens:])
        o_scr[hp, pl.ds(q_off, pair_tokens), :] = out.T

    for t in range(-2, n_items):
        if t + 2 < n_items:
            scores(t + 2, (t + 2) % PIPE_SLOTS)
        if 0 <= t + 1 < n_items:
            softmax(t + 1, (t + 1) % PIPE_SLOTS)
        if t >= 0:
            values(t, t % PIPE_SLOTS)

    gate = mod_ref[0, 0, 2:3, :]
    o = jnp.concatenate([o_scr[hp] for hp in range(N_HEAD_PAIRS)], axis=1)
    y = (o * _silu(g_ref[0])).astype(BF16)
    y = jnp.dot(y, wout_ref[0], preferred_element_type=F32)
    o_ref[0] = x_ref[0] + gate * y


def _attention_layer(x, q, k, v, kc, vc, g, mods, layer, rpb, w_out, j):
    b, seq_len, d = x.shape
    n_ctx = kc.shape[2] // b
    n_rows = seq_len // GRID_W
    assert n_rows % ATT_ROWS == 0 and n_rows >= SLAB_ROWS and ATT_ROWS % 2 == 0
    tile = ATT_ROWS * GRID_W
    n_keys = n_ctx + SLAB_ROWS * GRID_W
    tiles, ids = _bias_tile_plan(n_rows)
    bias = _attention_bias(rpb, tiles)
    tile_spec = pl.BlockSpec((1, tile, d), lambda bi, i: (bi, i, 0))
    q_spec = pl.BlockSpec((1, N_HEAD_PAIRS, tile // LANES, LANES, LANES),
                          lambda bi, i: (bi, 0, i, 0, 0))
    seq_spec = pl.BlockSpec((1, N_HEAD_PAIRS, seq_len, LANES), lambda bi, i: (bi, 0, 0, 0))
    ctx_spec = pl.BlockSpec((1, N_HEAD_PAIRS, n_ctx, LANES), lambda bi, i: (0, 0, bi, 0))
    seq_t_spec = pl.BlockSpec((1, N_HEAD_PAIRS, seq_len // LANES, LANES, LANES),
                              lambda bi, i: (bi, 0, 0, 0, 0))
    ctx_t_spec = pl.BlockSpec((1, N_HEAD_PAIRS, n_ctx // LANES, LANES, LANES),
                              lambda bi, i: (0, 0, bi, 0, 0))
    return pl.pallas_call(
        functools.partial(_attention_kernel, n_rows=n_rows),
        grid=(b, n_rows // ATT_ROWS),
        in_specs=[pl.BlockSpec(memory_space=pltpu.SMEM),
                  q_spec, seq_spec, seq_t_spec, ctx_spec, ctx_t_spec, tile_spec, tile_spec,
                  _mod_block(mods, layer, None),
                  pl.BlockSpec(bias.shape, lambda bi, i: (0, 0, 0, 0),
                               pipeline_mode=pl.Buffered(1)),
                  _layer_block(w_out, j)],
        out_specs=tile_spec,
        out_shape=jax.ShapeDtypeStruct(x.shape, F32),
        scratch_shapes=[pltpu.VMEM((PIPE_SLOTS, n_keys, 4 * GRID_W), F32),
                        pltpu.VMEM((PIPE_SLOTS, n_keys, 4 * GRID_W), BF16),
                        pltpu.VMEM((PIPE_SLOTS, SUBLANES, 4 * GRID_W), F32),
                        pltpu.VMEM((N_HEAD_PAIRS, tile, LANES), F32)],
        compiler_params=_params(2),
        name="na_attention",
    )(jnp.asarray(ids), q, k, v, kc, vc, g, x, mods, bias, w_out)


def kernel(x, c, ctx, c_ctx, norm_g, ada_w, ada_b, pool_w_in, pool_w_grp, pool_scale, pool_w_out,
           na_w_in, na_rpb, na_w_out, conv_w_in, conv_dw, conv_db, conv_w_out, final_g):
    depth = norm_g.shape[0]
    batch, _, d = x.shape
    assert batch < COND_ROWS and WIN_ROWS // 2 <= HALO

    cond = jnp.zeros((COND_ROWS, d), F32).at[:batch].set(c).at[batch].set(c_ctx)
    mods = _modulation(cond, ada_w, ada_b).reshape(depth, COND_ROWS, 3, d)
    norm_g = norm_g.reshape(depth, 1, d)

    f32_weights = {1: (na_w_in, na_w_out), 2: (conv_w_in, conv_w_out)}
    bf16_weights = {0: _fold_pool_weights(pool_w_in, pool_w_grp, pool_scale, pool_w_out)}

    def weights_of(kind):
        if kind not in bf16_weights:
            bf16_weights[kind] = tuple(w.astype(BF16) for w in f32_weights[kind])
        return bf16_weights[kind]

    def casts_for_next(i):
        kind = (i + 1) % N_MIXERS
        return f32_weights[kind] if i + 1 < depth and kind not in bf16_weights else ()

    last_ctx_reader = max([i for i in range(depth) if i % N_MIXERS == 1], default=-1)
    for i in range(depth):
        kind, j = i % N_MIXERS, i // N_MIXERS
        update_ctx = i < last_ctx_reader
        fg = final_g if i == depth - 1 else None
        w_bf16 = weights_of(kind)
        casts = casts_for_next(i)
        if kind != 1:
            if kind == 0:
                kernel_fn, weights = _pool_kernel, w_bf16
            else:
                kernel_fn = _conv_kernel
                weights = (w_bf16[0], conv_dw, conv_db.reshape(-1, 1, WIDTH), w_bf16[1])
            run = functools.partial(_mixer_layer, kernel_fn, mods=mods, layer=i, norm_g=norm_g,
                                    weights=weights, j=j)
            if update_ctx:
                ctx = run(x=ctx, mod_row=batch, final_g=None, tile=CTX_TILE, name=f"ctx_layer{i}")
            x = run(x=x, mod_row=None, final_g=fg, tile=X_TILE, name=f"layer{i}", casts=casts)
            if casts:
                x, *cast_out = x
        else:
            if update_ctx:
                raise NotImplementedError("context output of a neighbourhood-attention layer")
            if fg is not None:
                raise NotImplementedError("final norm after a neighbourhood-attention layer")
            w_in, w_out = w_bf16
            q, k, v, g, *cast_out = _project(
                x, mods, i, None, norm_g, w_in, j, (0, 1, 2, 3),
                ("pairs_t", "pairs", "pairs_t", "flat"),
                (HEAD_DIM ** -0.5 * LOG2E, 1.0, 1.0, 1.0), X_TILE, f"na_project{i}", casts=casts)
            kc, vc = _project(ctx.reshape(1, -1, d), mods, i, batch, norm_g, w_in, j, (1, 2),
                              ("pairs", "pairs_t"), (1.0, 1.0), X_TILE, f"na_ctx_project{i}")
            x = _attention_layer(x, q, k, v, kc, vc, g, mods, i, na_rpb[j], w_out, j)
        if casts:
            bf16_weights[(i + 1) % N_MIXERS] = tuple(cast_out)
    return x
```
